```python
import math
import jax, jax.numpy as jnp
from jax import lax
import numpy as np

D_MODEL = 1024
BATCH = 8
SEQ = 4096
DEPTH = 4

HEAD_DIM = 64
LRU_WIDTH = D_MODEL // 2
LRU_BLOCKS = LRU_WIDTH // HEAD_DIM
LRU_CONV = 4
LRU_C = 8.0
LRU_MIN_RAD = 0.9
LRU_MAX_RAD = 0.999
FOX_HEADS = (D_MODEL // 2) // HEAD_DIM
FOX_DIM = FOX_HEADS * HEAD_DIM
SWA_HEADS = (D_MODEL // 2) // HEAD_DIM
SWA_KV_HEADS = max(1, SWA_HEADS // 4)
SWA_DIM = SWA_HEADS * HEAD_DIM
SWA_WINDOW = 128
S5_WIDTH = D_MODEL // 2
S5_GROUP = 16
S5_GROUPS = S5_WIDTH // S5_GROUP
S5_STATE = 64
D_FF = 256 * ((8 * D_MODEL // 3 + 255) // 256)
PLE_DIM = 256
ROPE_THETA = 10000.0
QBLOCK = 128
EPS = 1e-6
MACARON = 0.5
OUT_SCALE = 0.5
N_EVEN = (DEPTH + 1) // 2
N_ODD = DEPTH // 2
EV_IN = 2 * LRU_WIDTH + 3 * FOX_DIM + FOX_HEADS
OD_IN = SWA_DIM + 2 * SWA_KV_HEADS * HEAD_DIM + S5_WIDTH
MIX_WIDTH = LRU_WIDTH + FOX_DIM

kernel_name = "hybrid_rglru_fox_swa_s5_macaron"


def rms_norm(x, g):
    x32 = x.astype(jnp.float32)
    y = x32 * lax.rsqrt(jnp.mean(x32 * x32, axis=-1, keepdims=True) + EPS)
    return (y * g.astype(jnp.float32)).astype(x.dtype)


def swiglu(x, wg, wu, wd):
    return (jax.nn.silu(x @ wg) * (x @ wu)) @ wd


def rope(x, pos):
    half = x.shape[-1] // 2
    inv = jnp.power(ROPE_THETA, -jnp.arange(half, dtype=jnp.float32) / half)
    ang = pos.astype(jnp.float32)[:, None] * inv[None, :]
    cos = jnp.cos(ang)[None, :, None, :]
    sin = jnp.sin(ang)[None, :, None, :]
    x32 = x.astype(jnp.float32)
    x1, x2 = x32[..., :half], x32[..., half:]
    return jnp.concatenate([x1 * cos - x2 * sin, x2 * cos + x1 * sin], axis=-1).astype(x.dtype)


def linear_scan_combine(left, right):
    a1, b1 = left
    a2, b2 = right
    return a1 * a2, a2 * b1 + b2


def rg_lru(xa, conv_w, conv_b, wa, ba, wx, bx, lam):
    B_, S_, W = xa.shape
    xp = jnp.pad(xa, ((0, 0), (LRU_CONV - 1, 0), (0, 0)))
    xc = conv_b
    for tap in range(LRU_CONV):
        xc = xc + xp[:, tap:tap + S_] * conv_w[tap]
    xh = xc.reshape(B_, S_, LRU_BLOCKS, W // LRU_BLOCKS)
    r = jax.nn.sigmoid((jnp.einsum('bshi,hij->bshj', xh, wa).reshape(B_, S_, W) + ba).astype(jnp.float32))
    i = jax.nn.sigmoid((jnp.einsum('bshi,hij->bshj', xh, wx).reshape(B_, S_, W) + bx).astype(jnp.float32))
    log_a = -LRU_C * r * jax.nn.softplus(lam.astype(jnp.float32))
    a = jnp.exp(log_a)
    b = jnp.sqrt(-jnp.expm1(2.0 * log_a)) * (i * xc.astype(jnp.float32))
    _, h = lax.associative_scan(linear_scan_combine, (a, b), axis=1)
    return h.astype(xa.dtype)


def fox_attention(q, k, v, f_logit, b_f, qn, kn):
    B_, S_, H, Dh = q.shape
    q = rms_norm(q, qn)
    k = rms_norm(k, kn)
    log_f = jax.nn.log_sigmoid(f_logit.astype(jnp.float32) + b_f.astype(jnp.float32))
    c = jnp.cumsum(log_f, axis=1).transpose(0, 2, 1)
    nb = S_ // QBLOCK
    qb = q.transpose(0, 2, 1, 3).reshape(B_, H, nb, QBLOCK, Dh).transpose(2, 0, 1, 3, 4)
    cb = c.reshape(B_, H, nb, QBLOCK).transpose(2, 0, 1, 3)
    kh = k.transpose(0, 2, 1, 3)
    vh = v.transpose(0, 2, 1, 3)
    kpos = jnp.arange(S_)
    scale = Dh ** -0.5

    def block(args):
        qi, ci, n = args
        s = jnp.einsum('bhqd,bhkd->bhqk', qi, kh).astype(jnp.float32) * scale
        s = s + ci[..., None] - c[:, :, None, :]
        qpos = n * QBLOCK + jnp.arange(QBLOCK)
        mask = kpos[None, :] <= qpos[:, None]
        s = jnp.where(mask, s, -jnp.inf)
        pr = jax.nn.softmax(s, axis=-1)
        return jnp.einsum('bhqk,bhkd->bhqd', pr.astype(vh.dtype), vh)

    o = lax.map(block, (qb, cb, jnp.arange(nb)))
    return o.transpose(1, 0, 3, 2, 4).reshape(B_, S_, H * Dh)


def swa_sink_attention(q, k, v, sinks, qn, kn):
    B_, S_, H, Dh = q.shape
    KVH = k.shape[2]
    G = H // KVH
    W = SWA_WINDOW
    nb = S_ // W
    pos = jnp.arange(S_)
    q = rope(rms_norm(q, qn), pos)
    k = rope(rms_norm(k, kn), pos)
    qb = q.reshape(B_, nb, W, KVH, G, Dh)

    def band(t):
        tp = jnp.pad(t, ((0, 0), (W, 0), (0, 0), (0, 0))).reshape(B_, nb + 1, W, KVH, Dh)
        return jnp.concatenate([tp[:, :-1], tp[:, 1:]], axis=2)

    kb, vb = band(k), band(v)
    s = jnp.einsum('bnqkgd,bnjkd->bnkgqj', qb, kb).astype(jnp.float32) * Dh ** -0.5
    qi = jnp.arange(W)[:, None]
    kj = jnp.arange(2 * W)[None, :]
    diff = qi + W - kj
    key_pos = jnp.arange(nb)[:, None, None] * W - W + kj[None]
    mask = (diff >= 0)[None] & (diff < SWA_WINDOW)[None] & (key_pos >= 0)
    s = jnp.where(mask[None, :, None, None], s, -jnp.inf)
    sink = sinks.astype(jnp.float32).reshape(KVH, G)[None, None, :, :, None, None]
    m = jnp.maximum(jnp.max(s, axis=-1, keepdims=True), sink)
    e = jnp.exp(s - m)
    pr = e / (jnp.sum(e, axis=-1, keepdims=True) + jnp.exp(sink - m))
    o = jnp.einsum('bnkgqj,bnjkd->bnqkgd', pr.astype(vb.dtype), vb)
    return o.reshape(B_, S_, H * Dh)


def s5_glu(u, lam_re, lam_im, log_dt, b_re, b_im, c_re, c_im, d, glu_w, glu_b):
    B_, S_, _ = u.shape
    f32 = jnp.float32
    u32 = u.astype(f32)
    ug = u32.reshape(B_, S_, S5_GROUPS, S5_GROUP)
    lam = lax.complex(lam_re.astype(f32), lam_im.astype(f32))
    dt = jnp.exp(log_dt.astype(f32))[:, None]
    lam_bar = jnp.exp(lam * dt)
    bmat = lax.complex(b_re.astype(f32), b_im.astype(f32))
    b_bar = ((lam_bar - 1.0) / lam)[..., None] * bmat
    bu = jnp.einsum('gpc,bsgc->bsgp', b_bar, ug.astype(jnp.complex64))
    a = jnp.broadcast_to(lam_bar[None, None], (1, S_, S5_GROUPS, S5_STATE))
    _, h = lax.associative_scan(linear_scan_combine, (a, bu), axis=1)
    cmat = lax.complex(c_re.astype(f32), c_im.astype(f32))
    y = jnp.real(jnp.einsum('gcp,bsgp->bsgc', cmat, h)).reshape(B_, S_, S5_WIDTH)
    y = y + d.astype(f32) * u32
    z = jax.nn.gelu(y).astype(u.dtype)
    return z * jax.nn.sigmoid(z @ glu_w + glu_b)


def even_mixer(h, w_in, conv_w, conv_b, wa, ba, wx, bx, lam, b_f, qn, kn, w_out):
    B_, S_, _ = h.shape
    z = h @ w_in
    o1 = LRU_WIDTH
    o2 = o1 + LRU_WIDTH
    o3 = o2 + FOX_DIM
    o4 = o3 + FOX_DIM
    o5 = o4 + FOX_DIM
    xa, ya, q, k, v, f = jnp.split(z, [o1, o2, o3, o4, o5], axis=-1)
    a_out = jax.nn.gelu(ya) * rg_lru(xa, conv_w, conv_b, wa, ba, wx, bx, lam)
    hd = (B_, S_, FOX_HEADS, HEAD_DIM)
    b_out = fox_attention(q.reshape(hd), k.reshape(hd), v.reshape(hd), f, b_f, qn, kn)
    return jnp.concatenate([a_out, b_out], axis=-1) @ w_out


def odd_mixer(h, w_in, qn, kn, sinks, lam_re, lam_im, log_dt, b_re, b_im, c_re, c_im, d,
              glu_w, glu_b, w_out):
    B_, S_, _ = h.shape
    z = h @ w_in
    kvd = SWA_KV_HEADS * HEAD_DIM
    o1 = SWA_DIM
    o2 = o1 + kvd
    o3 = o2 + kvd
    q, k, v, u = jnp.split(z, [o1, o2, o3], axis=-1)
    c_out = swa_sink_attention(q.reshape(B_, S_, SWA_HEADS, HEAD_DIM),
                               k.reshape(B_, S_, SWA_KV_HEADS, HEAD_DIM),
                               v.reshape(B_, S_, SWA_KV_HEADS, HEAD_DIM), sinks, qn, kn)
    d_out = s5_glu(u, lam_re, lam_im, log_dt, b_re, b_im, c_re, c_im, d, glu_w, glu_b)
    return jnp.concatenate([c_out, d_out], axis=-1) @ w_out


def setup_inputs(seed: int = 0) -> dict:
    key = jax.random.key(seed)
    ks = iter(jax.random.split(key, 64))
    f32 = jnp.float32

    def nrm(shape, scale):
        return jax.random.normal(next(ks), shape, f32) * scale

    def gain(shape):
        return 1.0 + 0.02 * jax.random.normal(next(ks), shape, f32)

    D, F, NE, NO = D_MODEL, D_FF, N_EVEN, N_ODD
    x = nrm((BATCH, SEQ, D), 1.0)
    p = nrm((DEPTH, BATCH, SEQ, PLE_DIM), 1.0)
    ffn1_norm = gain((DEPTH, D))
    ffn1_wg = nrm((DEPTH, D, F), D ** -0.5)
    ffn1_wu = nrm((DEPTH, D, F), D ** -0.5)
    ffn1_wd = nrm((DEPTH, F, D), F ** -0.5)
    mix_norm = gain((DEPTH, D))
    ffn2_norm = gain((DEPTH, D))
    ffn2_wg = nrm((DEPTH, D, F), D ** -0.5)
    ffn2_wu = nrm((DEPTH, D, F), D ** -0.5)
    ffn2_wd = nrm((DEPTH, F, D), F ** -0.5)
    ple_w = nrm((DEPTH, PLE_DIM, D), PLE_DIM ** -0.5)
    ple_norm = gain((DEPTH, D))
    ple_gate_norm = gain((DEPTH, D))
    ple_gate_w = nrm((DEPTH, D, D), D ** -0.5)
    ev_w_in = nrm((NE, D, EV_IN), D ** -0.5)
    lru_conv_w = nrm((NE, LRU_CONV, LRU_WIDTH), LRU_CONV ** -0.5)
    lru_conv_b = nrm((NE, LRU_WIDTH), 0.01)
    bs = LRU_WIDTH // LRU_BLOCKS
    lru_wa = nrm((NE, LRU_BLOCKS, bs, bs), bs ** -0.5)
    lru_ba = nrm((NE, LRU_WIDTH), 0.01)
    lru_wx = nrm((NE, LRU_BLOCKS, bs, bs), bs ** -0.5)
    lru_bx = nrm((NE, LRU_WIDTH), 0.01)
    unif = jax.random.uniform(next(ks), (NE, LRU_WIDTH), f32, LRU_MIN_RAD ** 2, LRU_MAX_RAD ** 2)
    lru_lambda = jnp.log(jnp.expm1(-0.5 * jnp.log(unif)))
    fox_bf = jax.random.uniform(next(ks), (NE, FOX_HEADS), f32, 1.0, 5.0)
    fox_q_norm = gain((NE, HEAD_DIM))
    fox_k_norm = gain((NE, HEAD_DIM))
    ev_w_out = nrm((NE, MIX_WIDTH, D), MIX_WIDTH ** -0.5 * OUT_SCALE)
    od_w_in = nrm((NO, D, OD_IN), D ** -0.5)
    swa_q_norm = gain((NO, HEAD_DIM))
    swa_k_norm = gain((NO, HEAD_DIM))
    swa_sinks = nrm((NO, SWA_HEADS), 0.5)
    n_idx = jnp.arange(S5_STATE, dtype=f32)
    s5_lambda_re = -0.5 + nrm((NO, S5_GROUPS, S5_STATE), 0.01)
    s5_lambda_im = jnp.pi * n_idx + nrm((NO, S5_GROUPS, S5_STATE), 0.01)
    s5_log_dt = jax.random.uniform(next(ks), (NO, S5_GROUPS), f32, math.log(1e-3), math.log(1e-1))
    s5_b_re = nrm((NO, S5_GROUPS, S5_STATE, S5_GROUP), (2 * S5_GROUP) ** -0.5)
    s5_b_im = nrm((NO, S5_GROUPS, S5_STATE, S5_GROUP), (2 * S5_GROUP) ** -0.5)
    s5_c_re = nrm((NO, S5_GROUPS, S5_GROUP, S5_STATE), S5_STATE ** -0.5)
    s5_c_im = nrm((NO, S5_GROUPS, S5_GROUP, S5_STATE), S5_STATE ** -0.5)
    s5_d = nrm((NO, S5_WIDTH), 1.0)
    s5_glu_w = nrm((NO, S5_WIDTH, S5_WIDTH), S5_WIDTH ** -0.5)
    s5_glu_b = nrm((NO, S5_WIDTH), 0.01)
    od_w_out = nrm((NO, MIX_WIDTH, D), MIX_WIDTH ** -0.5 * OUT_SCALE)
    return {
        "x": x, "p": p,
        "ffn1_norm": ffn1_norm, "ffn1_wg": ffn1_wg, "ffn1_wu": ffn1_wu, "ffn1_wd": ffn1_wd,
        "mix_norm": mix_norm,
        "ffn2_norm": ffn2_norm, "ffn2_wg": ffn2_wg, "ffn2_wu": ffn2_wu, "ffn2_wd": ffn2_wd,
        "ple_w": ple_w, "ple_norm": ple_norm, "ple_gate_norm": ple_gate_norm, "ple_gate_w": ple_gate_w,
        "ev_w_in": ev_w_in, "lru_conv_w": lru_conv_w, "lru_conv_b": lru_conv_b,
        "lru_wa": lru_wa, "lru_ba": lru_ba, "lru_wx": lru_wx, "lru_bx": lru_bx,
        "lru_lambda": lru_lambda, "fox_bf": fox_bf, "fox_q_norm": fox_q_norm,
        "fox_k_norm": fox_k_norm, "ev_w_out": ev_w_out,
        "od_w_in": od_w_in, "swa_q_norm": swa_q_norm, "swa_k_norm": swa_k_norm,
        "swa_sinks": swa_sinks, "s5_lambda_re": s5_lambda_re, "s5_lambda_im": s5_lambda_im,
        "s5_log_dt": s5_log_dt, "s5_b_re": s5_b_re, "s5_b_im": s5_b_im,
        "s5_c_re": s5_c_re, "s5_c_im": s5_c_im, "s5_d": s5_d,
        "s5_glu_w": s5_glu_w, "s5_glu_b": s5_glu_b, "od_w_out": od_w_out,
    }


def reference(x, p, ffn1_norm, ffn1_wg, ffn1_wu, ffn1_wd, mix_norm,
              ffn2_norm, ffn2_wg, ffn2_wu, ffn2_wd,
              ple_w, ple_norm, ple_gate_norm, ple_gate_w,
              ev_w_in, lru_conv_w, lru_conv_b, lru_wa, lru_ba, lru_wx, lru_bx,
              lru_lambda, fox_bf, fox_q_norm, fox_k_norm, ev_w_out,
              od_w_in, swa_q_norm, swa_k_norm, swa_sinks, s5_lambda_re, s5_lambda_im,
              s5_log_dt, s5_b_re, s5_b_im, s5_c_re, s5_c_im, s5_d,
              s5_glu_w, s5_glu_b, od_w_out):
    for i in range(DEPTH):
        x = x + MACARON * swiglu(rms_norm(x, ffn1_norm[i]), ffn1_wg[i], ffn1_wu[i], ffn1_wd[i])
        h = rms_norm(x, mix_norm[i])
        if i % 2 == 0:
            j = i // 2
            x = x + even_mixer(h, ev_w_in[j], lru_conv_w[j], lru_conv_b[j], lru_wa[j], lru_ba[j],
                               lru_wx[j], lru_bx[j], lru_lambda[j], fox_bf[j],
                               fox_q_norm[j], fox_k_norm[j], ev_w_out[j])
        else:
            j = i // 2
            x = x + odd_mixer(h, od_w_in[j], swa_q_norm[j], swa_k_norm[j], swa_sinks[j],
                              s5_lambda_re[j], s5_lambda_im[j], s5_log_dt[j], s5_b_re[j],
                              s5_b_im[j], s5_c_re[j], s5_c_im[j], s5_d[j],
                              s5_glu_w[j], s5_glu_b[j], od_w_out[j])
        x = x + MACARON * swiglu(rms_norm(x, ffn2_norm[i]), ffn2_wg[i], ffn2_wu[i], ffn2_wd[i])
        e = rms_norm(p[i] @ ple_w[i], ple_norm[i])
        g = jax.nn.sigmoid(rms_norm(x, ple_gate_norm[i]) @ ple_gate_w[i])
        x = x + g * e
    return x
```

```python
import functools

import jax
import jax.numpy as jnp
from jax import lax
from jax.experimental import pallas as pl
from jax.experimental.pallas import tpu as pltpu

F32 = jnp.float32
BF16 = jnp.bfloat16

D_MODEL = 1024
HEAD_DIM = 64
LRU_WIDTH = 512
LRU_BLOCKS = 8
LRU_CONV = 4
LRU_C = 8.0
FOX_HEADS = 8
FOX_DIM = 512
SWA_HEADS = 8
SWA_KV_HEADS = 2
SWA_GROUP = SWA_HEADS // SWA_KV_HEADS
SWA_DIM = 512
SWA_WINDOW = 128
S5_WIDTH = 512
S5_GROUP = 16
S5_GROUPS = 32
S5_STATE = 64
D_FF = 2816
PLE_DIM = 256
ROPE_THETA = 10000.0
EPS = 1e-6
MACARON = 0.5
QK_SCALE = HEAD_DIM ** -0.5

SUBLANES = 8
LANES = 128
ROW_TILE = 512
TIME_CHUNK = 64
ATTN_TILE = 512
S5_LANE_GROUPS = 4
VMEM_LIMIT = 56 * 1024 * 1024


def _dot(a, b):
    return jnp.dot(a, b, preferred_element_type=F32)


def _dot_nt(a, b):
    return lax.dot_general(a, b, (((1,), (1,)), ((), ())), preferred_element_type=F32)


def _rms(x, g):
    ms = jnp.mean(x * x, axis=-1, keepdims=True)
    return x * lax.rsqrt(ms + EPS) * g


def _softplus(x):
    return jnp.maximum(x, 0.0) + jnp.log1p(jnp.exp(-jnp.abs(x)))


def _log_sigmoid(x):
    return -_softplus(-x)


def _cumsum_rows(x):
    n = x.shape[0]
    row = lax.broadcasted_iota(jnp.int32, x.shape, 0)
    d = 1
    while d < n:
        x = x + jnp.where(row >= d, pltpu.roll(x, d, axis=0), 0.0)
        d *= 2
    return x


def _params(*sem):
    return pltpu.CompilerParams(dimension_semantics=sem, vmem_limit_bytes=VMEM_LIMIT)


def _const_spec(shape):
    nd = len(shape)
    return pl.BlockSpec(shape, lambda *_: (0,) * nd, pipeline_mode=pl.Buffered(1))


def _layer_spec(shape, layer):
    nd = len(shape)
    return pl.BlockSpec((None,) + tuple(shape), lambda *_: (layer,) + (0,) * nd,
                        pipeline_mode=pl.Buffered(1))


def _swiglu_update(x, g_ref, wg_ref, wu_ref, wd_ref):
    n = _rms(x, g_ref[...]).astype(BF16)
    hg = _dot(n, wg_ref[...])
    hu = _dot(n, wu_ref[...])
    act = (hg * jax.nn.sigmoid(hg) * hu).astype(BF16)
    return x + MACARON * _dot(act, wd_ref[...])


def _ffn_kernel(x_ref, g_ref, wg_ref, wu_ref, wd_ref, o_ref):
    o_ref[...] = _swiglu_update(x_ref[...], g_ref, wg_ref, wu_ref, wd_ref)


def _ffn_ple_kernel(x_ref, g_ref, wg_ref, wu_ref, wd_ref, p_ref, pw_ref, pn_ref, gn_ref, gw_ref,
                    o_ref):
    x = _swiglu_update(x_ref[...], g_ref, wg_ref, wu_ref, wd_ref)
    e = _rms(_dot(p_ref[...].astype(BF16), pw_ref[...]), pn_ref[...])
    gate = jax.nn.sigmoid(_dot(_rms(x, gn_ref[...]).astype(BF16), gw_ref[...]))
    o_ref[...] = x + gate * e


def _ffn(x2d, layer, norm, wg, wu, wd, ple=None):
    T, D = x2d.shape
    tm = min(ROW_TILE, T)
    row = pl.BlockSpec((tm, D), lambda i: (i, 0))
    in_specs = [row, _layer_spec((1, D), layer), _layer_spec((D, D_FF), layer),
                _layer_spec((D, D_FF), layer), _layer_spec((D_FF, D), layer)]
    args = [x2d, norm, wg, wu, wd]
    kern = _ffn_kernel
    if ple is not None:
        p, pw, pn, gn, gw = ple
        in_specs += [pl.BlockSpec((None, tm, PLE_DIM), lambda i: (layer, i, 0)),
                     _layer_spec((PLE_DIM, D), layer), _layer_spec((1, D), layer),
                     _layer_spec((1, D), layer), _layer_spec((D, D), layer)]
        args += [p, pw, pn, gn, gw]
        kern = _ffn_ple_kernel
    return pl.pallas_call(
        kern, grid=(T // tm,), in_specs=in_specs, out_specs=row,
        out_shape=jax.ShapeDtypeStruct((T, D), F32),
        compiler_params=_params("parallel"), name="ffn_ple" if ple is not None else "ffn",
    )(*args)


def _outproj_kernel(x_ref, tmaj_ref, bmaj_ref, wt_ref, wb_ref, o_ref):
    o_ref[...] = x_ref[...] + _dot(tmaj_ref[...], wt_ref[...]) + _dot(bmaj_ref[...], wb_ref[...])


def _outproj(x, part_tmaj, part_bmaj, w_tmaj, w_bmaj):
    B, S, D = x.shape
    tm = min(ROW_TILE, S)
    xs = pl.BlockSpec((None, tm, D), lambda b, s: (b, s, 0))
    return pl.pallas_call(
        _outproj_kernel, grid=(B, S // tm),
        in_specs=[xs, pl.BlockSpec((tm, 512), lambda b, s: (s, b)),
                  pl.BlockSpec((None, tm, 512), lambda b, s: (b, s, 0)),
                  _const_spec((512, D)), _const_spec((512, D))],
        out_specs=xs, out_shape=jax.ShapeDtypeStruct((B, S, D), F32),
        compiler_params=_params("parallel", "parallel"), name="outproj",
    )(x, part_tmaj, part_bmaj, w_tmaj, w_bmaj)


def _split3_bf16(c):
    hi = c.astype(BF16).astype(F32)
    r = c - hi
    mid = r.astype(BF16).astype(F32)
    return hi, mid, r - mid


def _even_in_kernel(x_ref, g_ref, wxy_ref, wqkv_ref, wf_ref, bf_ref, qn_ref, kn_ref,
                    xa_ref, ya_ref, q_ref, k_ref, v_ref, carry_ref):
    tm = x_ref.shape[0]

    @pl.when(pl.program_id(1) == 0)
    def _():
        carry_ref[...] = jnp.zeros_like(carry_ref)

    n = _rms(x_ref[...], g_ref[...]).astype(BF16)
    xy = _dot(n, wxy_ref[...])
    xa_ref[...] = xy[:, :LRU_WIDTH]
    ya_ref[...] = xy[:, LRU_WIDTH:]

    c = _cumsum_rows(_log_sigmoid(_dot(n, wf_ref[...]) + bf_ref[...])) + carry_ref[...]
    carry_ref[...] = c[tm - 1:tm, :]

    qkv = _dot(n, wqkv_ref[...])
    lane = lax.broadcasted_iota(jnp.int32, (tm, HEAD_DIM), 1)
    for h in range(FOX_HEADS):
        lo_ = h * HEAD_DIM
        qh = _rms(qkv[:, lo_:lo_ + HEAD_DIM], qn_ref[...]) * QK_SCALE
        kh = _rms(qkv[:, FOX_DIM + lo_:FOX_DIM + lo_ + HEAD_DIM], kn_ref[...])
        vh = qkv[:, 2 * FOX_DIM + lo_:2 * FOX_DIM + lo_ + HEAD_DIM]
        hi, mid, lo = _split3_bf16(c[:, h:h + 1])
        q_ext = jnp.where(lane == 0, hi, jnp.where(lane == 1, mid, jnp.where(
            lane == 2, lo, jnp.where(lane < 6, 1.0, 0.0))))
        k_ext = jnp.where(lane < 3, 1.0, jnp.where(lane == 3, -hi, jnp.where(
            lane == 4, -mid, jnp.where(lane == 5, -lo, 0.0))))
        q_ref[h] = jnp.concatenate([qh, q_ext], axis=-1).astype(BF16)
        k_ref[h] = jnp.concatenate([kh, k_ext], axis=-1).astype(BF16)
        v_ref[h] = vh.astype(BF16)


def _even_inproj(x, g, w_xy, w_qkv, w_f, b_f, qn, kn):
    B, S, D = x.shape
    tm = min(ROW_TILE, S)
    H = FOX_HEADS
    tmaj = pl.BlockSpec((tm, LRU_WIDTH), lambda b, s: (s, b))
    head = lambda w: pl.BlockSpec((None, H, tm, w), lambda b, s: (b, 0, s, 0))
    return pl.pallas_call(
        _even_in_kernel, grid=(B, S // tm),
        in_specs=[pl.BlockSpec((None, tm, D), lambda b, s: (b, s, 0)), _const_spec((1, D)),
                  _const_spec(w_xy.shape), _const_spec(w_qkv.shape), _const_spec(w_f.shape),
                  _const_spec((1, LANES)), _const_spec((1, HEAD_DIM)), _const_spec((1, HEAD_DIM))],
        out_specs=[tmaj, tmaj, head(2 * HEAD_DIM), head(2 * HEAD_DIM), head(HEAD_DIM)],
        out_shape=[jax.ShapeDtypeStruct((S, B * LRU_WIDTH), F32),
                   jax.ShapeDtypeStruct((S, B * LRU_WIDTH), F32),
                   jax.ShapeDtypeStruct((B, H, S, 2 * HEAD_DIM), BF16),
                   jax.ShapeDtypeStruct((B, H, S, 2 * HEAD_DIM), BF16),
                   jax.ShapeDtypeStruct((B, H, S, HEAD_DIM), BF16)],
        scratch_shapes=[pltpu.VMEM((1, LANES), F32)],
        compiler_params=_params("parallel", "arbitrary"), name="even_inproj",
    )(x, g, w_xy, w_qkv, w_f, b_f, qn, kn)


def _lru_kernel(xa_ref, ya_ref, cw_ref, cb_ref, wa_ref, ba_ref, wx_ref, bx_ref, lam_ref, o_ref,
                xpad_ref, a_ref, h_ref, carry_ref):
    rows = xa_ref.shape[0]
    halo = (LRU_CONV - 1) * SUBLANES
    half = LRU_WIDTH // 2

    @pl.when(pl.program_id(0) == 0)
    def _():
        xpad_ref[0:halo, :] = jnp.zeros((halo, LRU_WIDTH), F32)
        carry_ref[...] = jnp.zeros_like(carry_ref)

    xpad_ref[halo:halo + rows, :] = xa_ref[...]
    xc = cb_ref[...]
    for tap in range(LRU_CONV):
        xc = xc + xpad_ref[tap * SUBLANES:tap * SUBLANES + rows, :] * cw_ref[tap:tap + 1, :]
    xpad_ref[0:halo, :] = xpad_ref[rows:rows + halo, :]

    xb = xc.astype(BF16)

    def gate(w_ref, b_ref):
        z = jnp.concatenate([_dot(xb[:, :half], w_ref[0]), _dot(xb[:, half:], w_ref[1])], axis=-1)
        return jax.nn.sigmoid(z + b_ref[...])

    r = gate(wa_ref, ba_ref)
    i = gate(wx_ref, bx_ref)
    log_a = -LRU_C * r * _softplus(lam_ref[...])
    a_ref[...] = jnp.exp(log_a)
    th = jnp.tanh(log_a)
    h_ref[...] = jnp.sqrt(-2.0 * th / (1.0 - th)) * (i * xc)

    def step(t, h):
        sl = pl.ds(pl.multiple_of(t * SUBLANES, SUBLANES), SUBLANES)
        h = a_ref[sl, :] * h + h_ref[sl, :]
        h_ref[sl, :] = h
        return h

    carry_ref[...] = lax.fori_loop(0, rows // SUBLANES, step, carry_ref[...], unroll=8)
    o_ref[...] = (jax.nn.gelu(ya_ref[...]) * h_ref[...]).astype(BF16)


def _lru(xa_t, ya_t, conv_w, conv_b, wa_bd, ba, wx_bd, bx, lam):
    R, W = xa_t.shape
    rows = TIME_CHUNK * SUBLANES
    halo = (LRU_CONV - 1) * SUBLANES
    blk = pl.BlockSpec((rows, W), lambda t: (t, 0))
    return pl.pallas_call(
        _lru_kernel, grid=(R // rows,),
        in_specs=[blk, blk, _const_spec((LRU_CONV, W)), _const_spec((1, W)),
                  _const_spec(wa_bd.shape), _const_spec((1, W)),
                  _const_spec(wx_bd.shape), _const_spec((1, W)), _const_spec((1, W))],
        out_specs=blk, out_shape=jax.ShapeDtypeStruct((R, W), BF16),
        scratch_shapes=[pltpu.VMEM((halo + rows, W), F32), pltpu.VMEM((rows, W), F32),
                        pltpu.VMEM((rows, W), F32), pltpu.VMEM((SUBLANES, W), F32)],
        compiler_params=_params("arbitrary"), name="rg_lru",
    )(xa_t, ya_t, conv_w, conv_b, wa_bd, ba, wx_bd, bx, lam)


def _fox_kernel(q_ref, k_ref, v_ref, o_ref, m_ref, l_ref, acc_ref):
    tq = q_ref.shape[1]
    qi = pl.program_id(2)
    row = lax.broadcasted_iota(jnp.int32, (tq, tq), 0)
    col = lax.broadcasted_iota(jnp.int32, (tq, tq), 1)
    outs = []
    for hh in range(2):
        q = q_ref[hh]
        m_ref[...] = jnp.full(m_ref.shape, -jnp.inf, F32)
        l_ref[...] = jnp.zeros_like(l_ref)
        acc_ref[...] = jnp.zeros_like(acc_ref)

        def update(j, masked):
            ks = pl.ds(pl.multiple_of(j * tq, tq), tq)
            s = _dot_nt(q, k_ref[hh, ks, :])
            if masked:
                s = jnp.where(col <= row, s, -jnp.inf)
            m_old = m_ref[...]
            m_new = jnp.maximum(m_old, jnp.max(s, axis=-1, keepdims=True))
            alpha = jnp.exp(m_old - m_new)
            p = jnp.exp(s - m_new)
            l_ref[...] = alpha * l_ref[...] + jnp.sum(p, axis=-1, keepdims=True)
            acc_ref[...] = alpha * acc_ref[...] + _dot(p.astype(BF16), v_ref[hh, ks, :])
            m_ref[...] = m_new

        def body(j, carry):
            update(j, False)
            return carry

        lax.fori_loop(0, qi, body, 0)
        update(qi, True)
        outs.append(acc_ref[...] / l_ref[...])
    o_ref[...] = jnp.concatenate(outs, axis=-1).astype(BF16)


def _fox_attention(q_aug, k_aug, v):
    B, H, S, _ = q_aug.shape
    tq = min(ATTN_TILE, S)
    return pl.pallas_call(
        _fox_kernel, grid=(B, H // 2, S // tq),
        in_specs=[pl.BlockSpec((None, 2, tq, 2 * HEAD_DIM), lambda b, p, i: (b, p, i, 0)),
                  pl.BlockSpec((None, 2, S, 2 * HEAD_DIM), lambda b, p, i: (b, p, 0, 0)),
                  pl.BlockSpec((None, 2, S, HEAD_DIM), lambda b, p, i: (b, p, 0, 0))],
        out_specs=pl.BlockSpec((None, tq, 2 * HEAD_DIM), lambda b, p, i: (b, i, p)),
        out_shape=jax.ShapeDtypeStruct((B, S, H * HEAD_DIM), BF16),
        scratch_shapes=[pltpu.VMEM((tq, 1), F32), pltpu.VMEM((tq, 1), F32),
                        pltpu.VMEM((tq, HEAD_DIM), F32)],
        compiler_params=_params("parallel", "parallel", "arbitrary"), name="fox_attention",
    )(q_aug, k_aug, v)


def _rope_table_kernel(inv_ref, cos_ref, sin_ref):
    rows = cos_ref.shape[0]
    pos = pl.program_id(0) * rows + lax.broadcasted_iota(jnp.int32, cos_ref.shape, 0)
    lane = lax.broadcasted_iota(jnp.int32, cos_ref.shape, 1)
    ang = pos.astype(F32) * inv_ref[...]
    cos_ref[...] = jnp.cos(ang)
    sin_ref[...] = jnp.where(lane < HEAD_DIM // 2, -1.0, 1.0) * jnp.sin(ang)


def _rope_tables(S):
    half = HEAD_DIM // 2
    inv = jnp.power(ROPE_THETA, -jnp.arange(half, dtype=F32) / half)
    inv2 = jnp.concatenate([inv, inv])[None, :]
    rows = min(ROW_TILE, S)
    blk = pl.BlockSpec((rows, HEAD_DIM), lambda i: (i, 0))
    return pl.pallas_call(
        _rope_table_kernel, grid=(S // rows,),
        in_specs=[pl.BlockSpec((1, HEAD_DIM), lambda i: (0, 0))], out_specs=[blk, blk],
        out_shape=[jax.ShapeDtypeStruct((S, HEAD_DIM), F32)] * 2,
        compiler_params=_params("parallel"), name="rope_tables",
    )(inv2)


def _rope(x, cos2, sin2):
    half = HEAD_DIM // 2
    swapped = jnp.concatenate([x[:, half:], x[:, :half]], axis=-1)
    return x * cos2 + swapped * sin2


def _odd_in_kernel(x_ref, g_ref, wq_ref, wkv_ref, wu_ref, qn_ref, kn_ref, cos_ref, sin_ref,
                   q_ref, k_ref, v_ref, u_ref):
    n = _rms(x_ref[...], g_ref[...]).astype(BF16)
    u_ref[...] = _dot(n, wu_ref[...])
    cos2 = cos_ref[...]
    sin2 = sin_ref[...]
    q = _dot(n, wq_ref[...])
    for h in range(SWA_HEADS):
        qh = _rms(q[:, h * HEAD_DIM:(h + 1) * HEAD_DIM], qn_ref[...])
        q_ref[h] = (_rope(qh, cos2, sin2) * QK_SCALE).astype(BF16)
    kv = _dot(n, wkv_ref[...])
    kvd = SWA_KV_HEADS * HEAD_DIM
    for h in range(SWA_KV_HEADS):
        kh = _rms(kv[:, h * HEAD_DIM:(h + 1) * HEAD_DIM], kn_ref[...])
        k_ref[h] = _rope(kh, cos2, sin2).astype(BF16)
        v_ref[h] = kv[:, kvd + h * HEAD_DIM:kvd + (h + 1) * HEAD_DIM].astype(BF16)


def _odd_inproj(x, g, w_q, w_kv, w_u, qn, kn, cos2, sin2):
    B, S, D = x.shape
    tm = min(ROW_TILE, S)
    head = lambda nh: pl.BlockSpec((None, nh, tm, HEAD_DIM), lambda b, s: (b, 0, s, 0))
    tab = pl.BlockSpec((tm, HEAD_DIM), lambda b, s: (s, 0))
    return pl.pallas_call(
        _odd_in_kernel, grid=(B, S // tm),
        in_specs=[pl.BlockSpec((None, tm, D), lambda b, s: (b, s, 0)), _const_spec((1, D)),
                  _const_spec(w_q.shape), _const_spec(w_kv.shape), _const_spec(w_u.shape),
                  _const_spec((1, HEAD_DIM)), _const_spec((1, HEAD_DIM)), tab, tab],
        out_specs=[head(SWA_HEADS), head(SWA_KV_HEADS), head(SWA_KV_HEADS),
                   pl.BlockSpec((tm, S5_WIDTH), lambda b, s: (s, b))],
        out_shape=[jax.ShapeDtypeStruct((B, SWA_HEADS, S, HEAD_DIM), BF16),
                   jax.ShapeDtypeStruct((B, SWA_KV_HEADS, S, HEAD_DIM), BF16),
                   jax.ShapeDtypeStruct((B, SWA_KV_HEADS, S, HEAD_DIM), BF16),
                   jax.ShapeDtypeStruct((S, B * S5_WIDTH), F32)],
        compiler_params=_params("parallel", "parallel"), name="odd_inproj",
    )(x, g, w_q, w_kv, w_u, qn, kn, cos2, sin2)


def _swa_kernel(q_ref, k_ref, v_ref, sink_ref, o_ref):
    tq = q_ref.shape[1]
    W = SWA_WINDOW
    G = SWA_GROUP
    base = pl.program_id(1) * tq
    row = lax.broadcasted_iota(jnp.int32, (G * W, 2 * W), 0)
    col = lax.broadcasted_iota(jnp.int32, (G * W, 2 * W), 1)
    qoff = row % W
    rgrp = lax.broadcasted_iota(jnp.int32, (G * W, 1), 0) // W
    for n in range(tq // W):
        r0 = base + n * W
        kstart = pl.multiple_of(jnp.maximum(r0 - W, 0), W)
        diff = (r0 + qoff) - (kstart + col)
        valid = (diff >= 0) & (diff < W)
        heads = [None] * SWA_HEADS
        for kvh in range(SWA_KV_HEADS):
            q4 = jnp.concatenate([q_ref[kvh * G + g, n * W:(n + 1) * W, :] for g in range(G)], axis=0)
            s = _dot_nt(q4, k_ref[kvh, pl.ds(kstart, 2 * W), :])
            s = jnp.where(valid, s, -jnp.inf)
            sink = jnp.zeros((G * W, 1), F32)
            for g in range(G):
                hidx = kvh * G + g
                sink = jnp.where(rgrp == g, sink_ref[:, hidx:hidx + 1], sink)
            m = jnp.maximum(jnp.max(s, axis=-1, keepdims=True), sink)
            e = jnp.exp(s - m)
            pr = e / (jnp.sum(e, axis=-1, keepdims=True) + jnp.exp(sink - m))
            o = _dot(pr.astype(BF16), v_ref[kvh, pl.ds(kstart, 2 * W), :])
            for g in range(G):
                heads[kvh * G + g] = o[g * W:(g + 1) * W, :]
        o_ref[n * W:(n + 1) * W, :] = jnp.concatenate(heads, axis=-1).astype(BF16)


def _swa_attention(q, k, v, sinks):
    B, H, S, Dh = q.shape
    tq = min(ATTN_TILE, S)
    return pl.pallas_call(
        _swa_kernel, grid=(B, S // tq),
        in_specs=[pl.BlockSpec((None, H, tq, Dh), lambda b, i: (b, 0, i, 0)),
                  pl.BlockSpec((None, SWA_KV_HEADS, S, Dh), lambda b, i: (b, 0, 0, 0)),
                  pl.BlockSpec((None, SWA_KV_HEADS, S, Dh), lambda b, i: (b, 0, 0, 0)),
                  pl.BlockSpec((1, H), lambda b, i: (0, 0))],
        out_specs=pl.BlockSpec((None, tq, H * Dh), lambda b, i: (b, i, 0)),
        out_shape=jax.ShapeDtypeStruct((B, S, H * Dh), BF16),
        compiler_params=_params("parallel", "arbitrary"), name="swa_attention",
    )(q, k, v, sinks)


def _s5_prep_kernel(lr_ref, li_ref, ldt_ref, br_ref, bi_ref, ar_ref, ai_ref, bbr_ref, bbi_ref):
    lr, li = lr_ref[...], li_ref[...]
    dt = jnp.exp(ldt_ref[...])
    mag = jnp.exp(lr * dt)
    ar = mag * jnp.cos(li * dt)
    ai = mag * jnp.sin(li * dt)
    den = lr * lr + li * li
    cr = ((ar - 1.0) * lr + ai * li) / den
    ci = (ai * lr - (ar - 1.0) * li) / den
    br, bi = br_ref[...], bi_ref[...]
    ar_ref[...] = ar
    ai_ref[...] = ai
    bbr_ref[...] = cr * br - ci * bi
    bbi_ref[...] = cr * bi + ci * br


def _s5_prep(lam_re, lam_im, log_dt, b_re, b_im):
    G, P, C = b_re.shape
    rep = lambda a: jnp.repeat(a, C, axis=0)
    bt = lambda a: a.transpose(0, 2, 1).reshape(G * C, P)
    ldt = jnp.broadcast_to(log_dt[:, None], (G, P))
    full = pl.BlockSpec((G * C, P), lambda: (0, 0))
    outs = pl.pallas_call(
        _s5_prep_kernel, in_specs=[full] * 5, out_specs=[full] * 4,
        out_shape=[jax.ShapeDtypeStruct((G * C, P), F32)] * 4, name="s5_prep",
    )(rep(lam_re), rep(lam_im), rep(ldt), bt(b_re), bt(b_im))
    ar, ai, bbr, bbi = [o.reshape(G, C, P) for o in outs]
    return ar[:, 0], ai[:, 0], bbr, bbi


def _s5_kernel(u_ref, bm_ref, cm_ref, ar_ref, ai_ref, d_ref, gw_ref, gb_ref, o_ref,
               h_ref, carry_ref):
    rows = u_ref.shape[0]
    half = h_ref.shape[2] // 2

    @pl.when(pl.program_id(0) == 0)
    def _():
        carry_ref[...] = jnp.zeros_like(carry_ref)

    u = u_ref[...]
    ub = u.astype(BF16)
    ys = []
    for g in range(S5_LANE_GROUPS):
        h_ref[g] = _dot(ub[:, g * LANES:(g + 1) * LANES], bm_ref[g])
        ar = jnp.broadcast_to(ar_ref[g], (SUBLANES, half))
        ai = jnp.broadcast_to(ai_ref[g], (SUBLANES, half))

        def step(t, carry, g=g, ar=ar, ai=ai):
            hr, hi = carry
            sl = pl.ds(pl.multiple_of(t * SUBLANES, SUBLANES), SUBLANES)
            nr = ar * hr - ai * hi + h_ref[g, sl, 0:half]
            ni = ar * hi + ai * hr + h_ref[g, sl, half:2 * half]
            h_ref[g, sl, 0:half] = nr
            h_ref[g, sl, half:2 * half] = ni
            return nr, ni

        hr, hi = lax.fori_loop(0, rows // SUBLANES, step,
                               (carry_ref[g, :, 0:half], carry_ref[g, :, half:2 * half]), unroll=8)
        carry_ref[g, :, 0:half] = hr
        carry_ref[g, :, half:2 * half] = hi
        ys.append(_dot(h_ref[g].astype(BF16), cm_ref[g]))
    y = jnp.concatenate(ys, axis=-1) + d_ref[...] * u
    z = jax.nn.gelu(y)
    o_ref[...] = (z * jax.nn.sigmoid(_dot(z.astype(BF16), gw_ref[...]) + gb_ref[...])).astype(BF16)


def _s5(u_t, bmat, cmat, ar, ai, d, glu_w, glu_b):
    R, W = u_t.shape
    rows = TIME_CHUNK * SUBLANES
    nstate = bmat.shape[2]
    blk = pl.BlockSpec((rows, W), lambda t: (t, 0))
    return pl.pallas_call(
        _s5_kernel, grid=(R // rows,),
        in_specs=[blk, _const_spec(bmat.shape), _const_spec(cmat.shape), _const_spec(ar.shape),
                  _const_spec(ai.shape), _const_spec((1, W)), _const_spec((W, W)),
                  _const_spec((1, W))],
        out_specs=blk, out_shape=jax.ShapeDtypeStruct((R, W), BF16),
        scratch_shapes=[pltpu.VMEM((S5_LANE_GROUPS, rows, nstate), F32),
                        pltpu.VMEM((S5_LANE_GROUPS, SUBLANES, nstate), F32)],
        compiler_params=_params("arbitrary"), name="s5_glu",
    )(u_t, bmat, cmat, ar, ai, d, glu_w, glu_b)


def _s5_matrices(ar, ai, bbr, bbi, c_re, c_im):
    L, GL = S5_LANE_GROUPS, S5_GROUPS // S5_LANE_GROUPS
    C, P = S5_GROUP, S5_STATE
    eye = jnp.eye(GL, dtype=F32)

    def inmap(b):
        return jnp.einsum("lgcp,gh->lgchp", b.reshape(L, GL, C, P), eye).reshape(L, GL * C, GL * P)

    def outmap(c):
        return jnp.einsum("lgcp,gh->lgphc", c.reshape(L, GL, C, P), eye).reshape(L, GL * P, GL * C)

    bmat = jnp.concatenate([inmap(bbr), inmap(bbi)], axis=2).astype(BF16)
    cmat = jnp.concatenate([outmap(c_re), outmap(-c_im)], axis=1).astype(BF16)
    a_r = ar.reshape(L, 1, GL * P)
    a_i = ai.reshape(L, 1, GL * P)
    return bmat, cmat, a_r, a_i


def _block_diag_pairs(w):
    nb, bs, _ = w.shape
    half = nb // 2
    eye = jnp.eye(half, dtype=w.dtype)
    out = jnp.einsum("thij,hk->thikj", w.reshape(2, half, bs, bs), eye)
    return out.reshape(2, half * bs, half * bs).astype(BF16)


def kernel(x, p, ffn1_norm, ffn1_wg, ffn1_wu, ffn1_wd, mix_norm, ffn2_norm, ffn2_wg, ffn2_wu, ffn2_wd, ple_w, ple_norm, ple_gate_norm, ple_gate_w, ev_w_in, lru_conv_w, lru_conv_b, lru_wa, lru_ba, lru_wx, lru_bx, lru_lambda, fox_bf, fox_q_norm, fox_k_norm, ev_w_out, od_w_in, swa_q_norm, swa_k_norm, swa_sinks, s5_lambda_re, s5_lambda_im, s5_log_dt, s5_b_re, s5_b_im, s5_c_re, s5_c_im, s5_d, s5_glu_w, s5_glu_b, od_w_out):
    B, S, D = x.shape
    depth = p.shape[0]
    assert B == SUBLANES and D == D_MODEL and S % TIME_CHUNK == 0
    T = B * S
    bf = lambda a: a.astype(BF16)
    row = lambda a: a[:, None, :]

    f1 = (row(ffn1_norm), bf(ffn1_wg), bf(ffn1_wu), bf(ffn1_wd))
    f2 = (row(ffn2_norm), bf(ffn2_wg), bf(ffn2_wu), bf(ffn2_wd))
    ple = (p.reshape(depth, T, PLE_DIM), bf(ple_w), row(ple_norm), row(ple_gate_norm),
           bf(ple_gate_w))
    cos2, sin2 = _rope_tables(S)

    for i in range(depth):
        j = i // 2
        x = _ffn(x.reshape(T, D), i, *f1).reshape(B, S, D)
        g = mix_norm[i][None, :]
        if i % 2 == 0:
            w_in = bf(ev_w_in[j])
            o1, o2 = 2 * LRU_WIDTH, 2 * LRU_WIDTH + 3 * FOX_DIM
            w_f = jnp.pad(w_in[:, o2:], ((0, 0), (0, LANES - FOX_HEADS)))
            b_f = jnp.pad(fox_bf[j], (0, LANES - FOX_HEADS))[None, :]
            xa, ya, q_aug, k_aug, v = _even_inproj(
                x, g, w_in[:, :o1], w_in[:, o1:o2], w_f, b_f,
                fox_q_norm[j][None, :], fox_k_norm[j][None, :])
            a_out = _lru(xa.reshape(S * B, LRU_WIDTH), ya.reshape(S * B, LRU_WIDTH),
                         lru_conv_w[j], lru_conv_b[j][None, :],
                         _block_diag_pairs(lru_wa[j]), lru_ba[j][None, :],
                         _block_diag_pairs(lru_wx[j]), lru_bx[j][None, :],
                         lru_lambda[j][None, :])
            b_out = _fox_attention(q_aug, k_aug, v)
            w_out = bf(ev_w_out[j])
            x = _outproj(x, a_out.reshape(S, B * LRU_WIDTH), b_out,
                         w_out[:LRU_WIDTH], w_out[LRU_WIDTH:])
        else:
            w_in = bf(od_w_in[j])
            kvd = SWA_KV_HEADS * HEAD_DIM
            o1, o2 = SWA_DIM, SWA_DIM + 2 * kvd
            q, k, v, u = _odd_inproj(x, g, w_in[:, :o1], w_in[:, o1:o2], w_in[:, o2:],
                                     swa_q_norm[j][None, :], swa_k_norm[j][None, :], cos2, sin2)
            c_out = _swa_attention(q, k, v, swa_sinks[j][None, :])
            ar, ai, bbr, bbi = _s5_prep(s5_lambda_re[j], s5_lambda_im[j], s5_log_dt[j],
                                        s5_b_re[j], s5_b_im[j])
            bmat, cmat, a_r, a_i = _s5_matrices(ar, ai, bbr, bbi, s5_c_re[j], s5_c_im[j])
            d_out = _s5(u.reshape(S * B, S5_WIDTH), bmat, cmat, a_r, a_i, s5_d[j][None, :],
                        bf(s5_glu_w[j]), s5_glu_b[j][None, :])
            w_out = bf(od_w_out[j])
            x = _outproj(x, d_out.reshape(S, B * S5_WIDTH), c_out,
                         w_out[SWA_DIM:], w_out[:SWA_DIM])
        x = _ffn(x.reshape(T, D), i, *f2, ple=ple).reshape(B, S, D)
    return x
```

```python
import numpy as np

import jax
import jax.numpy as jnp
from jax import lax
from jax.experimental import pallas as pl
from jax.experimental.pallas import tpu as pltpu

F32 = jnp.float32
BF16 = jnp.bfloat16

D_MODEL = 1024
HEAD_DIM = 64
LRU_WIDTH = 512
LRU_CONV = 4
LRU_C = 8.0
FOX_HEADS = 8
FOX_DIM = 512
SWA_HEADS = 8
SWA_KV_HEADS = 2
SWA_GROUP = SWA_HEADS // SWA_KV_HEADS
SWA_DIM = 512
SWA_WINDOW = 128
S5_WIDTH = 512
S5_GROUP = 16
S5_GROUPS = 32
S5_STATE = 64
D_FF = 2816
PLE_DIM = 256
ROPE_THETA = 10000.0
EPS = 1e-6
MACARON = 0.5
QK_SCALE = HEAD_DIM ** -0.5

SUBLANES = 8
LANES = 128
ROW_TILE = 512
TIME_CHUNK = 64
ATTN_TILE = ROW_TILE
S5_LANE_GROUPS = 4
BIAS_LANES = 6
VMEM_LIMIT = 56 * 1024 * 1024


def _dot(a, b):
    return jnp.dot(a, b, preferred_element_type=F32)


def _dot_nt(a, b):
    return lax.dot_general(a, b, (((1,), (1,)), ((), ())), preferred_element_type=F32)


def _rms(x, g):
    ms = jnp.mean(x * x, axis=-1, keepdims=True)
    return x * lax.rsqrt(ms + EPS) * g


def _head_rms(x, ind, gain):
    sq = x * x
    hi = sq.astype(BF16)
    lo = (sq - hi.astype(F32)).astype(BF16)
    ms = _dot(hi, ind) + _dot(lo, ind)
    return x * lax.rsqrt(ms + EPS) * gain


def _softplus(x):
    return jnp.maximum(x, 0.0) + jnp.log1p(jnp.exp(-jnp.abs(x)))


def _log_sigmoid(x):
    return -_softplus(-x)


def _cumsum_rows(x):
    n = x.shape[0]
    row = lax.broadcasted_iota(jnp.int32, x.shape, 0)
    d = 1
    while d < n:
        x = x + jnp.where(row >= d, pltpu.roll(x, d, axis=0), 0.0)
        d *= 2
    return x


def _split3_bf16(c):
    hi = c.astype(BF16).astype(F32)
    r = c - hi
    mid = r.astype(BF16).astype(F32)
    return hi, mid, r - mid


def _keep_lanes(x, lo, hi):
    lane = lax.broadcasted_iota(jnp.int32, x.shape, 1)
    return jnp.where((lane >= lo) & (lane < hi), x.astype(F32), 0.0).astype(BF16)


def _params(*sem):
    return pltpu.CompilerParams(dimension_semantics=sem, vmem_limit_bytes=VMEM_LIMIT)


def _const_spec(shape):
    nd = len(shape)
    return pl.BlockSpec(shape, lambda *_: (0,) * nd, pipeline_mode=pl.Buffered(1))


def _layer_spec(shape, layer):
    nd = len(shape)
    return pl.BlockSpec((None,) + tuple(shape), lambda *_: (layer,) + (0,) * nd,
                        pipeline_mode=pl.Buffered(1))


def _head_mean_matrix(width):
    h = np.arange(width) // HEAD_DIM
    return jnp.asarray((h[:, None] == h[None, :]) / HEAD_DIM, dtype=BF16)


def _swiglu_update(x, g_ref, wg_ref, wu_ref, wd_ref):
    n = _rms(x, g_ref[...]).astype(BF16)
    hg = _dot(n, wg_ref[...])
    hu = _dot(n, wu_ref[...])
    act = (hg * jax.nn.sigmoid(hg) * hu).astype(BF16)
    return x + MACARON * _dot(act, wd_ref[...])


def _ffn_kernel(x_ref, g_ref, wg_ref, wu_ref, wd_ref, o_ref):
    o_ref[...] = _swiglu_update(x_ref[...], g_ref, wg_ref, wu_ref, wd_ref)


def _ffn_ple_kernel(x_ref, g_ref, wg_ref, wu_ref, wd_ref, p_ref, pw_ref, pn_ref, gn_ref, gw_ref,
                    o_ref):
    x = _swiglu_update(x_ref[...], g_ref, wg_ref, wu_ref, wd_ref)
    e = _rms(_dot(p_ref[...].astype(BF16), pw_ref[...]), pn_ref[...])
    gate = jax.nn.sigmoid(_dot(_rms(x, gn_ref[...]).astype(BF16), gw_ref[...]))
    o_ref[...] = x + gate * e


def _ffn(x2d, layer, norm, wg, wu, wd, ple=None):
    T, D = x2d.shape
    tm = min(ROW_TILE, T)
    row = pl.BlockSpec((tm, D), lambda i: (i, 0))
    in_specs = [row, _layer_spec((1, D), layer), _layer_spec((D, D_FF), layer),
                _layer_spec((D, D_FF), layer), _layer_spec((D_FF, D), layer)]
    args = [x2d, norm, wg, wu, wd]
    kern = _ffn_kernel
    if ple is not None:
        p, pw, pn, gn, gw = ple
        in_specs += [pl.BlockSpec((None, tm, PLE_DIM), lambda i: (layer, i, 0)),
                     _layer_spec((PLE_DIM, D), layer), _layer_spec((1, D), layer),
                     _layer_spec((1, D), layer), _layer_spec((D, D), layer)]
        args += [p, pw, pn, gn, gw]
        kern = _ffn_ple_kernel
    return pl.pallas_call(
        kern, grid=(T // tm,), in_specs=in_specs, out_specs=row,
        out_shape=jax.ShapeDtypeStruct((T, D), F32),
        compiler_params=_params("parallel"), name="ffn_ple" if ple is not None else "ffn",
    )(*args)


def _outproj_kernel(x_ref, tmaj_ref, bmaj_ref, wt_ref, wb_ref, o_ref):
    o_ref[...] = x_ref[...] + _dot(tmaj_ref[...], wt_ref[...]) + _dot(bmaj_ref[...], wb_ref[...])


def _outproj(x, part_tmaj, part_bmaj, w_tmaj, w_bmaj):
    B, S, D = x.shape
    tm = min(ROW_TILE, S)
    xs = pl.BlockSpec((None, tm, D), lambda b, s: (b, s, 0))
    return pl.pallas_call(
        _outproj_kernel, grid=(B, S // tm),
        in_specs=[xs, pl.BlockSpec((tm, 512), lambda b, s: (s, b)),
                  pl.BlockSpec((None, tm, 512), lambda b, s: (b, s, 0)),
                  _const_spec((512, D)), _const_spec((512, D))],
        out_specs=xs, out_shape=jax.ShapeDtypeStruct((B, S, D), F32),
        compiler_params=_params("parallel", "parallel"), name="outproj",
    )(x, part_tmaj, part_bmaj, w_tmaj, w_bmaj)


def _bias_lane_maps():
    eq = np.zeros((3 * LANES, LANES), np.float32)
    ek = np.zeros((3 * LANES, LANES), np.float32)
    oq = np.zeros((1, LANES), np.float32)
    ok = np.zeros((1, LANES), np.float32)
    for h in range(FOX_HEADS):
        for i in range(3):
            eq[i * LANES + h, BIAS_LANES * h + i] = 1.0
            ek[i * LANES + h, BIAS_LANES * h + 3 + i] = -1.0
            oq[0, BIAS_LANES * h + 3 + i] = 1.0
            ok[0, BIAS_LANES * h + i] = 1.0
    return jnp.asarray(eq, BF16), jnp.asarray(ek, BF16), jnp.asarray(oq), jnp.asarray(ok)


def _even_in_kernel(x_ref, g_ref, wxy_ref, wqkv_ref, wf_ref, bf_ref, qg_ref, kg_ref, ind_ref,
                    eq_ref, ek_ref, oq_ref, ok_ref,
                    xa_ref, ya_ref, q_ref, k_ref, qe_ref, ke_ref, vt_ref, carry_ref):
    tm = x_ref.shape[0]

    @pl.when(pl.program_id(1) == 0)
    def _():
        carry_ref[...] = jnp.zeros_like(carry_ref)

    n = _rms(x_ref[...], g_ref[...]).astype(BF16)
    xy = _dot(n, wxy_ref[...])
    xa_ref[...] = xy[:, :LRU_WIDTH]
    ya_ref[...] = xy[:, LRU_WIDTH:]

    c = _cumsum_rows(_log_sigmoid(_dot(n, wf_ref[...]) + bf_ref[...])) + carry_ref[...]
    carry_ref[...] = c[tm - 1:tm, :]
    pieces = jnp.concatenate(_split3_bf16(c), axis=-1).astype(BF16)
    qe_ref[...] = (_dot(pieces, eq_ref[...]) + oq_ref[...]).astype(BF16)
    ke_ref[...] = (_dot(pieces, ek_ref[...]) + ok_ref[...]).astype(BF16)

    qkv = _dot(n, wqkv_ref[...])
    ind = ind_ref[...]
    q_ref[...] = (_head_rms(qkv[:, :FOX_DIM], ind, qg_ref[...]) * QK_SCALE).astype(BF16)
    k_ref[...] = _head_rms(qkv[:, FOX_DIM:2 * FOX_DIM], ind, kg_ref[...]).astype(BF16)
    vt_ref[...] = qkv[:, 2 * FOX_DIM:].T.astype(BF16)


def _even_inproj(x, g, w_xy, w_qkv, w_f, b_f, qg, kg):
    B, S, D = x.shape
    tm = min(ROW_TILE, S)
    eq, ek, oq, ok = _bias_lane_maps()
    ind = _head_mean_matrix(FOX_DIM)
    tmaj = pl.BlockSpec((tm, LRU_WIDTH), lambda b, s: (s, b))
    bmaj = lambda w: pl.BlockSpec((None, tm, w), lambda b, s: (b, s, 0))
    consts = [g, w_xy, w_qkv, w_f, b_f, qg, kg, ind, eq, ek, oq, ok]
    return pl.pallas_call(
        _even_in_kernel, grid=(B, S // tm),
        in_specs=[bmaj(D)] + [_const_spec(a.shape) for a in consts],
        out_specs=[tmaj, tmaj, bmaj(FOX_DIM), bmaj(FOX_DIM), bmaj(LANES), bmaj(LANES),
                   pl.BlockSpec((None, None, FOX_DIM, tm), lambda b, s: (b, s, 0, 0))],
        out_shape=[jax.ShapeDtypeStruct((S, B * LRU_WIDTH), F32),
                   jax.ShapeDtypeStruct((S, B * LRU_WIDTH), F32),
                   jax.ShapeDtypeStruct((B, S, FOX_DIM), BF16),
                   jax.ShapeDtypeStruct((B, S, FOX_DIM), BF16),
                   jax.ShapeDtypeStruct((B, S, LANES), BF16),
                   jax.ShapeDtypeStruct((B, S, LANES), BF16),
                   jax.ShapeDtypeStruct((B, S // tm, FOX_DIM, tm), BF16)],
        scratch_shapes=[pltpu.VMEM((1, LANES), F32)],
        compiler_params=_params("parallel", "arbitrary"), name="even_inproj",
    )(x, *consts)


def _lru_kernel(xa_ref, ya_ref, cw_ref, cb_ref, wa_ref, ba_ref, wx_ref, bx_ref, lam_ref, o_ref,
                xpad_ref, a_ref, h_ref, carry_ref):
    rows = xa_ref.shape[0]
    halo = (LRU_CONV - 1) * SUBLANES
    half = LRU_WIDTH // 2

    @pl.when(pl.program_id(0) == 0)
    def _():
        xpad_ref[0:halo, :] = jnp.zeros((halo, LRU_WIDTH), F32)
        carry_ref[...] = jnp.zeros_like(carry_ref)

    xpad_ref[halo:halo + rows, :] = xa_ref[...]
    xc = cb_ref[...]
    for tap in range(LRU_CONV):
        xc = xc + xpad_ref[tap * SUBLANES:tap * SUBLANES + rows, :] * cw_ref[tap:tap + 1, :]
    xpad_ref[0:halo, :] = xpad_ref[rows:rows + halo, :]

    xb = xc.astype(BF16)

    def gate(w_ref, b_ref):
        z = jnp.concatenate([_dot(xb[:, :half], w_ref[0]), _dot(xb[:, half:], w_ref[1])], axis=-1)
        return jax.nn.sigmoid(z + b_ref[...])

    r = gate(wa_ref, ba_ref)
    i = gate(wx_ref, bx_ref)
    log_a = -LRU_C * r * _softplus(lam_ref[...])
    a_ref[...] = jnp.exp(log_a)
    th = jnp.tanh(log_a)
    h_ref[...] = jnp.sqrt(-2.0 * th / (1.0 - th)) * (i * xc)

    def step(t, h):
        sl = pl.ds(pl.multiple_of(t * SUBLANES, SUBLANES), SUBLANES)
        h = a_ref[sl, :] * h + h_ref[sl, :]
        h_ref[sl, :] = h
        return h

    carry_ref[...] = lax.fori_loop(0, rows // SUBLANES, step, carry_ref[...], unroll=8)
    o_ref[...] = (jax.nn.gelu(ya_ref[...]) * h_ref[...]).astype(BF16)


def _lru(xa_t, ya_t, conv_w, conv_b, wa_bd, ba, wx_bd, bx, lam):
    R, W = xa_t.shape
    rows = TIME_CHUNK * SUBLANES
    halo = (LRU_CONV - 1) * SUBLANES
    blk = pl.BlockSpec((rows, W), lambda t: (t, 0))
    return pl.pallas_call(
        _lru_kernel, grid=(R // rows,),
        in_specs=[blk, blk, _const_spec((LRU_CONV, W)), _const_spec((1, W)),
                  _const_spec(wa_bd.shape), _const_spec((1, W)),
                  _const_spec(wx_bd.shape), _const_spec((1, W)), _const_spec((1, W))],
        out_specs=blk, out_shape=jax.ShapeDtypeStruct((R, W), BF16),
        scratch_shapes=[pltpu.VMEM((halo + rows, W), F32), pltpu.VMEM((rows, W), F32),
                        pltpu.VMEM((rows, W), F32), pltpu.VMEM((SUBLANES, W), F32)],
        compiler_params=_params("arbitrary"), name="rg_lru",
    )(xa_t, ya_t, conv_w, conv_b, wa_bd, ba, wx_bd, bx, lam)


def _fox_kernel(q_ref, qe_ref, k_ref, ke_ref, vt_ref, o_ref, acc_ref):
    tq = q_ref.shape[0]
    qi = pl.program_id(2)
    pair = pl.program_id(1)
    q = q_ref[...]
    qe = qe_ref[...]
    qq = []
    for hh in range(2):
        b0 = BIAS_LANES * (2 * pair + hh)
        qq.append(jnp.concatenate([_keep_lanes(q, hh * HEAD_DIM, (hh + 1) * HEAD_DIM),
                                   _keep_lanes(qe, b0, b0 + BIAS_LANES)], axis=-1))
    key = lax.broadcasted_iota(jnp.int32, (tq, tq), 0)
    qry = lax.broadcasted_iota(jnp.int32, (tq, tq), 1)
    acc_ref[...] = jnp.zeros_like(acc_ref)

    def update(j, stats, masked):
        ks = pl.ds(pl.multiple_of(j * tq, tq), tq)
        kk = jnp.concatenate([k_ref[ks, :], ke_ref[ks, :]], axis=-1)
        new = []
        for hh in range(2):
            m_old, l_old = stats[hh]
            s = _dot_nt(kk, qq[hh])
            if masked:
                s = jnp.where(key <= qry, s, -jnp.inf)
            m_new = jnp.maximum(m_old, jnp.max(s, axis=0, keepdims=True))
            alpha = jnp.exp(m_old - m_new)
            p = jnp.exp(s - m_new)
            l_new = alpha * l_old + jnp.sum(p, axis=0, keepdims=True)
            rs = slice(hh * HEAD_DIM, (hh + 1) * HEAD_DIM)
            acc_ref[rs, :] = alpha * acc_ref[rs, :] + _dot(vt_ref[j, rs, :], p.astype(BF16))
            new.append((m_new, l_new))
        return tuple(new)

    init = ((jnp.full((1, tq), -jnp.inf, F32), jnp.zeros((1, tq), F32)),) * 2
    stats = lax.fori_loop(0, qi, lambda j, st: update(j, st, False), init)
    stats = update(qi, stats, True)
    denom = jnp.concatenate([jnp.broadcast_to(stats[hh][1], (HEAD_DIM, tq)) for hh in range(2)],
                            axis=0)
    o_ref[...] = (acc_ref[...] / denom).T.astype(BF16)


def _fox_attention(q, qe, k, ke, vt):
    B, S, _ = q.shape
    tq = vt.shape[3]
    nk = S // tq
    pairw = 2 * HEAD_DIM
    return pl.pallas_call(
        _fox_kernel, grid=(B, FOX_HEADS // 2, S // tq),
        in_specs=[pl.BlockSpec((None, tq, pairw), lambda b, p, i: (b, i, p)),
                  pl.BlockSpec((None, tq, LANES), lambda b, p, i: (b, i, 0)),
                  pl.BlockSpec((None, S, pairw), lambda b, p, i: (b, 0, p)),
                  pl.BlockSpec((None, S, LANES), lambda b, p, i: (b, 0, 0)),
                  pl.BlockSpec((None, nk, pairw, tq), lambda b, p, i: (b, 0, p, 0))],
        out_specs=pl.BlockSpec((None, tq, pairw), lambda b, p, i: (b, i, p)),
        out_shape=jax.ShapeDtypeStruct((B, S, FOX_DIM), BF16),
        scratch_shapes=[pltpu.VMEM((pairw, tq), F32)],
        compiler_params=_params("parallel", "parallel", "arbitrary"), name="fox_attention",
    )(q, qe, k, ke, vt)


def _rope_table_kernel(inv_ref, cos_ref, sin_ref):
    rows = cos_ref.shape[0]
    pos = pl.program_id(0) * rows + lax.broadcasted_iota(jnp.int32, cos_ref.shape, 0)
    lane = lax.broadcasted_iota(jnp.int32, cos_ref.shape, 1)
    ang = pos.astype(F32) * inv_ref[...]
    cos_ref[...] = jnp.cos(ang)
    sin_ref[...] = jnp.where((lane & (HEAD_DIM - 1)) < HEAD_DIM // 2, -1.0, 1.0) * jnp.sin(ang)


def _rope_tables(S):
    half = HEAD_DIM // 2
    inv = jnp.power(ROPE_THETA, -jnp.arange(half, dtype=F32) / half)
    inv4 = jnp.tile(inv, LANES // half)[None, :]
    rows = min(ROW_TILE, S)
    blk = pl.BlockSpec((rows, LANES), lambda i: (i, 0))
    return pl.pallas_call(
        _rope_table_kernel, grid=(S // rows,),
        in_specs=[pl.BlockSpec((1, LANES), lambda i: (0, 0))], out_specs=[blk, blk],
        out_shape=[jax.ShapeDtypeStruct((S, LANES), F32)] * 2,
        compiler_params=_params("parallel"), name="rope_tables",
    )(inv4)


def _rope(x, cos, sin):
    half = HEAD_DIM // 2
    lane = lax.broadcasted_iota(jnp.int32, x.shape, 1)
    first = (lane & (HEAD_DIM - 1)) < half
    swapped = jnp.where(first, pltpu.roll(x, LANES - half, axis=1), pltpu.roll(x, half, axis=1))
    return x * cos + swapped * sin


def _dup_halves(x):
    lane = lax.broadcasted_iota(jnp.int32, x.shape, 1)
    r = pltpu.roll(x, HEAD_DIM, axis=1)
    lo = lane < HEAD_DIM
    return jnp.concatenate([jnp.where(lo, x, r), jnp.where(lo, r, x)], axis=-1)


def _odd_in_kernel(x_ref, g_ref, wq_ref, wkv_ref, wu_ref, qg_ref, kg_ref, ind_ref, cos_ref, sin_ref,
                   q_ref, kd_ref, vd_ref, u_ref):
    n = _rms(x_ref[...], g_ref[...]).astype(BF16)
    u_ref[...] = _dot(n, wu_ref[...])
    cos, sin = cos_ref[...], sin_ref[...]
    q = _head_rms(_dot(n, wq_ref[...]), ind_ref[...], qg_ref[...])
    for blk in range(SWA_DIM // LANES):
        sl = slice(blk * LANES, (blk + 1) * LANES)
        q_ref[:, sl] = (_rope(q[:, sl], cos, sin) * QK_SCALE).astype(BF16)
    kv = _dot(n, wkv_ref[...])
    k = _head_rms(kv[:, :LANES], ind_ref[0:LANES, 0:LANES], kg_ref[...])
    kd_ref[...] = _dup_halves(_rope(k, cos, sin)).astype(BF16)
    vd_ref[...] = _dup_halves(kv[:, LANES:]).astype(BF16)


def _odd_inproj(x, g, w_q, w_kv, w_u, qg, kg, cos, sin):
    B, S, D = x.shape
    tm = min(ROW_TILE, S)
    ind = _head_mean_matrix(SWA_DIM)
    bmaj = lambda w: pl.BlockSpec((None, tm, w), lambda b, s: (b, s, 0))
    tab = pl.BlockSpec((tm, LANES), lambda b, s: (s, 0))
    consts = [g, w_q, w_kv, w_u, qg, kg, ind]
    return pl.pallas_call(
        _odd_in_kernel, grid=(B, S // tm),
        in_specs=[bmaj(D)] + [_const_spec(a.shape) for a in consts] + [tab, tab],
        out_specs=[bmaj(SWA_DIM), bmaj(2 * LANES), bmaj(2 * LANES),
                   pl.BlockSpec((tm, S5_WIDTH), lambda b, s: (s, b))],
        out_shape=[jax.ShapeDtypeStruct((B, S, SWA_DIM), BF16),
                   jax.ShapeDtypeStruct((B, S, 2 * LANES), BF16),
                   jax.ShapeDtypeStruct((B, S, 2 * LANES), BF16),
                   jax.ShapeDtypeStruct((S, B * S5_WIDTH), F32)],
        compiler_params=_params("parallel", "parallel"), name="odd_inproj",
    )(x, *consts, cos, sin)


def _swa_kernel(q_ref, kd_ref, vd_ref, sink_ref, o_ref):
    tq = q_ref.shape[0]
    W = SWA_WINDOW
    G = SWA_GROUP
    base = pl.program_id(1) * tq
    row = lax.broadcasted_iota(jnp.int32, (G * W, 2 * W), 0)
    col = lax.broadcasted_iota(jnp.int32, (G * W, 2 * W), 1)
    qoff = row & (W - 1)
    rgrp = lax.broadcasted_iota(jnp.int32, (G * W, 1), 0) // W
    lane = lax.broadcasted_iota(jnp.int32, (W, LANES), 1)
    for n in range(tq // W):
        rows = slice(n * W, (n + 1) * W)
        r0 = base + n * W
        kstart = pl.multiple_of(jnp.maximum(r0 - W, 0), W)
        diff = (r0 + qoff) - (kstart + col)
        valid = (diff >= 0) & (diff < W)
        for kvh in range(SWA_KV_HEADS):
            kv_lanes = slice(kvh * LANES, (kvh + 1) * LANES)
            parts = []
            for g in range(G):
                head = kvh * G + g
                blk = q_ref[rows, (head // 2) * LANES:(head // 2 + 1) * LANES]
                half = head % 2
                parts.append(_keep_lanes(blk, half * HEAD_DIM, (half + 1) * HEAD_DIM))
            s = _dot_nt(jnp.concatenate(parts, axis=0), kd_ref[pl.ds(kstart, 2 * W), kv_lanes])
            s = jnp.where(valid, s, -jnp.inf)
            sink = jnp.zeros((G * W, 1), F32)
            for g in range(G):
                head = kvh * G + g
                sink = jnp.where(rgrp == g, sink_ref[:, head:head + 1], sink)
            m = jnp.maximum(jnp.max(s, axis=-1, keepdims=True), sink)
            e = jnp.exp(s - m)
            pr = e / (jnp.sum(e, axis=-1, keepdims=True) + jnp.exp(sink - m))
            o = _dot(pr.astype(BF16), vd_ref[pl.ds(kstart, 2 * W), kv_lanes])
            for pair in range(G // 2):
                left = o[(2 * pair) * W:(2 * pair + 1) * W, :]
                right = o[(2 * pair + 1) * W:(2 * pair + 2) * W, :]
                lb = kvh * (G // 2) + pair
                o_ref[rows, lb * LANES:(lb + 1) * LANES] = jnp.where(
                    lane < HEAD_DIM, left, right).astype(BF16)


def _swa_attention(q, kd, vd, sinks):
    B, S, _ = q.shape
    tq = min(ATTN_TILE, S)
    return pl.pallas_call(
        _swa_kernel, grid=(B, S // tq),
        in_specs=[pl.BlockSpec((None, tq, SWA_DIM), lambda b, i: (b, i, 0)),
                  pl.BlockSpec((None, S, 2 * LANES), lambda b, i: (b, 0, 0)),
                  pl.BlockSpec((None, S, 2 * LANES), lambda b, i: (b, 0, 0)),
                  pl.BlockSpec((1, SWA_HEADS), lambda b, i: (0, 0))],
        out_specs=pl.BlockSpec((None, tq, SWA_DIM), lambda b, i: (b, i, 0)),
        out_shape=jax.ShapeDtypeStruct((B, S, SWA_DIM), BF16),
        compiler_params=_params("parallel", "arbitrary"), name="swa_attention",
    )(q, kd, vd, sinks)


def _s5_prep_kernel(lr_ref, li_ref, ldt_ref, br_ref, bi_ref, ar_ref, ai_ref, bbr_ref, bbi_ref):
    lr, li = lr_ref[...], li_ref[...]
    dt = jnp.exp(ldt_ref[...])
    mag = jnp.exp(lr * dt)
    ar = mag * jnp.cos(li * dt)
    ai = mag * jnp.sin(li * dt)
    den = lr * lr + li * li
    cr = ((ar - 1.0) * lr + ai * li) / den
    ci = (ai * lr - (ar - 1.0) * li) / den
    br, bi = br_ref[...], bi_ref[...]
    ar_ref[...] = ar
    ai_ref[...] = ai
    bbr_ref[...] = cr * br - ci * bi
    bbi_ref[...] = cr * bi + ci * br


def _s5_prep(lam_re, lam_im, log_dt, b_re, b_im):
    G, P, C = b_re.shape
    rep = lambda a: jnp.repeat(a, C, axis=0)
    bt = lambda a: a.transpose(0, 2, 1).reshape(G * C, P)
    ldt = jnp.broadcast_to(log_dt[:, None], (G, P))
    full = pl.BlockSpec((G * C, P), lambda: (0, 0))
    outs = pl.pallas_call(
        _s5_prep_kernel, in_specs=[full] * 5, out_specs=[full] * 4,
        out_shape=[jax.ShapeDtypeStruct((G * C, P), F32)] * 4, name="s5_prep",
    )(rep(lam_re), rep(lam_im), rep(ldt), bt(b_re), bt(b_im))
    ar, ai, bbr, bbi = [o.reshape(G, C, P) for o in outs]
    return ar[:, 0], ai[:, 0], bbr, bbi


def _s5_kernel(u_ref, bm_ref, cm_ref, ar_ref, ai_ref, d_ref, gw_ref, gb_ref, o_ref,
               h_ref, carry_ref):
    rows = u_ref.shape[0]
    half = h_ref.shape[2] // 2

    @pl.when(pl.program_id(0) == 0)
    def _():
        carry_ref[...] = jnp.zeros_like(carry_ref)

    u = u_ref[...]
    ub = u.astype(BF16)
    ys = []
    for g in range(S5_LANE_GROUPS):
        h_ref[g] = _dot(ub[:, g * LANES:(g + 1) * LANES], bm_ref[g])
        ar = jnp.broadcast_to(ar_ref[g], (SUBLANES, half))
        ai = jnp.broadcast_to(ai_ref[g], (SUBLANES, half))

        def step(t, carry, g=g, ar=ar, ai=ai):
            hr, hi = carry
            sl = pl.ds(pl.multiple_of(t * SUBLANES, SUBLANES), SUBLANES)
            nr = ar * hr - ai * hi + h_ref[g, sl, 0:half]
            ni = ar * hi + ai * hr + h_ref[g, sl, half:2 * half]
            h_ref[g, sl, 0:half] = nr
            h_ref[g, sl, half:2 * half] = ni
            return nr, ni

        hr, hi = lax.fori_loop(0, rows // SUBLANES, step,
                               (carry_ref[g, :, 0:half], carry_ref[g, :, half:2 * half]), unroll=8)
        carry_ref[g, :, 0:half] = hr
        carry_ref[g, :, half:2 * half] = hi
        ys.append(_dot(h_ref[g].astype(BF16), cm_ref[g]))
    y = jnp.concatenate(ys, axis=-1) + d_ref[...] * u
    z = jax.nn.gelu(y)
    o_ref[...] = (z * jax.nn.sigmoid(_dot(z.astype(BF16), gw_ref[...]) + gb_ref[...])).astype(BF16)


def _s5(u_t, bmat, cmat, ar, ai, d, glu_w, glu_b):
    R, W = u_t.shape
    rows = TIME_CHUNK * SUBLANES
    nstate = bmat.shape[2]
    blk = pl.BlockSpec((rows, W), lambda t: (t, 0))
    return pl.pallas_call(
        _s5_kernel, grid=(R // rows,),
        in_specs=[blk, _const_spec(bmat.shape), _const_spec(cmat.shape), _const_spec(ar.shape),
                  _const_spec(ai.shape), _const_spec((1, W)), _const_spec((W, W)),
                  _const_spec((1, W))],
        out_specs=blk, out_shape=jax.ShapeDtypeStruct((R, W), BF16),
        scratch_shapes=[pltpu.VMEM((S5_LANE_GROUPS, rows, nstate), F32),
                        pltpu.VMEM((S5_LANE_GROUPS, SUBLANES, nstate), F32)],
        compiler_params=_params("arbitrary"), name="s5_glu",
    )(u_t, bmat, cmat, ar, ai, d, glu_w, glu_b)


def _s5_matrices(ar, ai, bbr, bbi, c_re, c_im):
    L, GL = S5_LANE_GROUPS, S5_GROUPS // S5_LANE_GROUPS
    C, P = S5_GROUP, S5_STATE
    eye = jnp.eye(GL, dtype=F32)

    def inmap(b):
        return jnp.einsum("lgcp,gh->lgchp", b.reshape(L, GL, C, P), eye).reshape(L, GL * C, GL * P)

    def outmap(c):
        return jnp.einsum("lgcp,gh->lgphc", c.reshape(L, GL, C, P), eye).reshape(L, GL * P, GL * C)

    bmat = jnp.concatenate([inmap(bbr), inmap(bbi)], axis=2).astype(BF16)
    cmat = jnp.concatenate([outmap(c_re), outmap(-c_im)], axis=1).astype(BF16)
    a_r = ar.reshape(L, 1, GL * P)
    a_i = ai.reshape(L, 1, GL * P)
    return bmat, cmat, a_r, a_i


def _block_diag_pairs(w):
    nb, bs, _ = w.shape
    half = nb // 2
    eye = jnp.eye(half, dtype=w.dtype)
    out = jnp.einsum("thij,hk->thikj", w.reshape(2, half, bs, bs), eye)
    return out.reshape(2, half * bs, half * bs).astype(BF16)


def kernel(x, p, ffn1_norm, ffn1_wg, ffn1_wu, ffn1_wd, mix_norm, ffn2_norm, ffn2_wg, ffn2_wu, ffn2_wd, ple_w, ple_norm, ple_gate_norm, ple_gate_w, ev_w_in, lru_conv_w, lru_conv_b, lru_wa, lru_ba, lru_wx, lru_bx, lru_lambda, fox_bf, fox_q_norm, fox_k_norm, ev_w_out, od_w_in, swa_q_norm, swa_k_norm, swa_sinks, s5_lambda_re, s5_lambda_im, s5_log_dt, s5_b_re, s5_b_im, s5_c_re, s5_c_im, s5_d, s5_glu_w, s5_glu_b, od_w_out):
    B, S, D = x.shape
    depth = p.shape[0]
    assert B == SUBLANES and D == D_MODEL and S % TIME_CHUNK == 0
    T = B * S
    bf = lambda a: a.astype(BF16)
    row = lambda a: a[:, None, :]
    per_head = lambda gain, heads: jnp.tile(gain, heads)[None, :]

    f1 = (row(ffn1_norm), bf(ffn1_wg), bf(ffn1_wu), bf(ffn1_wd))
    f2 = (row(ffn2_norm), bf(ffn2_wg), bf(ffn2_wu), bf(ffn2_wd))
    ple = (p.reshape(depth, T, PLE_DIM), bf(ple_w), row(ple_norm), row(ple_gate_norm),
           bf(ple_gate_w))
    cos, sin = _rope_tables(S)

    for i in range(depth):
        j = i // 2
        x = _ffn(x.reshape(T, D), i, *f1).reshape(B, S, D)
        g = mix_norm[i][None, :]
        if i % 2 == 0:
            w_in = bf(ev_w_in[j])
            o1, o2 = 2 * LRU_WIDTH, 2 * LRU_WIDTH + 3 * FOX_DIM
            w_f = jnp.pad(w_in[:, o2:], ((0, 0), (0, LANES - FOX_HEADS)))
            b_f = jnp.pad(fox_bf[j], (0, LANES - FOX_HEADS))[None, :]
            xa, ya, q, k, qe, ke, vt = _even_inproj(
                x, g, w_in[:, :o1], w_in[:, o1:o2], w_f, b_f,
                per_head(fox_q_norm[j], FOX_HEADS), per_head(fox_k_norm[j], FOX_HEADS))
            a_out = _lru(xa.reshape(S * B, LRU_WIDTH), ya.reshape(S * B, LRU_WIDTH),
                         lru_conv_w[j], lru_conv_b[j][None, :],
                         _block_diag_pairs(lru_wa[j]), lru_ba[j][None, :],
                         _block_diag_pairs(lru_wx[j]), lru_bx[j][None, :],
                         lru_lambda[j][None, :])
            b_out = _fox_attention(q, qe, k, ke, vt)
            w_out = bf(ev_w_out[j])
            x = _outproj(x, a_out.reshape(S, B * LRU_WIDTH), b_out,
                         w_out[:LRU_WIDTH], w_out[LRU_WIDTH:])
        else:
            w_in = bf(od_w_in[j])
            kvd = SWA_KV_HEADS * HEAD_DIM
            o1, o2 = SWA_DIM, SWA_DIM + 2 * kvd
            q, kd, vd, u = _odd_inproj(
                x, g, w_in[:, :o1], w_in[:, o1:o2], w_in[:, o2:],
                per_head(swa_q_norm[j], SWA_HEADS), per_head(swa_k_norm[j], SWA_KV_HEADS), cos, sin)
            c_out = _swa_attention(q, kd, vd, swa_sinks[j][None, :])
            ar, ai, bbr, bbi = _s5_prep(s5_lambda_re[j], s5_lambda_im[j], s5_log_dt[j],
                                        s5_b_re[j], s5_b_im[j])
            bmat, cmat, a_r, a_i = _s5_matrices(ar, ai, bbr, bbi, s5_c_re[j], s5_c_im[j])
            d_out = _s5(u.reshape(S * B, S5_WIDTH), bmat, cmat, a_r, a_i, s5_d[j][None, :],
                        bf(s5_glu_w[j]), s5_glu_b[j][None, :])
            w_out = bf(od_w_out[j])
            x = _outproj(x, d_out.reshape(S, B * S5_WIDTH), c_out,
                         w_out[SWA_DIM:], w_out[:SWA_DIM])
        x = _ffn(x.reshape(T, D), i, *f2, ple=ple).reshape(B, S, D)
    return x
```

```python
import numpy as np

import jax
import jax.numpy as jnp
from jax import lax
from jax.experimental import pallas as pl
from jax.experimental.pallas import tpu as pltpu

F32 = jnp.float32
BF16 = jnp.bfloat16

D_MODEL = 1024
HEAD_DIM = 64
LRU_WIDTH = 512
LRU_CONV = 4
LRU_C = 8.0
FOX_HEADS = 8
FOX_DIM = 512
SWA_HEADS = 8
SWA_KV_HEADS = 2
SWA_GROUP = SWA_HEADS // SWA_KV_HEADS
SWA_DIM = 512
SWA_WINDOW = 128
S5_WIDTH = 512
S5_GROUP = 16
S5_GROUPS = 32
S5_STATE = 64
D_FF = 2816
PLE_DIM = 256
ROPE_THETA = 10000.0
EPS = 1e-6
MACARON = 0.5
QK_SCALE = HEAD_DIM ** -0.5
LOG2E = 1.4426950408889634

SUBLANES = 8
LANES = 128
TIME_CHUNK = 64
ROW_TILE = SUBLANES * TIME_CHUNK
ATTN_TILE = 512
S5_LANE_GROUPS = 4
BIAS_LANES = 6
VT_ROWS = 80
VMEM_LIMIT = 56 * 1024 * 1024


def _dot(a, b):
    return jnp.dot(a, b, preferred_element_type=F32)


def _dot_nt(a, b):
    return lax.dot_general(a, b, (((1,), (1,)), ((), ())), preferred_element_type=F32)


def _rms(x, g):
    ms = jnp.mean(x * x, axis=-1, keepdims=True)
    return x * lax.rsqrt(ms + EPS) * g


def _head_rms(x, ind, gain):
    sq = x * x
    hi = sq.astype(BF16)
    lo = (sq - hi.astype(F32)).astype(BF16)
    ms = _dot(hi, ind) + _dot(lo, ind)
    return x * lax.rsqrt(ms + EPS) * gain


def _softplus(x):
    return jnp.maximum(x, 0.0) + jnp.log1p(jnp.exp(-jnp.abs(x)))


def _log_sigmoid(x):
    return -_softplus(-x)


def _cumsum_time(x):
    n = x.shape[0]
    row = lax.broadcasted_iota(jnp.int32, x.shape, 0)
    d = SUBLANES
    while d < n:
        x = x + jnp.where(row >= d, pltpu.roll(x, d, axis=0), 0.0)
        d *= 2
    return x


def _time_major_perm(steps):
    r = np.arange(SUBLANES * steps)
    src = (r % SUBLANES) * steps + r // SUBLANES
    return jnp.asarray(src[:, None] == r[None, :], dtype=BF16)


def _permute_rows(perm, x):
    return _dot(perm, x).astype(BF16)


def _split3_bf16(c):
    hi = c.astype(BF16).astype(F32)
    r = c - hi
    mid = r.astype(BF16).astype(F32)
    return hi, mid, r - mid


def _keep_lanes(x, lo, hi):
    lane = lax.broadcasted_iota(jnp.int32, x.shape, 1)
    return jnp.where((lane >= lo) & (lane < hi), x.astype(F32), 0.0).astype(BF16)


def _params(*sem):
    return pltpu.CompilerParams(dimension_semantics=sem, vmem_limit_bytes=VMEM_LIMIT)


def _const_spec(shape):
    nd = len(shape)
    return pl.BlockSpec(shape, lambda *_: (0,) * nd, pipeline_mode=pl.Buffered(1))


def _layer_spec(shape, layer):
    nd = len(shape)
    return pl.BlockSpec((None,) + tuple(shape), lambda *_: (layer,) + (0,) * nd,
                        pipeline_mode=pl.Buffered(1))


def _head_mean_matrix(width):
    h = np.arange(width) // HEAD_DIM
    return jnp.asarray((h[:, None] == h[None, :]) / HEAD_DIM, dtype=BF16)


def _swiglu_update(x, g_ref, wg_ref, wu_ref, wd_ref):
    n = _rms(x, g_ref[...]).astype(BF16)
    hg = _dot(n, wg_ref[...])
    hu = _dot(n, wu_ref[...])
    act = (hg * jax.nn.sigmoid(hg) * hu).astype(BF16)
    return x + MACARON * _dot(act, wd_ref[...])


def _ffn_kernel(x_ref, g_ref, wg_ref, wu_ref, wd_ref, o_ref):
    o_ref[...] = _swiglu_update(x_ref[...], g_ref, wg_ref, wu_ref, wd_ref)


def _ffn_ple_kernel(x_ref, g_ref, wg_ref, wu_ref, wd_ref, p_ref, pw_ref, pn_ref, gn_ref, gw_ref,
                    o_ref):
    x = _swiglu_update(x_ref[...], g_ref, wg_ref, wu_ref, wd_ref)
    e = _rms(_dot(p_ref[...].astype(BF16), pw_ref[...]), pn_ref[...])
    gate = jax.nn.sigmoid(_dot(_rms(x, gn_ref[...]).astype(BF16), gw_ref[...]))
    o_ref[...] = x + gate * e


def _ffn(x2d, layer, norm, wg, wu, wd, ple=None):
    T, D = x2d.shape
    tm = min(ROW_TILE, T)
    row = pl.BlockSpec((tm, D), lambda i: (i, 0))
    in_specs = [row, _layer_spec((1, D), layer), _layer_spec((D, D_FF), layer),
                _layer_spec((D, D_FF), layer), _layer_spec((D_FF, D), layer)]
    args = [x2d, norm, wg, wu, wd]
    kern = _ffn_kernel
    if ple is not None:
        p, pw, pn, gn, gw = ple
        in_specs += [pl.BlockSpec((None, tm, PLE_DIM), lambda i: (layer, i, 0)),
                     _layer_spec((PLE_DIM, D), layer), _layer_spec((1, D), layer),
                     _layer_spec((1, D), layer), _layer_spec((D, D), layer)]
        args += [p, pw, pn, gn, gw]
        kern = _ffn_ple_kernel
    return pl.pallas_call(
        kern, grid=(T // tm,), in_specs=in_specs, out_specs=row,
        out_shape=jax.ShapeDtypeStruct((T, D), F32),
        compiler_params=_params("parallel"), name="ffn_ple" if ple is not None else "ffn",
    )(*args)


def _outproj_kernel(x_ref, tmaj_ref, bmaj_ref, wt_ref, wb_ref, unperm_ref, o_ref):
    B, ts, D = x_ref.shape
    part_t = _permute_rows(unperm_ref[...], tmaj_ref[...])
    part_b = bmaj_ref[...].reshape(B * ts, bmaj_ref.shape[2])
    upd = _dot(part_t, wt_ref[...]) + _dot(part_b, wb_ref[...])
    o_ref[...] = x_ref[...] + upd.reshape(B, ts, D)


def _outproj(x, part_tmaj, part_bmaj, w_tmaj, w_bmaj):
    B, S, D = x.shape
    ts = TIME_CHUNK
    W = part_tmaj.shape[1]
    unperm = _time_major_perm(ts).T
    xs = pl.BlockSpec((B, ts, D), lambda s: (0, s, 0))
    return pl.pallas_call(
        _outproj_kernel, grid=(S // ts,),
        in_specs=[xs, pl.BlockSpec((B * ts, W), lambda s: (s, 0)),
                  pl.BlockSpec((B, ts, W), lambda s: (0, s, 0)),
                  _const_spec((W, D)), _const_spec((W, D)), _const_spec(unperm.shape)],
        out_specs=xs, out_shape=jax.ShapeDtypeStruct((B, S, D), F32),
        compiler_params=_params("parallel"), name="outproj",
    )(x, part_tmaj, part_bmaj, w_tmaj, w_bmaj, unperm)


def _bias_lane_maps():
    eq = np.zeros((3 * LANES, LANES), np.float32)
    ek = np.zeros((3 * LANES, LANES), np.float32)
    oq = np.zeros((1, LANES), np.float32)
    ok = np.zeros((1, LANES), np.float32)
    for h in range(FOX_HEADS):
        for i in range(3):
            eq[i * LANES + h, BIAS_LANES * h + i] = 1.0
            ek[i * LANES + h, BIAS_LANES * h + 3 + i] = -1.0
            oq[0, BIAS_LANES * h + 3 + i] = 1.0
            ok[0, BIAS_LANES * h + i] = 1.0
    return jnp.asarray(eq, BF16), jnp.asarray(ek, BF16), jnp.asarray(oq), jnp.asarray(ok)


def _even_in_kernel(x_ref, g_ref, wxy_ref, wqkv_ref, wf_ref, bf_ref, qg_ref, kg_ref, ind_ref,
                    eq_ref, ek_ref, oq_ref, ok_ref, perm_ref, unperm_ref,
                    xa_ref, ya_ref, q_ref, k_ref, qe_ref, ke_ref, v_ref, carry_ref):
    B, ts, D = x_ref.shape
    rows = B * ts

    @pl.when(pl.program_id(0) == 0)
    def _():
        carry_ref[...] = jnp.zeros_like(carry_ref)

    n = _rms(x_ref[...].reshape(rows, D), g_ref[...]).astype(BF16)
    n_t = _permute_rows(perm_ref[...], n)
    xy = _dot(n_t, wxy_ref[...])
    xa_ref[...] = xy[:, :LRU_WIDTH]
    ya_ref[...] = xy[:, LRU_WIDTH:]

    logf = _log_sigmoid(_dot(n_t, wf_ref[...]) + bf_ref[...]) * LOG2E
    c = _cumsum_time(logf) + pltpu.repeat(carry_ref[...], ts, axis=0)
    carry_ref[...] = c[rows - SUBLANES:rows, :]
    pieces = jnp.concatenate(_split3_bf16(c), axis=-1).astype(BF16)
    unperm = unperm_ref[...]
    qe = (_dot(pieces, eq_ref[...]) + oq_ref[...]).astype(BF16)
    ke = (_dot(pieces, ek_ref[...]) + ok_ref[...]).astype(BF16)
    qe_ref[...] = _permute_rows(unperm, qe).reshape(B, ts, LANES)
    ke_ref[...] = _permute_rows(unperm, ke).reshape(B, ts, LANES)

    qkv = _dot(n, wqkv_ref[...])
    ind = ind_ref[...]
    q = _head_rms(qkv[:, :FOX_DIM], ind, qg_ref[...]) * (QK_SCALE * LOG2E)
    k = _head_rms(qkv[:, FOX_DIM:2 * FOX_DIM], ind, kg_ref[...])
    q_ref[...] = q.astype(BF16).reshape(B, ts, FOX_DIM)
    k_ref[...] = k.astype(BF16).reshape(B, ts, FOX_DIM)
    v_ref[...] = qkv[:, 2 * FOX_DIM:].astype(BF16).reshape(B, ts, FOX_DIM)


def _even_inproj(x, g, w_xy, w_qkv, w_f, b_f, qg, kg):
    B, S, D = x.shape
    ts = TIME_CHUNK
    eq, ek, oq, ok = _bias_lane_maps()
    perm = _time_major_perm(ts)
    tmaj = pl.BlockSpec((B * ts, LRU_WIDTH), lambda s: (s, 0))
    bmaj = lambda w: pl.BlockSpec((B, ts, w), lambda s: (0, s, 0))
    consts = [g, w_xy, w_qkv, w_f, b_f, qg, kg, _head_mean_matrix(FOX_DIM), eq, ek, oq, ok,
              perm, perm.T]
    return pl.pallas_call(
        _even_in_kernel, grid=(S // ts,),
        in_specs=[bmaj(D)] + [_const_spec(a.shape) for a in consts],
        out_specs=[tmaj, tmaj, bmaj(FOX_DIM), bmaj(FOX_DIM), bmaj(LANES), bmaj(LANES),
                   bmaj(FOX_DIM)],
        out_shape=[jax.ShapeDtypeStruct((S * B, LRU_WIDTH), F32),
                   jax.ShapeDtypeStruct((S * B, LRU_WIDTH), F32),
                   jax.ShapeDtypeStruct((B, S, FOX_DIM), BF16),
                   jax.ShapeDtypeStruct((B, S, FOX_DIM), BF16),
                   jax.ShapeDtypeStruct((B, S, LANES), BF16),
                   jax.ShapeDtypeStruct((B, S, LANES), BF16),
                   jax.ShapeDtypeStruct((B, S, FOX_DIM), BF16)],
        scratch_shapes=[pltpu.VMEM((SUBLANES, LANES), F32)],
        compiler_params=_params("arbitrary"), name="even_inproj",
    )(x, *consts)


def _vt_kernel(v_ref, o_ref):
    vt = v_ref[...].astype(F32).T
    ones = jnp.ones((VT_ROWS - HEAD_DIM, vt.shape[1]), BF16)
    for h in range(FOX_HEADS):
        o_ref[h, 0:HEAD_DIM, :] = vt[h * HEAD_DIM:(h + 1) * HEAD_DIM, :].astype(BF16)
        o_ref[h, HEAD_DIM:VT_ROWS, :] = ones


def _value_transpose(v):
    B, S, W = v.shape
    tk = min(ATTN_TILE, S)
    return pl.pallas_call(
        _vt_kernel, grid=(B, S // tk),
        in_specs=[pl.BlockSpec((None, tk, W), lambda b, j: (b, j, 0))],
        out_specs=pl.BlockSpec((None, FOX_HEADS, None, VT_ROWS, tk), lambda b, j: (b, 0, j, 0, 0)),
        out_shape=jax.ShapeDtypeStruct((B, FOX_HEADS, S // tk, VT_ROWS, tk), BF16),
        compiler_params=_params("parallel", "parallel"), name="value_transpose",
    )(v)


def _lru_kernel(xa_ref, ya_ref, cw_ref, cb_ref, wa_ref, ba_ref, wx_ref, bx_ref, lam_ref, o_ref,
                xpad_ref, a_ref, h_ref, carry_ref):
    rows = xa_ref.shape[0]
    halo = (LRU_CONV - 1) * SUBLANES
    half = LRU_WIDTH // 2

    @pl.when(pl.program_id(0) == 0)
    def _():
        xpad_ref[0:halo, :] = jnp.zeros((halo, LRU_WIDTH), F32)
        carry_ref[...] = jnp.zeros_like(carry_ref)

    xpad_ref[halo:halo + rows, :] = xa_ref[...]
    xc = cb_ref[...]
    for tap in range(LRU_CONV):
        xc = xc + xpad_ref[tap * SUBLANES:tap * SUBLANES + rows, :] * cw_ref[tap:tap + 1, :]
    xpad_ref[0:halo, :] = xpad_ref[rows:rows + halo, :]

    xb = xc.astype(BF16)

    def gate(w_ref, b_ref):
        z = jnp.concatenate([_dot(xb[:, :half], w_ref[0]), _dot(xb[:, half:], w_ref[1])], axis=-1)
        return jax.nn.sigmoid(z + b_ref[...])

    r = gate(wa_ref, ba_ref)
    i = gate(wx_ref, bx_ref)
    log_a = -LRU_C * r * _softplus(lam_ref[...])
    a_ref[...] = jnp.exp(log_a)
    th = jnp.tanh(log_a)
    h_ref[...] = jnp.sqrt(-2.0 * th / (1.0 - th)) * (i * xc)

    def step(t, h):
        sl = pl.ds(pl.multiple_of(t * SUBLANES, SUBLANES), SUBLANES)
        h = a_ref[sl, :] * h + h_ref[sl, :]
        h_ref[sl, :] = h
        return h

    carry_ref[...] = lax.fori_loop(0, rows // SUBLANES, step, carry_ref[...], unroll=8)
    o_ref[...] = (jax.nn.gelu(ya_ref[...]) * h_ref[...]).astype(BF16)


def _lru(xa_t, ya_t, conv_w, conv_b, wa_bd, ba, wx_bd, bx, lam):
    R, W = xa_t.shape
    rows = TIME_CHUNK * SUBLANES
    halo = (LRU_CONV - 1) * SUBLANES
    blk = pl.BlockSpec((rows, W), lambda t: (t, 0))
    return pl.pallas_call(
        _lru_kernel, grid=(R // rows,),
        in_specs=[blk, blk, _const_spec((LRU_CONV, W)), _const_spec((1, W)),
                  _const_spec(wa_bd.shape), _const_spec((1, W)),
                  _const_spec(wx_bd.shape), _const_spec((1, W)), _const_spec((1, W))],
        out_specs=blk, out_shape=jax.ShapeDtypeStruct((R, W), BF16),
        scratch_shapes=[pltpu.VMEM((halo + rows, W), F32), pltpu.VMEM((rows, W), F32),
                        pltpu.VMEM((rows, W), F32), pltpu.VMEM((SUBLANES, W), F32)],
        compiler_params=_params("arbitrary"), name="rg_lru",
    )(xa_t, ya_t, conv_w, conv_b, wa_bd, ba, wx_bd, bx, lam)


def _fox_kernel(q_ref, qe_ref, k_ref, ke_ref, vt_ref, o_ref, acc_ref):
    tq = q_ref.shape[0]
    qi = pl.program_id(2)
    pair = pl.program_id(1)
    q = q_ref[...]
    qe = qe_ref[...]
    qq = []
    for hh in range(2):
        b0 = BIAS_LANES * (2 * pair + hh)
        qq.append(jnp.concatenate([_keep_lanes(q, hh * HEAD_DIM, (hh + 1) * HEAD_DIM),
                                   _keep_lanes(qe, b0, b0 + BIAS_LANES)], axis=-1))
    key = lax.broadcasted_iota(jnp.int32, (tq, tq), 0)
    qry = lax.broadcasted_iota(jnp.int32, (tq, tq), 1)
    acc_ref[...] = jnp.zeros_like(acc_ref)

    def update(j, m_run, masked):
        ks = pl.ds(pl.multiple_of(j * tq, tq), tq)
        kk = jnp.concatenate([k_ref[ks, :], ke_ref[ks, :]], axis=-1)
        new = []
        for hh in range(2):
            s = _dot_nt(kk, qq[hh])
            if masked:
                s = jnp.where(key <= qry, s, -jnp.inf)
            m_new = jnp.maximum(m_run[hh], jnp.max(s, axis=0, keepdims=True))
            alpha = jnp.exp2(m_run[hh] - m_new)
            p = jnp.exp2(s - m_new).astype(BF16)
            acc_ref[hh] = alpha * acc_ref[hh] + _dot(vt_ref[hh, j], p)
            new.append(m_new)
        return tuple(new)

    init = (jnp.full((1, tq), -jnp.inf, F32),) * 2
    m_run = lax.fori_loop(0, qi, lambda j, m: update(j, m, False), init)
    update(qi, m_run, True)
    out = [acc_ref[hh, 0:HEAD_DIM, :] / acc_ref[hh, HEAD_DIM:HEAD_DIM + 1, :] for hh in range(2)]
    o_ref[...] = jnp.concatenate(out, axis=0).T.astype(BF16)


def _fox_attention(q, qe, k, ke, vt):
    B, S, _ = q.shape
    nk, tq = vt.shape[2], vt.shape[4]
    pairw = 2 * HEAD_DIM
    return pl.pallas_call(
        _fox_kernel, grid=(B, FOX_HEADS // 2, S // tq),
        in_specs=[pl.BlockSpec((None, tq, pairw), lambda b, p, i: (b, i, p)),
                  pl.BlockSpec((None, tq, LANES), lambda b, p, i: (b, i, 0)),
                  pl.BlockSpec((None, S, pairw), lambda b, p, i: (b, 0, p)),
                  pl.BlockSpec((None, S, LANES), lambda b, p, i: (b, 0, 0)),
                  pl.BlockSpec((None, 2, nk, VT_ROWS, tq), lambda b, p, i: (b, p, 0, 0, 0))],
        out_specs=pl.BlockSpec((None, tq, pairw), lambda b, p, i: (b, i, p)),
        out_shape=jax.ShapeDtypeStruct((B, S, FOX_DIM), BF16),
        scratch_shapes=[pltpu.VMEM((2, VT_ROWS, tq), F32)],
        compiler_params=_params("parallel", "parallel", "arbitrary"), name="fox_attention",
    )(q, qe, k, ke, vt)


def _rope_table_kernel(inv_ref, cos_ref, sin_ref):
    rows = cos_ref.shape[0]
    pos = pl.program_id(0) * rows + lax.broadcasted_iota(jnp.int32, cos_ref.shape, 0)
    lane = lax.broadcasted_iota(jnp.int32, cos_ref.shape, 1)
    ang = pos.astype(F32) * inv_ref[...]
    cos_ref[...] = jnp.cos(ang)
    sin_ref[...] = jnp.where((lane & (HEAD_DIM - 1)) < HEAD_DIM // 2, -1.0, 1.0) * jnp.sin(ang)


def _rope_tables(S):
    half = HEAD_DIM // 2
    inv = jnp.power(ROPE_THETA, -jnp.arange(half, dtype=F32) / half)
    inv4 = jnp.tile(inv, LANES // half)[None, :]
    rows = min(ROW_TILE, S)
    blk = pl.BlockSpec((rows, LANES), lambda i: (i, 0))
    return pl.pallas_call(
        _rope_table_kernel, grid=(S // rows,),
        in_specs=[pl.BlockSpec((1, LANES), lambda i: (0, 0))], out_specs=[blk, blk],
        out_shape=[jax.ShapeDtypeStruct((S, LANES), F32)] * 2,
        compiler_params=_params("parallel"), name="rope_tables",
    )(inv4)


def _rope(x, cos, sin):
    half = HEAD_DIM // 2
    lane = lax.broadcasted_iota(jnp.int32, x.shape, 1)
    first = (lane & (HEAD_DIM - 1)) < half
    swapped = jnp.where(first, pltpu.roll(x, LANES - half, axis=1), pltpu.roll(x, half, axis=1))
    return x * cos + swapped * sin


def _dup_halves(x):
    lane = lax.broadcasted_iota(jnp.int32, x.shape, 1)
    r = pltpu.roll(x, HEAD_DIM, axis=1)
    lo = lane < HEAD_DIM
    return jnp.concatenate([jnp.where(lo, x, r), jnp.where(lo, r, x)], axis=-1)


def _odd_in_kernel(x_ref, g_ref, wq_ref, wkv_ref, wu_ref, qg_ref, kg_ref, ind_ref, perm_ref,
                   cos_ref, sin_ref, q_ref, kd_ref, vd_ref, u_ref):
    B, ts, D = x_ref.shape
    rows = B * ts
    n = _rms(x_ref[...].reshape(rows, D), g_ref[...]).astype(BF16)
    u_ref[...] = _dot(_permute_rows(perm_ref[...], n), wu_ref[...])
    cos = jnp.concatenate([cos_ref[...]] * B, axis=0)
    sin = jnp.concatenate([sin_ref[...]] * B, axis=0)
    q = _head_rms(_dot(n, wq_ref[...]), ind_ref[...], qg_ref[...])
    q = jnp.concatenate([_rope(q[:, blk * LANES:(blk + 1) * LANES], cos, sin)
                         for blk in range(SWA_DIM // LANES)], axis=-1) * QK_SCALE
    q_ref[...] = q.astype(BF16).reshape(B, ts, SWA_DIM)
    kv = _dot(n, wkv_ref[...])
    k = _head_rms(kv[:, :LANES], ind_ref[0:LANES, 0:LANES], kg_ref[...])
    kd_ref[...] = _dup_halves(_rope(k, cos, sin)).astype(BF16).reshape(B, ts, 2 * LANES)
    vd_ref[...] = _dup_halves(kv[:, LANES:]).astype(BF16).reshape(B, ts, 2 * LANES)


def _odd_inproj(x, g, w_q, w_kv, w_u, qg, kg, cos, sin):
    B, S, D = x.shape
    ts = TIME_CHUNK
    bmaj = lambda w: pl.BlockSpec((B, ts, w), lambda s: (0, s, 0))
    tab = pl.BlockSpec((ts, LANES), lambda s: (s, 0))
    consts = [g, w_q, w_kv, w_u, qg, kg, _head_mean_matrix(SWA_DIM), _time_major_perm(ts)]
    return pl.pallas_call(
        _odd_in_kernel, grid=(S // ts,),
        in_specs=[bmaj(D)] + [_const_spec(a.shape) for a in consts] + [tab, tab],
        out_specs=[bmaj(SWA_DIM), bmaj(2 * LANES), bmaj(2 * LANES),
                   pl.BlockSpec((B * ts, S5_WIDTH), lambda s: (s, 0))],
        out_shape=[jax.ShapeDtypeStruct((B, S, SWA_DIM), BF16),
                   jax.ShapeDtypeStruct((B, S, 2 * LANES), BF16),
                   jax.ShapeDtypeStruct((B, S, 2 * LANES), BF16),
                   jax.ShapeDtypeStruct((S * B, S5_WIDTH), F32)],
        compiler_params=_params("parallel"), name="odd_inproj",
    )(x, *consts, cos, sin)


def _swa_kernel(q_ref, kd_ref, vd_ref, sink_ref, o_ref):
    tq = q_ref.shape[0]
    W = SWA_WINDOW
    G = SWA_GROUP
    base = pl.program_id(1) * tq
    row = lax.broadcasted_iota(jnp.int32, (G * W, 2 * W), 0)
    col = lax.broadcasted_iota(jnp.int32, (G * W, 2 * W), 1)
    qoff = row & (W - 1)
    rgrp = lax.broadcasted_iota(jnp.int32, (G * W, 1), 0) // W
    lane = lax.broadcasted_iota(jnp.int32, (W, LANES), 1)
    for n in range(tq // W):
        rows = slice(n * W, (n + 1) * W)
        r0 = base + n * W
        kstart = pl.multiple_of(jnp.maximum(r0 - W, 0), W)
        diff = (r0 + qoff) - (kstart + col)
        valid = (diff >= 0) & (diff < W)
        for kvh in range(SWA_KV_HEADS):
            kv_lanes = slice(kvh * LANES, (kvh + 1) * LANES)
            parts = []
            for g in range(G):
                head = kvh * G + g
                blk = q_ref[rows, (head // 2) * LANES:(head // 2 + 1) * LANES]
                half = head % 2
                parts.append(_keep_lanes(blk, half * HEAD_DIM, (half + 1) * HEAD_DIM))
            s = _dot_nt(jnp.concatenate(parts, axis=0), kd_ref[pl.ds(kstart, 2 * W), kv_lanes])
            s = jnp.where(valid, s, -jnp.inf)
            sink = jnp.zeros((G * W, 1), F32)
            for g in range(G):
                head = kvh * G + g
                sink = jnp.where(rgrp == g, sink_ref[:, head:head + 1], sink)
            m = jnp.maximum(jnp.max(s, axis=-1, keepdims=True), sink)
            e = jnp.exp(s - m)
            pr = e / (jnp.sum(e, axis=-1, keepdims=True) + jnp.exp(sink - m))
            o = _dot(pr.astype(BF16), vd_ref[pl.ds(kstart, 2 * W), kv_lanes])
            for pair in range(G // 2):
                left = o[(2 * pair) * W:(2 * pair + 1) * W, :]
                right = o[(2 * pair + 1) * W:(2 * pair + 2) * W, :]
                lb = kvh * (G // 2) + pair
                o_ref[rows, lb * LANES:(lb + 1) * LANES] = jnp.where(
                    lane < HEAD_DIM, left, right).astype(BF16)


def _swa_attention(q, kd, vd, sinks):
    B, S, _ = q.shape
    tq = min(ATTN_TILE, S)
    return pl.pallas_call(
        _swa_kernel, grid=(B, S // tq),
        in_specs=[pl.BlockSpec((None, tq, SWA_DIM), lambda b, i: (b, i, 0)),
                  pl.BlockSpec((None, S, 2 * LANES), lambda b, i: (b, 0, 0)),
                  pl.BlockSpec((None, S, 2 * LANES), lambda b, i: (b, 0, 0)),
                  pl.BlockSpec((1, SWA_HEADS), lambda b, i: (0, 0))],
        out_specs=pl.BlockSpec((None, tq, SWA_DIM), lambda b, i: (b, i, 0)),
        out_shape=jax.ShapeDtypeStruct((B, S, SWA_DIM), BF16),
        compiler_params=_params("parallel", "arbitrary"), name="swa_attention",
    )(q, kd, vd, sinks)


def _s5_prep_kernel(lr_ref, li_ref, ldt_ref, br_ref, bi_ref, ar_ref, ai_ref, bbr_ref, bbi_ref):
    lr, li = lr_ref[...], li_ref[...]
    dt = jnp.exp(ldt_ref[...])
    mag = jnp.exp(lr * dt)
    ar = mag * jnp.cos(li * dt)
    ai = mag * jnp.sin(li * dt)
    den = lr * lr + li * li
    cr = ((ar - 1.0) * lr + ai * li) / den
    ci = (ai * lr - (ar - 1.0) * li) / den
    br, bi = br_ref[...], bi_ref[...]
    ar_ref[...] = ar
    ai_ref[...] = ai
    bbr_ref[...] = cr * br - ci * bi
    bbi_ref[...] = cr * bi + ci * br


def _s5_prep(lam_re, lam_im, log_dt, b_re, b_im):
    G, P, C = b_re.shape
    rep = lambda a: jnp.repeat(a, C, axis=0)
    bt = lambda a: a.transpose(0, 2, 1).reshape(G * C, P)
    ldt = jnp.broadcast_to(log_dt[:, None], (G, P))
    full = pl.BlockSpec((G * C, P), lambda: (0, 0))
    outs = pl.pallas_call(
        _s5_prep_kernel, in_specs=[full] * 5, out_specs=[full] * 4,
        out_shape=[jax.ShapeDtypeStruct((G * C, P), F32)] * 4, name="s5_prep",
    )(rep(lam_re), rep(lam_im), rep(ldt), bt(b_re), bt(b_im))
    ar, ai, bbr, bbi = [o.reshape(G, C, P) for o in outs]
    return ar[:, 0], ai[:, 0], bbr, bbi


def _s5_kernel(u_ref, bm_ref, cm_ref, ar_ref, ai_ref, d_ref, gw_ref, gb_ref, o_ref,
               h_ref, carry_ref):
    rows = u_ref.shape[0]
    half = h_ref.shape[2] // 2

    @pl.when(pl.program_id(0) == 0)
    def _():
        carry_ref[...] = jnp.zeros_like(carry_ref)

    u = u_ref[...]
    ub = u.astype(BF16)
    ys = []
    for g in range(S5_LANE_GROUPS):
        h_ref[g] = _dot(ub[:, g * LANES:(g + 1) * LANES], bm_ref[g])
        ar = jnp.broadcast_to(ar_ref[g], (SUBLANES, half))
        ai = jnp.broadcast_to(ai_ref[g], (SUBLANES, half))

        def step(t, carry, g=g, ar=ar, ai=ai):
            hr, hi = carry
            sl = pl.ds(pl.multiple_of(t * SUBLANES, SUBLANES), SUBLANES)
            nr = ar * hr - ai * hi + h_ref[g, sl, 0:half]
            ni = ar * hi + ai * hr + h_ref[g, sl, half:2 * half]
            h_ref[g, sl, 0:half] = nr
            h_ref[g, sl, half:2 * half] = ni
            return nr, ni

        hr, hi = lax.fori_loop(0, rows // SUBLANES, step,
                               (carry_ref[g, :, 0:half], carry_ref[g, :, half:2 * half]), unroll=8)
        carry_ref[g, :, 0:half] = hr
        carry_ref[g, :, half:2 * half] = hi
        ys.append(_dot(h_ref[g].astype(BF16), cm_ref[g]))
    y = jnp.concatenate(ys, axis=-1) + d_ref[...] * u
    z = jax.nn.gelu(y)
    o_ref[...] = (z * jax.nn.sigmoid(_dot(z.astype(BF16), gw_ref[...]) + gb_ref[...])).astype(BF16)


def _s5(u_t, bmat, cmat, ar, ai, d, glu_w, glu_b):
    R, W = u_t.shape
    rows = TIME_CHUNK * SUBLANES
    nstate = bmat.shape[2]
    blk = pl.BlockSpec((rows, W), lambda t: (t, 0))
    return pl.pallas_call(
        _s5_kernel, grid=(R // rows,),
        in_specs=[blk, _const_spec(bmat.shape), _const_spec(cmat.shape), _const_spec(ar.shape),
                  _const_spec(ai.shape), _const_spec((1, W)), _const_spec((W, W)),
                  _const_spec((1, W))],
        out_specs=blk, out_shape=jax.ShapeDtypeStruct((R, W), BF16),
        scratch_shapes=[pltpu.VMEM((S5_LANE_GROUPS, rows, nstate), F32),
                        pltpu.VMEM((S5_LANE_GROUPS, SUBLANES, nstate), F32)],
        compiler_params=_params("arbitrary"), name="s5_glu",
    )(u_t, bmat, cmat, ar, ai, d, glu_w, glu_b)


def _s5_matrices(ar, ai, bbr, bbi, c_re, c_im):
    L, GL = S5_LANE_GROUPS, S5_GROUPS // S5_LANE_GROUPS
    C, P = S5_GROUP, S5_STATE
    eye = jnp.eye(GL, dtype=F32)

    def inmap(b):
        return jnp.einsum("lgcp,gh->lgchp", b.reshape(L, GL, C, P), eye).reshape(L, GL * C, GL * P)

    def outmap(c):
        return jnp.einsum("lgcp,gh->lgphc", c.reshape(L, GL, C, P), eye).reshape(L, GL * P, GL * C)

    bmat = jnp.concatenate([inmap(bbr), inmap(bbi)], axis=2).astype(BF16)
    cmat = jnp.concatenate([outmap(c_re), outmap(-c_im)], axis=1).astype(BF16)
    a_r = ar.reshape(L, 1, GL * P)
    a_i = ai.reshape(L, 1, GL * P)
    return bmat, cmat, a_r, a_i


def _block_diag_pairs(w):
    nb, bs, _ = w.shape
    half = nb // 2
    eye = jnp.eye(half, dtype=w.dtype)
    out = jnp.einsum("thij,hk->thikj", w.reshape(2, half, bs, bs), eye)
    return out.reshape(2, half * bs, half * bs).astype(BF16)


def kernel(x, p, ffn1_norm, ffn1_wg, ffn1_wu, ffn1_wd, mix_norm, ffn2_norm, ffn2_wg, ffn2_wu, ffn2_wd, ple_w, ple_norm, ple_gate_norm, ple_gate_w, ev_w_in, lru_conv_w, lru_conv_b, lru_wa, lru_ba, lru_wx, lru_bx, lru_lambda, fox_bf, fox_q_norm, fox_k_norm, ev_w_out, od_w_in, swa_q_norm, swa_k_norm, swa_sinks, s5_lambda_re, s5_lambda_im, s5_log_dt, s5_b_re, s5_b_im, s5_c_re, s5_c_im, s5_d, s5_glu_w, s5_glu_b, od_w_out):
    B, S, D = x.shape
    depth = p.shape[0]
    assert B == SUBLANES and D == D_MODEL and S % TIME_CHUNK == 0
    T = B * S
    bf = lambda a: a.astype(BF16)
    row = lambda a: a[:, None, :]
    per_head = lambda gain, heads: jnp.tile(gain, heads)[None, :]

    f1 = (row(ffn1_norm), bf(ffn1_wg), bf(ffn1_wu), bf(ffn1_wd))
    f2 = (row(ffn2_norm), bf(ffn2_wg), bf(ffn2_wu), bf(ffn2_wd))
    ple = (p.reshape(depth, T, PLE_DIM), bf(ple_w), row(ple_norm), row(ple_gate_norm),
           bf(ple_gate_w))
    cos, sin = _rope_tables(S)

    for i in range(depth):
        j = i // 2
        x = _ffn(x.reshape(T, D), i, *f1).reshape(B, S, D)
        g = mix_norm[i][None, :]
        if i % 2 == 0:
            w_in = bf(ev_w_in[j])
            o1, o2 = 2 * LRU_WIDTH, 2 * LRU_WIDTH + 3 * FOX_DIM
            w_f = jnp.pad(w_in[:, o2:], ((0, 0), (0, LANES - FOX_HEADS)))
            b_f = jnp.pad(fox_bf[j], (0, LANES - FOX_HEADS))[None, :]
            xa, ya, q, k, qe, ke, v = _even_inproj(
                x, g, w_in[:, :o1], w_in[:, o1:o2], w_f, b_f,
                per_head(fox_q_norm[j], FOX_HEADS), per_head(fox_k_norm[j], FOX_HEADS))
            a_out = _lru(xa, ya, lru_conv_w[j], lru_conv_b[j][None, :],
                         _block_diag_pairs(lru_wa[j]), lru_ba[j][None, :],
                         _block_diag_pairs(lru_wx[j]), lru_bx[j][None, :],
                         lru_lambda[j][None, :])
            b_out = _fox_attention(q, qe, k, ke, _value_transpose(v))
            w_out = bf(ev_w_out[j])
            x = _outproj(x, a_out, b_out, w_out[:LRU_WIDTH], w_out[LRU_WIDTH:])
        else:
            w_in = bf(od_w_in[j])
            kvd = SWA_KV_HEADS * HEAD_DIM
            o1, o2 = SWA_DIM, SWA_DIM + 2 * kvd
            q, kd, vd, u = _odd_inproj(
                x, g, w_in[:, :o1], w_in[:, o1:o2], w_in[:, o2:],
                per_head(swa_q_norm[j], SWA_HEADS), per_head(swa_k_norm[j], SWA_KV_HEADS), cos, sin)
            c_out = _swa_attention(q, kd, vd, swa_sinks[j][None, :])
            ar, ai, bbr, bbi = _s5_prep(s5_lambda_re[j], s5_lambda_im[j], s5_log_dt[j],
                                        s5_b_re[j], s5_b_im[j])
            bmat, cmat, a_r, a_i = _s5_matrices(ar, ai, bbr, bbi, s5_c_re[j], s5_c_im[j])
            d_out = _s5(u, bmat, cmat, a_r, a_i, s5_d[j][None, :],
                        bf(s5_glu_w[j]), s5_glu_b[j][None, :])
            w_out = bf(od_w_out[j])
            x = _outproj(x, d_out, c_out, w_out[SWA_DIM:], w_out[:SWA_DIM])
        x = _ffn(x.reshape(T, D), i, *f2, ple=ple).reshape(B, S, D)
    return x
```

```python
import numpy as np

import jax
import jax.numpy as jnp
from jax import lax
from jax.experimental import pallas as pl
from jax.experimental.pallas import tpu as pltpu

F32 = jnp.float32
BF16 = jnp.bfloat16

D_MODEL = 1024
HEAD_DIM = 64
LRU_WIDTH = 512
LRU_CONV = 4
LRU_C = 8.0
FOX_HEADS = 8
FOX_DIM = 512
SWA_HEADS = 8
SWA_KV_HEADS = 2
SWA_GROUP = SWA_HEADS // SWA_KV_HEADS
SWA_DIM = 512
SWA_WINDOW = 128
S5_WIDTH = 512
S5_GROUP = 16
S5_GROUPS = 32
S5_STATE = 64
D_FF = 2816
PLE_DIM = 256
ROPE_THETA = 10000.0
EPS = 1e-6
MACARON = 0.5
QK_SCALE = HEAD_DIM ** -0.5
LOG2E = 1.4426950408889634

SUBLANES = 8
LANES = 128
TIME_CHUNK = 64
ROW_TILE = SUBLANES * TIME_CHUNK
ATTN_TILE = 512
S5_LANE_GROUPS = 4
BIAS_LANES = 6
VT_ROWS = 80
VMEM_LIMIT = 56 * 1024 * 1024


def _dot(a, b):
    return jnp.dot(a, b, preferred_element_type=F32)


def _dot_nt(a, b):
    return lax.dot_general(a, b, (((1,), (1,)), ((), ())), preferred_element_type=F32)


def _rms(x, g):
    ms = jnp.mean(x * x, axis=-1, keepdims=True)
    return x * lax.rsqrt(ms + EPS) * g


def _head_rms(x, ind, gain):
    sq = x * x
    hi = sq.astype(BF16)
    lo = (sq - hi.astype(F32)).astype(BF16)
    ms = _dot(hi, ind) + _dot(lo, ind)
    return x * lax.rsqrt(ms + EPS) * gain


def _softplus(x):
    return jnp.maximum(x, 0.0) + jnp.log1p(jnp.exp(-jnp.abs(x)))


def _log_sigmoid(x):
    return -_softplus(-x)


def _cumsum_time(x):
    n = x.shape[0]
    row = lax.broadcasted_iota(jnp.int32, x.shape, 0)
    d = SUBLANES
    while d < n:
        x = x + jnp.where(row >= d, pltpu.roll(x, d, axis=0), 0.0)
        d *= 2
    return x


def _time_major_perm(steps):
    r = np.arange(SUBLANES * steps)
    src = (r % SUBLANES) * steps + r // SUBLANES
    return jnp.asarray(src[:, None] == r[None, :], dtype=BF16)


def _permute_rows(perm, x):
    return _dot(perm, x).astype(BF16)


def _split3_bf16(c):
    hi = c.astype(BF16).astype(F32)
    r = c - hi
    mid = r.astype(BF16).astype(F32)
    return hi, mid, r - mid


def _keep_lanes(x, lo, hi):
    lane = lax.broadcasted_iota(jnp.int32, x.shape, 1)
    return jnp.where((lane >= lo) & (lane < hi), x.astype(F32), 0.0).astype(BF16)


def _params(*sem):
    return pltpu.CompilerParams(dimension_semantics=sem, vmem_limit_bytes=VMEM_LIMIT)


def _const_spec(shape):
    nd = len(shape)
    return pl.BlockSpec(shape, lambda *_: (0,) * nd, pipeline_mode=pl.Buffered(1))


def _layer_spec(shape, layer):
    nd = len(shape)
    return pl.BlockSpec((None,) + tuple(shape), lambda *_: (layer,) + (0,) * nd,
                        pipeline_mode=pl.Buffered(1))


def _head_mean_matrix(width):
    h = np.arange(width) // HEAD_DIM
    return jnp.asarray((h[:, None] == h[None, :]) / HEAD_DIM, dtype=BF16)


def _swiglu_update(x, g_ref, wg_ref, wu_ref, wd_ref):
    n = _rms(x, g_ref[...]).astype(BF16)
    hg = _dot(n, wg_ref[...])
    hu = _dot(n, wu_ref[...])
    act = (hg * jax.nn.sigmoid(hg) * hu).astype(BF16)
    return x + MACARON * _dot(act, wd_ref[...])


def _ffn_kernel(x_ref, g_ref, wg_ref, wu_ref, wd_ref, o_ref):
    o_ref[...] = _swiglu_update(x_ref[...], g_ref, wg_ref, wu_ref, wd_ref)


def _ffn_ple_kernel(x_ref, g_ref, wg_ref, wu_ref, wd_ref, p_ref, pw_ref, pn_ref, gn_ref, gw_ref,
                    o_ref):
    x = _swiglu_update(x_ref[...], g_ref, wg_ref, wu_ref, wd_ref)
    e = _rms(_dot(p_ref[...].astype(BF16), pw_ref[...]), pn_ref[...])
    gate = jax.nn.sigmoid(_dot(_rms(x, gn_ref[...]).astype(BF16), gw_ref[...]))
    o_ref[...] = x + gate * e


def _ffn(x2d, layer, norm, wg, wu, wd, ple=None):
    T, D = x2d.shape
    tm = min(ROW_TILE, T)
    row = pl.BlockSpec((tm, D), lambda i: (i, 0))
    in_specs = [row, _layer_spec((1, D), layer), _layer_spec((D, D_FF), layer),
                _layer_spec((D, D_FF), layer), _layer_spec((D_FF, D), layer)]
    args = [x2d, norm, wg, wu, wd]
    kern = _ffn_kernel
    if ple is not None:
        p, pw, pn, gn, gw = ple
        in_specs += [pl.BlockSpec((None, tm, PLE_DIM), lambda i: (layer, i, 0)),
                     _layer_spec((PLE_DIM, D), layer), _layer_spec((1, D), layer),
                     _layer_spec((1, D), layer), _layer_spec((D, D), layer)]
        args += [p, pw, pn, gn, gw]
        kern = _ffn_ple_kernel
    return pl.pallas_call(
        kern, grid=(T // tm,), in_specs=in_specs, out_specs=row,
        out_shape=jax.ShapeDtypeStruct((T, D), F32),
        compiler_params=_params("parallel"), name="ffn_ple" if ple is not None else "ffn",
    )(*args)


def _outproj_kernel(x_ref, tmaj_ref, bmaj_ref, wt_ref, wb_ref, unperm_ref, o_ref):
    B, ts, D = x_ref.shape
    part_t = _permute_rows(unperm_ref[...], tmaj_ref[...])
    part_b = bmaj_ref[...].reshape(B * ts, bmaj_ref.shape[2])
    upd = _dot(part_t, wt_ref[...]) + _dot(part_b, wb_ref[...])
    o_ref[...] = x_ref[...] + upd.reshape(B, ts, D)


def _outproj(x, part_tmaj, part_bmaj, w_tmaj, w_bmaj):
    B, S, D = x.shape
    ts = TIME_CHUNK
    W = part_tmaj.shape[1]
    unperm = _time_major_perm(ts).T
    xs = pl.BlockSpec((B, ts, D), lambda s: (0, s, 0))
    return pl.pallas_call(
        _outproj_kernel, grid=(S // ts,),
        in_specs=[xs, pl.BlockSpec((B * ts, W), lambda s: (s, 0)),
                  pl.BlockSpec((B, ts, W), lambda s: (0, s, 0)),
                  _const_spec((W, D)), _const_spec((W, D)), _const_spec(unperm.shape)],
        out_specs=xs, out_shape=jax.ShapeDtypeStruct((B, S, D), F32),
        compiler_params=_params("parallel"), name="outproj",
    )(x, part_tmaj, part_bmaj, w_tmaj, w_bmaj, unperm)


def _bias_lane_maps():
    eq = np.zeros((3 * LANES, LANES), np.float32)
    ek = np.zeros((3 * LANES, LANES), np.float32)
    oq = np.zeros((1, LANES), np.float32)
    ok = np.zeros((1, LANES), np.float32)
    for h in range(FOX_HEADS):
        for i in range(3):
            eq[i * LANES + h, BIAS_LANES * h + i] = 1.0
            ek[i * LANES + h, BIAS_LANES * h + 3 + i] = -1.0
            oq[0, BIAS_LANES * h + 3 + i] = 1.0
            ok[0, BIAS_LANES * h + i] = 1.0
    return jnp.asarray(eq, BF16), jnp.asarray(ek, BF16), jnp.asarray(oq), jnp.asarray(ok)


def _even_in_kernel(x_ref, g_ref, wxy_ref, wqkv_ref, wf_ref, bf_ref, qg_ref, kg_ref, ind_ref,
                    eq_ref, ek_ref, oq_ref, ok_ref, perm_ref, unperm_ref,
                    xa_ref, ya_ref, q_ref, k_ref, qe_ref, ke_ref, v_ref, carry_ref):
    B, ts, D = x_ref.shape
    rows = B * ts

    @pl.when(pl.program_id(0) == 0)
    def _():
        carry_ref[...] = jnp.zeros_like(carry_ref)

    n = _rms(x_ref[...].reshape(rows, D), g_ref[...]).astype(BF16)
    n_t = _permute_rows(perm_ref[...], n)
    xy = _dot(n_t, wxy_ref[...])
    xa_ref[...] = xy[:, :LRU_WIDTH]
    ya_ref[...] = xy[:, LRU_WIDTH:]

    logf = _log_sigmoid(_dot(n_t, wf_ref[...]) + bf_ref[...]) * LOG2E
    c = _cumsum_time(logf) + pltpu.repeat(carry_ref[...], ts, axis=0)
    carry_ref[...] = c[rows - SUBLANES:rows, :]
    pieces = jnp.concatenate(_split3_bf16(c), axis=-1).astype(BF16)
    unperm = unperm_ref[...]
    qe = (_dot(pieces, eq_ref[...]) + oq_ref[...]).astype(BF16)
    ke = (_dot(pieces, ek_ref[...]) + ok_ref[...]).astype(BF16)
    qe_ref[...] = _permute_rows(unperm, qe).reshape(B, ts, LANES)
    ke_ref[...] = _permute_rows(unperm, ke).reshape(B, ts, LANES)

    qkv = _dot(n, wqkv_ref[...])
    ind = ind_ref[...]
    q = _head_rms(qkv[:, :FOX_DIM], ind, qg_ref[...]) * (QK_SCALE * LOG2E)
    k = _head_rms(qkv[:, FOX_DIM:2 * FOX_DIM], ind, kg_ref[...])
    q_ref[...] = q.astype(BF16).reshape(B, ts, FOX_DIM)
    k_ref[...] = k.astype(BF16).reshape(B, ts, FOX_DIM)
    v_ref[...] = qkv[:, 2 * FOX_DIM:].astype(BF16).reshape(B, ts, FOX_DIM)


def _even_inproj(x, g, w_xy, w_qkv, w_f, b_f, qg, kg):
    B, S, D = x.shape
    ts = TIME_CHUNK
    eq, ek, oq, ok = _bias_lane_maps()
    perm = _time_major_perm(ts)
    tmaj = pl.BlockSpec((B * ts, LRU_WIDTH), lambda s: (s, 0))
    bmaj = lambda w: pl.BlockSpec((B, ts, w), lambda s: (0, s, 0))
    consts = [g, w_xy, w_qkv, w_f, b_f, qg, kg, _head_mean_matrix(FOX_DIM), eq, ek, oq, ok,
              perm, perm.T]
    return pl.pallas_call(
        _even_in_kernel, grid=(S // ts,),
        in_specs=[bmaj(D)] + [_const_spec(a.shape) for a in consts],
        out_specs=[tmaj, tmaj, bmaj(FOX_DIM), bmaj(FOX_DIM), bmaj(LANES), bmaj(LANES),
                   bmaj(FOX_DIM)],
        out_shape=[jax.ShapeDtypeStruct((S * B, LRU_WIDTH), F32),
                   jax.ShapeDtypeStruct((S * B, LRU_WIDTH), F32),
                   jax.ShapeDtypeStruct((B, S, FOX_DIM), BF16),
                   jax.ShapeDtypeStruct((B, S, FOX_DIM), BF16),
                   jax.ShapeDtypeStruct((B, S, LANES), BF16),
                   jax.ShapeDtypeStruct((B, S, LANES), BF16),
                   jax.ShapeDtypeStruct((B, S, FOX_DIM), BF16)],
        scratch_shapes=[pltpu.VMEM((SUBLANES, LANES), F32)],
        compiler_params=_params("arbitrary"), name="even_inproj",
    )(x, *consts)


def _vt_kernel(v_ref, o_ref):
    heads, nchunk, _, chunk = o_ref.shape
    vt = v_ref[...].astype(F32).T
    ones = jnp.ones((VT_ROWS - HEAD_DIM, chunk), BF16)
    for h in range(heads):
        for c in range(nchunk):
            o_ref[h, c, 0:HEAD_DIM, :] = vt[h * HEAD_DIM:(h + 1) * HEAD_DIM,
                                            c * chunk:(c + 1) * chunk].astype(BF16)
            o_ref[h, c, HEAD_DIM:VT_ROWS, :] = ones


def _value_transpose(v, chunk):
    B, S, W = v.shape
    heads = W // HEAD_DIM
    tk = min(ATTN_TILE, S)
    chunk = min(chunk, tk)
    return pl.pallas_call(
        _vt_kernel, grid=(B, S // tk),
        in_specs=[pl.BlockSpec((None, tk, W), lambda b, j: (b, j, 0))],
        out_specs=pl.BlockSpec((None, heads, tk // chunk, VT_ROWS, chunk),
                               lambda b, j: (b, 0, j, 0, 0)),
        out_shape=jax.ShapeDtypeStruct((B, heads, S // chunk, VT_ROWS, chunk), BF16),
        compiler_params=_params("parallel", "parallel"), name="value_transpose",
    )(v)


def _lru_kernel(xa_ref, ya_ref, cw_ref, cb_ref, wa_ref, ba_ref, wx_ref, bx_ref, lam_ref, o_ref,
                xpad_ref, a_ref, h_ref, carry_ref):
    rows = xa_ref.shape[0]
    halo = (LRU_CONV - 1) * SUBLANES
    half = LRU_WIDTH // 2

    @pl.when(pl.program_id(0) == 0)
    def _():
        xpad_ref[0:halo, :] = jnp.zeros((halo, LRU_WIDTH), F32)
        carry_ref[...] = jnp.zeros_like(carry_ref)

    xpad_ref[halo:halo + rows, :] = xa_ref[...]
    xc = cb_ref[...]
    for tap in range(LRU_CONV):
        xc = xc + xpad_ref[tap * SUBLANES:tap * SUBLANES + rows, :] * cw_ref[tap:tap + 1, :]
    xpad_ref[0:halo, :] = xpad_ref[rows:rows + halo, :]

    xb = xc.astype(BF16)

    def gate(w_ref, b_ref):
        z = jnp.concatenate([_dot(xb[:, :half], w_ref[0]), _dot(xb[:, half:], w_ref[1])], axis=-1)
        return jax.nn.sigmoid(z + b_ref[...])

    r = gate(wa_ref, ba_ref)
    i = gate(wx_ref, bx_ref)
    log_a = -LRU_C * r * _softplus(lam_ref[...])
    a_ref[...] = jnp.exp(log_a)
    th = jnp.tanh(log_a)
    h_ref[...] = jnp.sqrt(-2.0 * th / (1.0 - th)) * (i * xc)

    def step(t, h):
        sl = pl.ds(pl.multiple_of(t * SUBLANES, SUBLANES), SUBLANES)
        h = a_ref[sl, :] * h + h_ref[sl, :]
        h_ref[sl, :] = h
        return h

    carry_ref[...] = lax.fori_loop(0, rows // SUBLANES, step, carry_ref[...], unroll=8)
    o_ref[...] = (jax.nn.gelu(ya_ref[...]) * h_ref[...]).astype(BF16)


def _lru(xa_t, ya_t, conv_w, conv_b, wa_bd, ba, wx_bd, bx, lam):
    R, W = xa_t.shape
    rows = TIME_CHUNK * SUBLANES
    halo = (LRU_CONV - 1) * SUBLANES
    blk = pl.BlockSpec((rows, W), lambda t: (t, 0))
    return pl.pallas_call(
        _lru_kernel, grid=(R // rows,),
        in_specs=[blk, blk, _const_spec((LRU_CONV, W)), _const_spec((1, W)),
                  _const_spec(wa_bd.shape), _const_spec((1, W)),
                  _const_spec(wx_bd.shape), _const_spec((1, W)), _const_spec((1, W))],
        out_specs=blk, out_shape=jax.ShapeDtypeStruct((R, W), BF16),
        scratch_shapes=[pltpu.VMEM((halo + rows, W), F32), pltpu.VMEM((rows, W), F32),
                        pltpu.VMEM((rows, W), F32), pltpu.VMEM((SUBLANES, W), F32)],
        compiler_params=_params("arbitrary"), name="rg_lru",
    )(xa_t, ya_t, conv_w, conv_b, wa_bd, ba, wx_bd, bx, lam)


def _fox_kernel(q_ref, qe_ref, k_ref, ke_ref, vt_ref, o_ref, acc_ref):
    tq = q_ref.shape[0]
    qi = pl.program_id(1)
    qe = qe_ref[...]
    qq = []
    for h in range(FOX_HEADS):
        half = h % 2
        blk = q_ref[:, (h // 2) * LANES:(h // 2 + 1) * LANES]
        qq.append(jnp.concatenate(
            [_keep_lanes(blk, half * HEAD_DIM, (half + 1) * HEAD_DIM),
             _keep_lanes(qe, BIAS_LANES * h, BIAS_LANES * (h + 1))], axis=-1))
    key = lax.broadcasted_iota(jnp.int32, (tq, tq), 0)
    qry = lax.broadcasted_iota(jnp.int32, (tq, tq), 1)
    acc_ref[...] = jnp.zeros_like(acc_ref)

    def update(j, m_run, masked):
        ks = pl.ds(pl.multiple_of(j * tq, tq), tq)
        ke = ke_ref[ks, :]
        scores = []
        for h in range(FOX_HEADS):
            kk = jnp.concatenate([k_ref[ks, (h // 2) * LANES:(h // 2 + 1) * LANES], ke], axis=-1)
            scores.append(_dot_nt(kk, qq[h]))
        new = []
        for h, s in enumerate(scores):
            if masked:
                s = jnp.where(key <= qry, s, -jnp.inf)
            m_new = jnp.maximum(m_run[h], jnp.max(s, axis=0, keepdims=True))
            alpha = jnp.exp2(m_run[h] - m_new)
            p = jnp.exp2(s - m_new).astype(BF16)
            acc_ref[h] = alpha * acc_ref[h] + _dot(vt_ref[h, j], p)
            new.append(m_new)
        return tuple(new)

    init = (jnp.full((1, tq), -jnp.inf, F32),) * FOX_HEADS
    m_run = lax.fori_loop(0, qi, lambda j, m: update(j, m, False), init)
    update(qi, m_run, True)
    for pair in range(FOX_HEADS // 2):
        out = [acc_ref[h, 0:HEAD_DIM, :] / acc_ref[h, HEAD_DIM:HEAD_DIM + 1, :]
               for h in (2 * pair, 2 * pair + 1)]
        o_ref[:, pair * LANES:(pair + 1) * LANES] = jnp.concatenate(out, axis=0).T.astype(BF16)


def _fox_attention(q, qe, k, ke, vt):
    B, S, _ = q.shape
    nk, tq = vt.shape[2], vt.shape[4]
    qblk = lambda w: pl.BlockSpec((None, tq, w), lambda b, i: (b, i, 0))
    kblk = lambda w: pl.BlockSpec((None, S, w), lambda b, i: (b, 0, 0))
    return pl.pallas_call(
        _fox_kernel, grid=(B, S // tq),
        in_specs=[qblk(FOX_DIM), qblk(LANES), kblk(FOX_DIM), kblk(LANES),
                  pl.BlockSpec((None, FOX_HEADS, nk, VT_ROWS, tq), lambda b, i: (b, 0, 0, 0, 0))],
        out_specs=qblk(FOX_DIM),
        out_shape=jax.ShapeDtypeStruct((B, S, FOX_DIM), BF16),
        scratch_shapes=[pltpu.VMEM((FOX_HEADS, VT_ROWS, tq), F32)],
        compiler_params=_params("parallel", "arbitrary"), name="fox_attention",
    )(q, qe, k, ke, vt)


def _rope_table_kernel(inv_ref, cos_ref, sin_ref):
    rows = cos_ref.shape[0]
    pos = pl.program_id(0) * rows + lax.broadcasted_iota(jnp.int32, cos_ref.shape, 0)
    lane = lax.broadcasted_iota(jnp.int32, cos_ref.shape, 1)
    ang = pos.astype(F32) * inv_ref[...]
    cos_ref[...] = jnp.cos(ang)
    sin_ref[...] = jnp.where((lane & (HEAD_DIM - 1)) < HEAD_DIM // 2, -1.0, 1.0) * jnp.sin(ang)


def _rope_tables(S):
    half = HEAD_DIM // 2
    inv = jnp.power(ROPE_THETA, -jnp.arange(half, dtype=F32) / half)
    inv4 = jnp.tile(inv, LANES // half)[None, :]
    rows = min(ROW_TILE, S)
    blk = pl.BlockSpec((rows, LANES), lambda i: (i, 0))
    return pl.pallas_call(
        _rope_table_kernel, grid=(S // rows,),
        in_specs=[pl.BlockSpec((1, LANES), lambda i: (0, 0))], out_specs=[blk, blk],
        out_shape=[jax.ShapeDtypeStruct((S, LANES), F32)] * 2,
        compiler_params=_params("parallel"), name="rope_tables",
    )(inv4)


def _rope(x, cos, sin):
    half = HEAD_DIM // 2
    lane = lax.broadcasted_iota(jnp.int32, x.shape, 1)
    first = (lane & (HEAD_DIM - 1)) < half
    swapped = jnp.where(first, pltpu.roll(x, LANES - half, axis=1), pltpu.roll(x, half, axis=1))
    return x * cos + swapped * sin


def _dup_halves(x):
    lane = lax.broadcasted_iota(jnp.int32, x.shape, 1)
    r = pltpu.roll(x, HEAD_DIM, axis=1)
    lo = lane < HEAD_DIM
    return jnp.concatenate([jnp.where(lo, x, r), jnp.where(lo, r, x)], axis=-1)


def _odd_in_kernel(x_ref, g_ref, wq_ref, wkv_ref, wu_ref, qg_ref, kg_ref, ind_ref, perm_ref,
                   cos_ref, sin_ref, q_ref, kd_ref, v_ref, u_ref):
    B, ts, D = x_ref.shape
    rows = B * ts
    n = _rms(x_ref[...].reshape(rows, D), g_ref[...]).astype(BF16)
    u_ref[...] = _dot(_permute_rows(perm_ref[...], n), wu_ref[...])
    cos = jnp.concatenate([cos_ref[...]] * B, axis=0)
    sin = jnp.concatenate([sin_ref[...]] * B, axis=0)
    q = _head_rms(_dot(n, wq_ref[...]), ind_ref[...], qg_ref[...])
    q = jnp.concatenate([_rope(q[:, blk * LANES:(blk + 1) * LANES], cos, sin)
                         for blk in range(SWA_DIM // LANES)], axis=-1) * (QK_SCALE * LOG2E)
    q_ref[...] = q.astype(BF16).reshape(B, ts, SWA_DIM)
    kv = _dot(n, wkv_ref[...])
    k = _head_rms(kv[:, :LANES], ind_ref[0:LANES, 0:LANES], kg_ref[...])
    kd_ref[...] = _dup_halves(_rope(k, cos, sin)).astype(BF16).reshape(B, ts, 2 * LANES)
    v_ref[...] = kv[:, LANES:].astype(BF16).reshape(B, ts, LANES)


def _odd_inproj(x, g, w_q, w_kv, w_u, qg, kg, cos, sin):
    B, S, D = x.shape
    ts = TIME_CHUNK
    bmaj = lambda w: pl.BlockSpec((B, ts, w), lambda s: (0, s, 0))
    tab = pl.BlockSpec((ts, LANES), lambda s: (s, 0))
    consts = [g, w_q, w_kv, w_u, qg, kg, _head_mean_matrix(SWA_DIM), _time_major_perm(ts)]
    return pl.pallas_call(
        _odd_in_kernel, grid=(S // ts,),
        in_specs=[bmaj(D)] + [_const_spec(a.shape) for a in consts] + [tab, tab],
        out_specs=[bmaj(SWA_DIM), bmaj(2 * LANES), bmaj(LANES),
                   pl.BlockSpec((B * ts, S5_WIDTH), lambda s: (s, 0))],
        out_shape=[jax.ShapeDtypeStruct((B, S, SWA_DIM), BF16),
                   jax.ShapeDtypeStruct((B, S, 2 * LANES), BF16),
                   jax.ShapeDtypeStruct((B, S, LANES), BF16),
                   jax.ShapeDtypeStruct((S * B, S5_WIDTH), F32)],
        compiler_params=_params("parallel"), name="odd_inproj",
    )(x, *consts, cos, sin)


def _swa_kernel(q_ref, kd_ref, vt_ref, sink_ref, o_ref):
    tq = q_ref.shape[0]
    W = SWA_WINDOW
    G = SWA_GROUP
    base = pl.program_id(1) * tq
    key = lax.broadcasted_iota(jnp.int32, (2 * W, G * W), 0)
    qoff = lax.broadcasted_iota(jnp.int32, (2 * W, G * W), 1) & (W - 1)
    cgrp = lax.broadcasted_iota(jnp.int32, (1, G * W), 1) // W
    sinks = sink_ref[...] * LOG2E
    for n in range(tq // W):
        rows = slice(n * W, (n + 1) * W)
        r0 = base + n * W
        kstart = pl.multiple_of(jnp.maximum(r0 - W, 0), W)
        chunk = kstart // W
        diff = (r0 + qoff) - (kstart + key)
        valid = (diff >= 0) & (diff < W)
        for kvh in range(SWA_KV_HEADS):
            parts = []
            for g in range(G):
                head = kvh * G + g
                blk = q_ref[rows, (head // 2) * LANES:(head // 2 + 1) * LANES]
                half = head % 2
                parts.append(_keep_lanes(blk, half * HEAD_DIM, (half + 1) * HEAD_DIM))
            s = _dot_nt(kd_ref[pl.ds(kstart, 2 * W), kvh * LANES:(kvh + 1) * LANES],
                        jnp.concatenate(parts, axis=0))
            s = jnp.where(valid, s, -jnp.inf)
            sink = jnp.zeros((1, G * W), F32)
            for g in range(G):
                head = kvh * G + g
                sink = jnp.where(cgrp == g, sinks[:, head:head + 1], sink)
            m = jnp.maximum(jnp.max(s, axis=0, keepdims=True), sink)
            p = jnp.exp2(s - m).astype(BF16)
            acc = _dot(vt_ref[kvh, chunk], p[0:W, :]) + _dot(vt_ref[kvh, chunk + 1], p[W:2 * W, :])
            den = acc[HEAD_DIM:HEAD_DIM + 1, :] + jnp.exp2(sink - m)
            o = acc[0:HEAD_DIM, :] / den
            for pair in range(G // 2):
                both = jnp.concatenate([o[:, (2 * pair) * W:(2 * pair + 1) * W],
                                        o[:, (2 * pair + 1) * W:(2 * pair + 2) * W]], axis=0)
                lb = kvh * (G // 2) + pair
                o_ref[rows, lb * LANES:(lb + 1) * LANES] = both.T.astype(BF16)


def _swa_attention(q, kd, vt, sinks):
    B, S, _ = q.shape
    tq = min(ATTN_TILE, S)
    return pl.pallas_call(
        _swa_kernel, grid=(B, S // tq),
        in_specs=[pl.BlockSpec((None, tq, SWA_DIM), lambda b, i: (b, i, 0)),
                  pl.BlockSpec((None, S, 2 * LANES), lambda b, i: (b, 0, 0)),
                  pl.BlockSpec((None,) + vt.shape[1:], lambda b, i: (b, 0, 0, 0, 0)),
                  pl.BlockSpec((1, SWA_HEADS), lambda b, i: (0, 0))],
        out_specs=pl.BlockSpec((None, tq, SWA_DIM), lambda b, i: (b, i, 0)),
        out_shape=jax.ShapeDtypeStruct((B, S, SWA_DIM), BF16),
        compiler_params=_params("parallel", "arbitrary"), name="swa_attention",
    )(q, kd, vt, sinks)


def _s5_prep_kernel(lr_ref, li_ref, ldt_ref, br_ref, bi_ref, ar_ref, ai_ref, bbr_ref, bbi_ref):
    lr, li = lr_ref[...], li_ref[...]
    dt = jnp.exp(ldt_ref[...])
    mag = jnp.exp(lr * dt)
    ar = mag * jnp.cos(li * dt)
    ai = mag * jnp.sin(li * dt)
    den = lr * lr + li * li
    cr = ((ar - 1.0) * lr + ai * li) / den
    ci = (ai * lr - (ar - 1.0) * li) / den
    br, bi = br_ref[...], bi_ref[...]
    ar_ref[...] = ar
    ai_ref[...] = ai
    bbr_ref[...] = cr * br - ci * bi
    bbi_ref[...] = cr * bi + ci * br


def _s5_prep(lam_re, lam_im, log_dt, b_re, b_im):
    G, P, C = b_re.shape
    rep = lambda a: jnp.repeat(a, C, axis=0)
    bt = lambda a: a.transpose(0, 2, 1).reshape(G * C, P)
    ldt = jnp.broadcast_to(log_dt[:, None], (G, P))
    full = pl.BlockSpec((G * C, P), lambda: (0, 0))
    outs = pl.pallas_call(
        _s5_prep_kernel, in_specs=[full] * 5, out_specs=[full] * 4,
        out_shape=[jax.ShapeDtypeStruct((G * C, P), F32)] * 4, name="s5_prep",
    )(rep(lam_re), rep(lam_im), rep(ldt), bt(b_re), bt(b_im))
    ar, ai, bbr, bbi = [o.reshape(G, C, P) for o in outs]
    return ar[:, 0], ai[:, 0], bbr, bbi


def _s5_kernel(u_ref, bm_ref, cm_ref, ar_ref, ai_ref, d_ref, gw_ref, gb_ref, o_ref,
               h_ref, carry_ref):
    rows = u_ref.shape[0]
    half = h_ref.shape[2] // 2

    @pl.when(pl.program_id(0) == 0)
    def _():
        carry_ref[...] = jnp.zeros_like(carry_ref)

    u = u_ref[...]
    ub = u.astype(BF16)
    ys = []
    for g in range(S5_LANE_GROUPS):
        h_ref[g] = _dot(ub[:, g * LANES:(g + 1) * LANES], bm_ref[g])
        ar = jnp.broadcast_to(ar_ref[g], (SUBLANES, half))
        ai = jnp.broadcast_to(ai_ref[g], (SUBLANES, half))

        def step(t, carry, g=g, ar=ar, ai=ai):
            hr, hi = carry
            sl = pl.ds(pl.multiple_of(t * SUBLANES, SUBLANES), SUBLANES)
            nr = ar * hr - ai * hi + h_ref[g, sl, 0:half]
            ni = ar * hi + ai * hr + h_ref[g, sl, half:2 * half]
            h_ref[g, sl, 0:half] = nr
            h_ref[g, sl, half:2 * half] = ni
            return nr, ni

        hr, hi = lax.fori_loop(0, rows // SUBLANES, step,
                               (carry_ref[g, :, 0:half], carry_ref[g, :, half:2 * half]), unroll=8)
        carry_ref[g, :, 0:half] = hr
        carry_ref[g, :, half:2 * half] = hi
        ys.append(_dot(h_ref[g].astype(BF16), cm_ref[g]))
    y = jnp.concatenate(ys, axis=-1) + d_ref[...] * u
    z = jax.nn.gelu(y)
    o_ref[...] = (z * jax.nn.sigmoid(_dot(z.astype(BF16), gw_ref[...]) + gb_ref[...])).astype(BF16)


def _s5(u_t, bmat, cmat, ar, ai, d, glu_w, glu_b):
    R, W = u_t.shape
    rows = TIME_CHUNK * SUBLANES
    nstate = bmat.shape[2]
    blk = pl.BlockSpec((rows, W), lambda t: (t, 0))
    return pl.pallas_call(
        _s5_kernel, grid=(R // rows,),
        in_specs=[blk, _const_spec(bmat.shape), _const_spec(cmat.shape), _const_spec(ar.shape),
                  _const_spec(ai.shape), _const_spec((1, W)), _const_spec((W, W)),
                  _const_spec((1, W))],
        out_specs=blk, out_shape=jax.ShapeDtypeStruct((R, W), BF16),
        scratch_shapes=[pltpu.VMEM((S5_LANE_GROUPS, rows, nstate), F32),
                        pltpu.VMEM((S5_LANE_GROUPS, SUBLANES, nstate), F32)],
        compiler_params=_params("arbitrary"), name="s5_glu",
    )(u_t, bmat, cmat, ar, ai, d, glu_w, glu_b)


def _s5_matrices(ar, ai, bbr, bbi, c_re, c_im):
    L, GL = S5_LANE_GROUPS, S5_GROUPS // S5_LANE_GROUPS
    C, P = S5_GROUP, S5_STATE
    eye = jnp.eye(GL, dtype=F32)

    def inmap(b):
        return jnp.einsum("lgcp,gh->lgchp", b.reshape(L, GL, C, P), eye).reshape(L, GL * C, GL * P)

    def outmap(c):
        return jnp.einsum("lgcp,gh->lgphc", c.reshape(L, GL, C, P), eye).reshape(L, GL * P, GL * C)

    bmat = jnp.concatenate([inmap(bbr), inmap(bbi)], axis=2).astype(BF16)
    cmat = jnp.concatenate([outmap(c_re), outmap(-c_im)], axis=1).astype(BF16)
    a_r = ar.reshape(L, 1, GL * P)
    a_i = ai.reshape(L, 1, GL * P)
    return bmat, cmat, a_r, a_i


def _block_diag_pairs(w):
    nb, bs, _ = w.shape
    half = nb // 2
    eye = jnp.eye(half, dtype=w.dtype)
    out = jnp.einsum("thij,hk->thikj", w.reshape(2, half, bs, bs), eye)
    return out.reshape(2, half * bs, half * bs).astype(BF16)


def kernel(x, p, ffn1_norm, ffn1_wg, ffn1_wu, ffn1_wd, mix_norm, ffn2_norm, ffn2_wg, ffn2_wu, ffn2_wd, ple_w, ple_norm, ple_gate_norm, ple_gate_w, ev_w_in, lru_conv_w, lru_conv_b, lru_wa, lru_ba, lru_wx, lru_bx, lru_lambda, fox_bf, fox_q_norm, fox_k_norm, ev_w_out, od_w_in, swa_q_norm, swa_k_norm, swa_sinks, s5_lambda_re, s5_lambda_im, s5_log_dt, s5_b_re, s5_b_im, s5_c_re, s5_c_im, s5_d, s5_glu_w, s5_glu_b, od_w_out):
    B, S, D = x.shape
    depth = p.shape[0]
    assert B == SUBLANES and D == D_MODEL and S % TIME_CHUNK == 0
    T = B * S
    bf = lambda a: a.astype(BF16)
    row = lambda a: a[:, None, :]
    per_head = lambda gain, heads: jnp.tile(gain, heads)[None, :]

    f1 = (row(ffn1_norm), bf(ffn1_wg), bf(ffn1_wu), bf(ffn1_wd))
    f2 = (row(ffn2_norm), bf(ffn2_wg), bf(ffn2_wu), bf(ffn2_wd))
    ple = (p.reshape(depth, T, PLE_DIM), bf(ple_w), row(ple_norm), row(ple_gate_norm),
           bf(ple_gate_w))
    cos, sin = _rope_tables(S)

    for i in range(depth):
        j = i // 2
        x = _ffn(x.reshape(T, D), i, *f1).reshape(B, S, D)
        g = mix_norm[i][None, :]
        if i % 2 == 0:
            w_in = bf(ev_w_in[j])
            o1, o2 = 2 * LRU_WIDTH, 2 * LRU_WIDTH + 3 * FOX_DIM
            w_f = jnp.pad(w_in[:, o2:], ((0, 0), (0, LANES - FOX_HEADS)))
            b_f = jnp.pad(fox_bf[j], (0, LANES - FOX_HEADS))[None, :]
            xa, ya, q, k, qe, ke, v = _even_inproj(
                x, g, w_in[:, :o1], w_in[:, o1:o2], w_f, b_f,
                per_head(fox_q_norm[j], FOX_HEADS), per_head(fox_k_norm[j], FOX_HEADS))
            a_out = _lru(xa, ya, lru_conv_w[j], lru_conv_b[j][None, :],
                         _block_diag_pairs(lru_wa[j]), lru_ba[j][None, :],
                         _block_diag_pairs(lru_wx[j]), lru_bx[j][None, :],
                         lru_lambda[j][None, :])
            b_out = _fox_attention(q, qe, k, ke, _value_transpose(v, ATTN_TILE))
            w_out = bf(ev_w_out[j])
            x = _outproj(x, a_out, b_out, w_out[:LRU_WIDTH], w_out[LRU_WIDTH:])
        else:
            w_in = bf(od_w_in[j])
            kvd = SWA_KV_HEADS * HEAD_DIM
            o1, o2 = SWA_DIM, SWA_DIM + 2 * kvd
            q, kd, v, u = _odd_inproj(
                x, g, w_in[:, :o1], w_in[:, o1:o2], w_in[:, o2:],
                per_head(swa_q_norm[j], SWA_HEADS), per_head(swa_k_norm[j], SWA_KV_HEADS), cos, sin)
            c_out = _swa_attention(q, kd, _value_transpose(v, SWA_WINDOW), swa_sinks[j][None, :])
            ar, ai, bbr, bbi = _s5_prep(s5_lambda_re[j], s5_lambda_im[j], s5_log_dt[j],
                                        s5_b_re[j], s5_b_im[j])
            bmat, cmat, a_r, a_i = _s5_matrices(ar, ai, bbr, bbi, s5_c_re[j], s5_c_im[j])
            d_out = _s5(u, bmat, cmat, a_r, a_i, s5_d[j][None, :],
                        bf(s5_glu_w[j]), s5_glu_b[j][None, :])
            w_out = bf(od_w_out[j])
            x = _outproj(x, d_out, c_out, w_out[SWA_DIM:], w_out[:SWA_DIM])
        x = _ffn(x.reshape(T, D), i, *f2, ple=ple).reshape(B, S, D)
    return x
```

```python
import numpy as np

import jax
import jax.numpy as jnp
from jax import lax
from jax.experimental import pallas as pl
from jax.experimental.pallas import tpu as pltpu

F32 = jnp.float32
BF16 = jnp.bfloat16

D_MODEL = 1024
HEAD_DIM = 64
LRU_WIDTH = 512
LRU_CONV = 4
LRU_C = 8.0
FOX_HEADS = 8
FOX_DIM = 512
SWA_HEADS = 8
SWA_KV_HEADS = 2
SWA_GROUP = SWA_HEADS // SWA_KV_HEADS
SWA_DIM = 512
SWA_WINDOW = 128
S5_WIDTH = 512
S5_GROUP = 16
S5_GROUPS = 32
S5_STATE = 64
D_FF = 2816
PLE_DIM = 256
ROPE_THETA = 10000.0
EPS = 1e-6
MACARON = 0.5
QK_SCALE = HEAD_DIM ** -0.5
LOG2E = 1.4426950408889634

SUBLANES = 8
LANES = 128
TIME_CHUNK = 64
ROW_TILE = SUBLANES * TIME_CHUNK
ATTN_TILE = 512
S5_LANE_GROUPS = 4
S5_CHUNKS = 2
BIAS_LANES = 6
VT_ROWS = 80
VMEM_LIMIT = 56 * 1024 * 1024


def _dot(a, b):
    return jnp.dot(a, b, preferred_element_type=F32)


def _dot_nt(a, b):
    return lax.dot_general(a, b, (((1,), (1,)), ((), ())), preferred_element_type=F32)


def _rms(x, g):
    ms = jnp.mean(x * x, axis=-1, keepdims=True)
    return x * lax.rsqrt(ms + EPS) * g


def _head_rms(x, ind, gain):
    sq = x * x
    hi = sq.astype(BF16)
    lo = (sq - hi.astype(F32)).astype(BF16)
    ms = _dot(hi, ind) + _dot(lo, ind)
    return x * lax.rsqrt(ms + EPS) * gain


def _softplus(x):
    return jnp.maximum(x, 0.0) + jnp.log1p(jnp.exp(-jnp.abs(x)))


def _log_sigmoid(x):
    return -_softplus(-x)


def _cumsum_time(x):
    n = x.shape[0]
    row = lax.broadcasted_iota(jnp.int32, x.shape, 0)
    d = SUBLANES
    while d < n:
        x = x + jnp.where(row >= d, pltpu.roll(x, d, axis=0), 0.0)
        d *= 2
    return x


def _time_major_perm(steps):
    r = np.arange(SUBLANES * steps)
    src = (r % SUBLANES) * steps + r // SUBLANES
    return jnp.asarray(src[:, None] == r[None, :], dtype=BF16)


def _permute_rows(perm, x):
    return _dot(perm, x).astype(BF16)


def _split3_bf16(c):
    hi = c.astype(BF16).astype(F32)
    r = c - hi
    mid = r.astype(BF16).astype(F32)
    return hi, mid, r - mid


def _keep_lanes(x, lo, hi):
    lane = lax.broadcasted_iota(jnp.int32, x.shape, 1)
    return jnp.where((lane >= lo) & (lane < hi), x.astype(F32), 0.0).astype(BF16)


def _params(*sem):
    return pltpu.CompilerParams(dimension_semantics=sem, vmem_limit_bytes=VMEM_LIMIT)


def _const_spec(shape):
    nd = len(shape)
    return pl.BlockSpec(shape, lambda *_: (0,) * nd, pipeline_mode=pl.Buffered(1))


def _layer_spec(shape, layer):
    nd = len(shape)
    return pl.BlockSpec((None,) + tuple(shape), lambda *_: (layer,) + (0,) * nd,
                        pipeline_mode=pl.Buffered(1))


def _head_mean_matrix(width):
    h = np.arange(width) // HEAD_DIM
    return jnp.asarray((h[:, None] == h[None, :]) / HEAD_DIM, dtype=BF16)


def _swiglu_update(x, g_ref, wg_ref, wu_ref, wd_ref):
    n = _rms(x, g_ref[...]).astype(BF16)
    hg = _dot(n, wg_ref[...])
    hu = _dot(n, wu_ref[...])
    act = (hg * jax.nn.sigmoid(hg) * hu).astype(BF16)
    return x + MACARON * _dot(act, wd_ref[...])


def _ffn_kernel(x_ref, g_ref, wg_ref, wu_ref, wd_ref, o_ref):
    o_ref[...] = _swiglu_update(x_ref[...], g_ref, wg_ref, wu_ref, wd_ref)


def _ffn(x2d, layer, norm, wg, wu, wd):
    T, D = x2d.shape
    tm = min(ROW_TILE, T)
    row = pl.BlockSpec((tm, D), lambda i: (i, 0))
    return pl.pallas_call(
        _ffn_kernel, grid=(T // tm,),
        in_specs=[row, _layer_spec((1, D), layer), _layer_spec((D, D_FF), layer),
                  _layer_spec((D, D_FF), layer), _layer_spec((D_FF, D), layer)],
        out_specs=row, out_shape=jax.ShapeDtypeStruct((T, D), F32),
        compiler_params=_params("parallel"), name="ffn",
    )(x2d, norm, wg, wu, wd)


def _mix_ffn_ple_kernel(x_ref, tmaj_ref, bmaj_ref, wt_ref, wb_ref, unperm_ref,
                        g_ref, wg_ref, wu_ref, wd_ref, p_ref, pw_ref, pn_ref, gn_ref, gw_ref, o_ref):
    B, ts, D = x_ref.shape
    rows = B * ts
    part_t = _permute_rows(unperm_ref[...], tmaj_ref[...])
    part_b = bmaj_ref[...].reshape(rows, bmaj_ref.shape[2])
    x = x_ref[...].reshape(rows, D) + _dot(part_t, wt_ref[...]) + _dot(part_b, wb_ref[...])
    x = _swiglu_update(x, g_ref, wg_ref, wu_ref, wd_ref)
    e = _rms(_dot(p_ref[...].reshape(rows, PLE_DIM).astype(BF16), pw_ref[...]), pn_ref[...])
    gate = jax.nn.sigmoid(_dot(_rms(x, gn_ref[...]).astype(BF16), gw_ref[...]))
    o_ref[...] = (x + gate * e).reshape(B, ts, D)


def _mix_ffn_ple(x, part_tmaj, part_bmaj, w_tmaj, w_bmaj, layer, ffn, ple):
    B, S, D = x.shape
    ts = TIME_CHUNK
    W = part_tmaj.shape[1]
    unperm = _time_major_perm(ts).T
    xs = pl.BlockSpec((B, ts, D), lambda s: (0, s, 0))
    in_specs = [xs, pl.BlockSpec((B * ts, W), lambda s: (s, 0)),
                pl.BlockSpec((B, ts, W), lambda s: (0, s, 0)),
                _const_spec((W, D)), _const_spec((W, D)), _const_spec(unperm.shape),
                _layer_spec((1, D), layer), _layer_spec((D, D_FF), layer),
                _layer_spec((D, D_FF), layer), _layer_spec((D_FF, D), layer),
                pl.BlockSpec((None, B, ts, PLE_DIM), lambda s: (layer, 0, s, 0)),
                _layer_spec((PLE_DIM, D), layer), _layer_spec((1, D), layer),
                _layer_spec((1, D), layer), _layer_spec((D, D), layer)]
    return pl.pallas_call(
        _mix_ffn_ple_kernel, grid=(S // ts,), in_specs=in_specs, out_specs=xs,
        out_shape=jax.ShapeDtypeStruct((B, S, D), F32),
        compiler_params=_params("parallel"), name="mix_ffn_ple",
    )(x, part_tmaj, part_bmaj, w_tmaj, w_bmaj, unperm, *ffn, *ple)


def _bias_lane_maps():
    eq = np.zeros((3 * LANES, LANES), np.float32)
    ek = np.zeros((3 * LANES, LANES), np.float32)
    oq = np.zeros((1, LANES), np.float32)
    ok = np.zeros((1, LANES), np.float32)
    for h in range(FOX_HEADS):
        for i in range(3):
            eq[i * LANES + h, BIAS_LANES * h + i] = 1.0
            ek[i * LANES + h, BIAS_LANES * h + 3 + i] = -1.0
            oq[0, BIAS_LANES * h + 3 + i] = 1.0
            ok[0, BIAS_LANES * h + i] = 1.0
    return jnp.asarray(eq, BF16), jnp.asarray(ek, BF16), jnp.asarray(oq), jnp.asarray(ok)


def _even_in_kernel(x_ref, g_ref, wxy_ref, wqkv_ref, wf_ref, bf_ref, qg_ref, kg_ref, ind_ref,
                    eq_ref, ek_ref, oq_ref, ok_ref, perm_ref, unperm_ref,
                    xa_ref, ya_ref, q_ref, k_ref, qe_ref, ke_ref, v_ref, carry_ref):
    B, ts, D = x_ref.shape
    rows = B * ts

    @pl.when(pl.program_id(0) == 0)
    def _():
        carry_ref[...] = jnp.zeros_like(carry_ref)

    n = _rms(x_ref[...].reshape(rows, D), g_ref[...]).astype(BF16)
    n_t = _permute_rows(perm_ref[...], n)
    xy = _dot(n_t, wxy_ref[...])
    xa_ref[...] = xy[:, :LRU_WIDTH]
    ya_ref[...] = xy[:, LRU_WIDTH:]

    logf = _log_sigmoid(_dot(n_t, wf_ref[...]) + bf_ref[...]) * LOG2E
    c = _cumsum_time(logf) + pltpu.repeat(carry_ref[...], ts, axis=0)
    carry_ref[...] = c[rows - SUBLANES:rows, :]
    pieces = jnp.concatenate(_split3_bf16(c), axis=-1).astype(BF16)
    unperm = unperm_ref[...]
    qe = (_dot(pieces, eq_ref[...]) + oq_ref[...]).astype(BF16)
    ke = (_dot(pieces, ek_ref[...]) + ok_ref[...]).astype(BF16)
    qe_ref[...] = _permute_rows(unperm, qe).reshape(B, ts, LANES)
    ke_ref[...] = _permute_rows(unperm, ke).reshape(B, ts, LANES)

    qkv = _dot(n, wqkv_ref[...])
    ind = ind_ref[...]
    q = _head_rms(qkv[:, :FOX_DIM], ind, qg_ref[...]) * (QK_SCALE * LOG2E)
    k = _head_rms(qkv[:, FOX_DIM:2 * FOX_DIM], ind, kg_ref[...])
    q_ref[...] = q.astype(BF16).reshape(B, ts, FOX_DIM)
    k_ref[...] = k.astype(BF16).reshape(B, ts, FOX_DIM)
    v_ref[...] = qkv[:, 2 * FOX_DIM:].astype(BF16).reshape(B, ts, FOX_DIM)


def _even_inproj(x, g, w_xy, w_qkv, w_f, b_f, qg, kg):
    B, S, D = x.shape
    ts = TIME_CHUNK
    eq, ek, oq, ok = _bias_lane_maps()
    perm = _time_major_perm(ts)
    tmaj = pl.BlockSpec((B * ts, LRU_WIDTH), lambda s: (s, 0))
    bmaj = lambda w: pl.BlockSpec((B, ts, w), lambda s: (0, s, 0))
    consts = [g, w_xy, w_qkv, w_f, b_f, qg, kg, _head_mean_matrix(FOX_DIM), eq, ek, oq, ok,
              perm, perm.T]
    return pl.pallas_call(
        _even_in_kernel, grid=(S // ts,),
        in_specs=[bmaj(D)] + [_const_spec(a.shape) for a in consts],
        out_specs=[tmaj, tmaj, bmaj(FOX_DIM), bmaj(FOX_DIM), bmaj(LANES), bmaj(LANES),
                   bmaj(FOX_DIM)],
        out_shape=[jax.ShapeDtypeStruct((S * B, LRU_WIDTH), F32),
                   jax.ShapeDtypeStruct((S * B, LRU_WIDTH), F32),
                   jax.ShapeDtypeStruct((B, S, FOX_DIM), BF16),
                   jax.ShapeDtypeStruct((B, S, FOX_DIM), BF16),
                   jax.ShapeDtypeStruct((B, S, LANES), BF16),
                   jax.ShapeDtypeStruct((B, S, LANES), BF16),
                   jax.ShapeDtypeStruct((B, S, FOX_DIM), BF16)],
        scratch_shapes=[pltpu.VMEM((SUBLANES, LANES), F32)],
        compiler_params=_params("arbitrary"), name="even_inproj",
    )(x, *consts)


def _vt_kernel(v_ref, o_ref):
    heads, nchunk, _, chunk = o_ref.shape
    vt = v_ref[...].astype(F32).T
    ones = jnp.ones((VT_ROWS - HEAD_DIM, chunk), BF16)
    for h in range(heads):
        for c in range(nchunk):
            o_ref[h, c, 0:HEAD_DIM, :] = vt[h * HEAD_DIM:(h + 1) * HEAD_DIM,
                                            c * chunk:(c + 1) * chunk].astype(BF16)
            o_ref[h, c, HEAD_DIM:VT_ROWS, :] = ones


def _value_transpose(v, chunk):
    B, S, W = v.shape
    heads = W // HEAD_DIM
    tk = min(ATTN_TILE, S)
    chunk = min(chunk, tk)
    return pl.pallas_call(
        _vt_kernel, grid=(B, S // tk),
        in_specs=[pl.BlockSpec((None, tk, W), lambda b, j: (b, j, 0))],
        out_specs=pl.BlockSpec((None, heads, tk // chunk, VT_ROWS, chunk),
                               lambda b, j: (b, 0, j, 0, 0)),
        out_shape=jax.ShapeDtypeStruct((B, heads, S // chunk, VT_ROWS, chunk), BF16),
        compiler_params=_params("parallel", "parallel"), name="value_transpose",
    )(v)


def _lru_kernel(xa_ref, ya_ref, cw_ref, cb_ref, wa_ref, ba_ref, wx_ref, bx_ref, lam_ref, o_ref,
                xpad_ref, a_ref, h_ref, carry_ref):
    rows = xa_ref.shape[0]
    halo = (LRU_CONV - 1) * SUBLANES
    half = LRU_WIDTH // 2

    @pl.when(pl.program_id(0) == 0)
    def _():
        xpad_ref[0:halo, :] = jnp.zeros((halo, LRU_WIDTH), F32)
        carry_ref[...] = jnp.zeros_like(carry_ref)

    xpad_ref[halo:halo + rows, :] = xa_ref[...]
    xc = cb_ref[...]
    for tap in range(LRU_CONV):
        xc = xc + xpad_ref[tap * SUBLANES:tap * SUBLANES + rows, :] * cw_ref[tap:tap + 1, :]
    xpad_ref[0:halo, :] = xpad_ref[rows:rows + halo, :]

    xb = xc.astype(BF16)

    def gate(w_ref, b_ref):
        z = jnp.concatenate([_dot(xb[:, :half], w_ref[0]), _dot(xb[:, half:], w_ref[1])], axis=-1)
        return jax.nn.sigmoid(z + b_ref[...])

    r = gate(wa_ref, ba_ref)
    i = gate(wx_ref, bx_ref)
    log_a = -LRU_C * r * _softplus(lam_ref[...])
    a_ref[...] = jnp.exp(log_a)
    th = jnp.tanh(log_a)
    h_ref[...] = jnp.sqrt(-2.0 * th / (1.0 - th)) * (i * xc)

    def step(t, h):
        sl = pl.ds(pl.multiple_of(t * SUBLANES, SUBLANES), SUBLANES)
        h = a_ref[sl, :] * h + h_ref[sl, :]
        h_ref[sl, :] = h
        return h

    carry_ref[...] = lax.fori_loop(0, rows // SUBLANES, step, carry_ref[...], unroll=8)
    o_ref[...] = (jax.nn.gelu(ya_ref[...]) * h_ref[...]).astype(BF16)


def _lru(xa_t, ya_t, conv_w, conv_b, wa_bd, ba, wx_bd, bx, lam):
    R, W = xa_t.shape
    rows = TIME_CHUNK * SUBLANES
    halo = (LRU_CONV - 1) * SUBLANES
    blk = pl.BlockSpec((rows, W), lambda t: (t, 0))
    return pl.pallas_call(
        _lru_kernel, grid=(R // rows,),
        in_specs=[blk, blk, _const_spec((LRU_CONV, W)), _const_spec((1, W)),
                  _const_spec(wa_bd.shape), _const_spec((1, W)),
                  _const_spec(wx_bd.shape), _const_spec((1, W)), _const_spec((1, W))],
        out_specs=blk, out_shape=jax.ShapeDtypeStruct((R, W), BF16),
        scratch_shapes=[pltpu.VMEM((halo + rows, W), F32), pltpu.VMEM((rows, W), F32),
                        pltpu.VMEM((rows, W), F32), pltpu.VMEM((SUBLANES, W), F32)],
        compiler_params=_params("arbitrary"), name="rg_lru",
    )(xa_t, ya_t, conv_w, conv_b, wa_bd, ba, wx_bd, bx, lam)


def _fox_kernel(q_ref, qe_ref, k_ref, ke_ref, vt_ref, o_ref, acc_ref, s_ref):
    tq = q_ref.shape[0]
    qi = pl.program_id(1)
    qe = qe_ref[...]
    qq = []
    for h in range(FOX_HEADS):
        half = h % 2
        blk = q_ref[:, (h // 2) * LANES:(h // 2 + 1) * LANES]
        qq.append(jnp.concatenate(
            [_keep_lanes(blk, half * HEAD_DIM, (half + 1) * HEAD_DIM),
             _keep_lanes(qe, BIAS_LANES * h, BIAS_LANES * (h + 1))], axis=-1))
    key = lax.broadcasted_iota(jnp.int32, (tq, tq), 0)
    qry = lax.broadcasted_iota(jnp.int32, (tq, tq), 1)
    acc_ref[...] = jnp.zeros_like(acc_ref)

    def update(j, m_run, masked):
        ks = pl.ds(pl.multiple_of(j * tq, tq), tq)
        ke = ke_ref[ks, :]
        for h in range(FOX_HEADS):
            kk = jnp.concatenate([k_ref[ks, (h // 2) * LANES:(h // 2 + 1) * LANES], ke], axis=-1)
            s = _dot_nt(kk, qq[h])
            if masked:
                s = jnp.where(key <= qry, s, -jnp.inf)
            s_ref[h] = s
        new = []
        for h in range(FOX_HEADS):
            m_new = jnp.maximum(m_run[h], jnp.max(s_ref[h], axis=0, keepdims=True))
            alpha = jnp.exp2(m_run[h] - m_new)
            p = jnp.exp2(s_ref[h] - m_new).astype(BF16)
            acc_ref[h] = alpha * acc_ref[h] + _dot(vt_ref[h, j], p)
            new.append(m_new)
        return tuple(new)

    init = (jnp.full((1, tq), -jnp.inf, F32),) * FOX_HEADS
    m_run = lax.fori_loop(0, qi, lambda j, m: update(j, m, False), init)
    update(qi, m_run, True)
    for pair in range(FOX_HEADS // 2):
        out = [acc_ref[h, 0:HEAD_DIM, :] / acc_ref[h, HEAD_DIM:HEAD_DIM + 1, :]
               for h in (2 * pair, 2 * pair + 1)]
        o_ref[:, pair * LANES:(pair + 1) * LANES] = jnp.concatenate(out, axis=0).T.astype(BF16)


def _fox_attention(q, qe, k, ke, vt):
    B, S, _ = q.shape
    nk, tq = vt.shape[2], vt.shape[4]
    qblk = lambda w: pl.BlockSpec((None, tq, w), lambda b, i: (b, i, 0))
    kblk = lambda w: pl.BlockSpec((None, S, w), lambda b, i: (b, 0, 0))
    return pl.pallas_call(
        _fox_kernel, grid=(B, S // tq),
        in_specs=[qblk(FOX_DIM), qblk(LANES), kblk(FOX_DIM), kblk(LANES),
                  pl.BlockSpec((None, FOX_HEADS, nk, VT_ROWS, tq), lambda b, i: (b, 0, 0, 0, 0))],
        out_specs=qblk(FOX_DIM),
        out_shape=jax.ShapeDtypeStruct((B, S, FOX_DIM), BF16),
        scratch_shapes=[pltpu.VMEM((FOX_HEADS, VT_ROWS, tq), F32),
                        pltpu.VMEM((FOX_HEADS, tq, tq), F32)],
        compiler_params=_params("parallel", "arbitrary"), name="fox_attention",
    )(q, qe, k, ke, vt)


def _rope_table_kernel(inv_ref, cos_ref, sin_ref):
    rows = cos_ref.shape[0]
    pos = pl.program_id(0) * rows + lax.broadcasted_iota(jnp.int32, cos_ref.shape, 0)
    lane = lax.broadcasted_iota(jnp.int32, cos_ref.shape, 1)
    ang = pos.astype(F32) * inv_ref[...]
    cos_ref[...] = jnp.cos(ang)
    sin_ref[...] = jnp.where((lane & (HEAD_DIM - 1)) < HEAD_DIM // 2, -1.0, 1.0) * jnp.sin(ang)


def _rope_tables(S):
    half = HEAD_DIM // 2
    inv = jnp.power(ROPE_THETA, -jnp.arange(half, dtype=F32) / half)
    inv4 = jnp.tile(inv, LANES // half)[None, :]
    rows = min(ROW_TILE, S)
    blk = pl.BlockSpec((rows, LANES), lambda i: (i, 0))
    return pl.pallas_call(
        _rope_table_kernel, grid=(S // rows,),
        in_specs=[pl.BlockSpec((1, LANES), lambda i: (0, 0))], out_specs=[blk, blk],
        out_shape=[jax.ShapeDtypeStruct((S, LANES), F32)] * 2,
        compiler_params=_params("parallel"), name="rope_tables",
    )(inv4)


def _rope(x, cos, sin):
    half = HEAD_DIM // 2
    lane = lax.broadcasted_iota(jnp.int32, x.shape, 1)
    first = (lane & (HEAD_DIM - 1)) < half
    swapped = jnp.where(first, pltpu.roll(x, LANES - half, axis=1), pltpu.roll(x, half, axis=1))
    return x * cos + swapped * sin


def _dup_halves(x):
    lane = lax.broadcasted_iota(jnp.int32, x.shape, 1)
    r = pltpu.roll(x, HEAD_DIM, axis=1)
    lo = lane < HEAD_DIM
    return jnp.concatenate([jnp.where(lo, x, r), jnp.where(lo, r, x)], axis=-1)


def _odd_in_kernel(x_ref, g_ref, wq_ref, wkv_ref, wu_ref, qg_ref, kg_ref, ind_ref, perm_ref,
                   cos_ref, sin_ref, q_ref, kd_ref, v_ref, u_ref):
    B, ts, D = x_ref.shape
    rows = B * ts
    n = _rms(x_ref[...].reshape(rows, D), g_ref[...]).astype(BF16)
    u_ref[...] = _dot(_permute_rows(perm_ref[...], n), wu_ref[...])
    cos = jnp.concatenate([cos_ref[...]] * B, axis=0)
    sin = jnp.concatenate([sin_ref[...]] * B, axis=0)
    q = _head_rms(_dot(n, wq_ref[...]), ind_ref[...], qg_ref[...])
    q = jnp.concatenate([_rope(q[:, blk * LANES:(blk + 1) * LANES], cos, sin)
                         for blk in range(SWA_DIM // LANES)], axis=-1) * (QK_SCALE * LOG2E)
    q_ref[...] = q.astype(BF16).reshape(B, ts, SWA_DIM)
    kv = _dot(n, wkv_ref[...])
    k = _head_rms(kv[:, :LANES], ind_ref[0:LANES, 0:LANES], kg_ref[...])
    kd_ref[...] = _dup_halves(_rope(k, cos, sin)).astype(BF16).reshape(B, ts, 2 * LANES)
    v_ref[...] = kv[:, LANES:].astype(BF16).reshape(B, ts, LANES)


def _odd_inproj(x, g, w_q, w_kv, w_u, qg, kg, cos, sin):
    B, S, D = x.shape
    ts = TIME_CHUNK
    bmaj = lambda w: pl.BlockSpec((B, ts, w), lambda s: (0, s, 0))
    tab = pl.BlockSpec((ts, LANES), lambda s: (s, 0))
    consts = [g, w_q, w_kv, w_u, qg, kg, _head_mean_matrix(SWA_DIM), _time_major_perm(ts)]
    return pl.pallas_call(
        _odd_in_kernel, grid=(S // ts,),
        in_specs=[bmaj(D)] + [_const_spec(a.shape) for a in consts] + [tab, tab],
        out_specs=[bmaj(SWA_DIM), bmaj(2 * LANES), bmaj(LANES),
                   pl.BlockSpec((B * ts, S5_WIDTH), lambda s: (s, 0))],
        out_shape=[jax.ShapeDtypeStruct((B, S, SWA_DIM), BF16),
                   jax.ShapeDtypeStruct((B, S, 2 * LANES), BF16),
                   jax.ShapeDtypeStruct((B, S, LANES), BF16),
                   jax.ShapeDtypeStruct((S * B, S5_WIDTH), F32)],
        compiler_params=_params("parallel"), name="odd_inproj",
    )(x, *consts, cos, sin)


def _swa_kernel(q_ref, kd_ref, vt_ref, sink_ref, o_ref):
    tq = q_ref.shape[0]
    W = SWA_WINDOW
    G = SWA_GROUP
    base = pl.program_id(1) * tq
    key = lax.broadcasted_iota(jnp.int32, (2 * W, G * W), 0)
    qoff = lax.broadcasted_iota(jnp.int32, (2 * W, G * W), 1) & (W - 1)
    cgrp = lax.broadcasted_iota(jnp.int32, (1, G * W), 1) // W
    sinks = sink_ref[...] * LOG2E
    for n in range(tq // W):
        rows = slice(n * W, (n + 1) * W)
        r0 = base + n * W
        kstart = pl.multiple_of(jnp.maximum(r0 - W, 0), W)
        chunk = kstart // W
        diff = (r0 + qoff) - (kstart + key)
        valid = (diff >= 0) & (diff < W)
        for kvh in range(SWA_KV_HEADS):
            parts = []
            for g in range(G):
                head = kvh * G + g
                blk = q_ref[rows, (head // 2) * LANES:(head // 2 + 1) * LANES]
                half = head % 2
                parts.append(_keep_lanes(blk, half * HEAD_DIM, (half + 1) * HEAD_DIM))
            s = _dot_nt(kd_ref[pl.ds(kstart, 2 * W), kvh * LANES:(kvh + 1) * LANES],
                        jnp.concatenate(parts, axis=0))
            s = jnp.where(valid, s, -jnp.inf)
            sink = jnp.zeros((1, G * W), F32)
            for g in range(G):
                head = kvh * G + g
                sink = jnp.where(cgrp == g, sinks[:, head:head + 1], sink)
            m = jnp.maximum(jnp.max(s, axis=0, keepdims=True), sink)
            p = jnp.exp2(s - m).astype(BF16)
            acc = _dot(vt_ref[kvh, chunk], p[0:W, :]) + _dot(vt_ref[kvh, chunk + 1], p[W:2 * W, :])
            den = acc[HEAD_DIM:HEAD_DIM + 1, :] + jnp.exp2(sink - m)
            o = acc[0:HEAD_DIM, :] / den
            for pair in range(G // 2):
                both = jnp.concatenate([o[:, (2 * pair) * W:(2 * pair + 1) * W],
                                        o[:, (2 * pair + 1) * W:(2 * pair + 2) * W]], axis=0)
                lb = kvh * (G // 2) + pair
                o_ref[rows, lb * LANES:(lb + 1) * LANES] = both.T.astype(BF16)


def _swa_attention(q, kd, vt, sinks):
    B, S, _ = q.shape
    tq = min(ATTN_TILE, S)
    return pl.pallas_call(
        _swa_kernel, grid=(B, S // tq),
        in_specs=[pl.BlockSpec((None, tq, SWA_DIM), lambda b, i: (b, i, 0)),
                  pl.BlockSpec((None, S, 2 * LANES), lambda b, i: (b, 0, 0)),
                  pl.BlockSpec((None,) + vt.shape[1:], lambda b, i: (b, 0, 0, 0, 0)),
                  pl.BlockSpec((1, SWA_HEADS), lambda b, i: (0, 0))],
        out_specs=pl.BlockSpec((None, tq, SWA_DIM), lambda b, i: (b, i, 0)),
        out_shape=jax.ShapeDtypeStruct((B, S, SWA_DIM), BF16),
        compiler_params=_params("parallel", "arbitrary"), name="swa_attention",
    )(q, kd, vt, sinks)


def _s5_prep_kernel(lr_ref, li_ref, ldt_ref, br_ref, bi_ref, ar_ref, ai_ref, bbr_ref, bbi_ref):
    lr, li = lr_ref[...], li_ref[...]
    dt = jnp.exp(ldt_ref[...])
    mag = jnp.exp(lr * dt)
    ar = mag * jnp.cos(li * dt)
    ai = mag * jnp.sin(li * dt)
    den = lr * lr + li * li
    cr = ((ar - 1.0) * lr + ai * li) / den
    ci = (ai * lr - (ar - 1.0) * li) / den
    br, bi = br_ref[...], bi_ref[...]
    ar_ref[...] = ar
    ai_ref[...] = ai
    bbr_ref[...] = cr * br - ci * bi
    bbi_ref[...] = cr * bi + ci * br


def _s5_prep(lam_re, lam_im, log_dt, b_re, b_im):
    G, P, C = b_re.shape
    rep = lambda a: jnp.repeat(a, C, axis=0)
    bt = lambda a: a.transpose(0, 2, 1).reshape(G * C, P)
    ldt = jnp.broadcast_to(log_dt[:, None], (G, P))
    full = pl.BlockSpec((G * C, P), lambda: (0, 0))
    outs = pl.pallas_call(
        _s5_prep_kernel, in_specs=[full] * 5, out_specs=[full] * 4,
        out_shape=[jax.ShapeDtypeStruct((G * C, P), F32)] * 4, name="s5_prep",
    )(rep(lam_re), rep(lam_im), rep(ldt), bt(b_re), bt(b_im))
    ar, ai, bbr, bbi = [o.reshape(G, C, P) for o in outs]
    return ar[:, 0], ai[:, 0], bbr, bbi


def _s5_kernel(u_ref, bm_ref, cm_ref, ar_ref, ai_ref, d_ref, gw_ref, gb_ref, o_ref,
               h_ref, carry_ref):
    rows = u_ref.shape[0] // S5_CHUNKS
    steps = rows // SUBLANES
    half = h_ref.shape[3] // 2

    @pl.when(pl.program_id(0) == 0)
    def _():
        carry_ref[...] = jnp.zeros_like(carry_ref)

    def chunk_rows(c):
        return slice(c * rows, (c + 1) * rows)

    def input_map(c):
        ub = u_ref[chunk_rows(c), :].astype(BF16)
        for g in range(S5_LANE_GROUPS):
            h_ref[c, g] = _dot(ub[:, g * LANES:(g + 1) * LANES], bm_ref[g])

    def scan(c):
        for g in range(S5_LANE_GROUPS):
            ar = jnp.broadcast_to(ar_ref[g], (SUBLANES, half))
            ai = jnp.broadcast_to(ai_ref[g], (SUBLANES, half))
            hr = carry_ref[g, :, 0:half]
            hi = carry_ref[g, :, half:2 * half]
            for t in range(steps):
                sl = slice(t * SUBLANES, (t + 1) * SUBLANES)
                nr = ar * hr - ai * hi + h_ref[c, g, sl, 0:half]
                ni = ar * hi + ai * hr + h_ref[c, g, sl, half:2 * half]
                h_ref[c, g, sl, 0:half] = nr
                h_ref[c, g, sl, half:2 * half] = ni
                hr, hi = nr, ni
            carry_ref[g, :, 0:half] = hr
            carry_ref[g, :, half:2 * half] = hi

    def output_map(c):
        u = u_ref[chunk_rows(c), :]
        ys = [_dot(h_ref[c, g].astype(BF16), cm_ref[g]) for g in range(S5_LANE_GROUPS)]
        y = jnp.concatenate(ys, axis=-1) + d_ref[...] * u
        z = jax.nn.gelu(y)
        gate = jax.nn.sigmoid(_dot(z.astype(BF16), gw_ref[...]) + gb_ref[...])
        o_ref[chunk_rows(c), :] = (z * gate).astype(BF16)

    for c in range(S5_CHUNKS):
        input_map(c)
    for c in range(S5_CHUNKS):
        scan(c)
        output_map(c)


def _s5(u_t, bmat, cmat, ar, ai, d, glu_w, glu_b):
    R, W = u_t.shape
    rows = S5_CHUNKS * TIME_CHUNK * SUBLANES
    nstate = bmat.shape[2]
    blk = pl.BlockSpec((rows, W), lambda t: (t, 0))
    return pl.pallas_call(
        _s5_kernel, grid=(R // rows,),
        in_specs=[blk, _const_spec(bmat.shape), _const_spec(cmat.shape), _const_spec(ar.shape),
                  _const_spec(ai.shape), _const_spec((1, W)), _const_spec((W, W)),
                  _const_spec((1, W))],
        out_specs=blk, out_shape=jax.ShapeDtypeStruct((R, W), BF16),
        scratch_shapes=[pltpu.VMEM((S5_CHUNKS, S5_LANE_GROUPS, rows // S5_CHUNKS, nstate), F32),
                        pltpu.VMEM((S5_LANE_GROUPS, SUBLANES, nstate), F32)],
        compiler_params=_params("arbitrary"), name="s5_glu",
    )(u_t, bmat, cmat, ar, ai, d, glu_w, glu_b)


def _s5_matrices(ar, ai, bbr, bbi, c_re, c_im):
    L, GL = S5_LANE_GROUPS, S5_GROUPS // S5_LANE_GROUPS
    C, P = S5_GROUP, S5_STATE
    eye = jnp.eye(GL, dtype=F32)

    def inmap(b):
        return jnp.einsum("lgcp,gh->lgchp", b.reshape(L, GL, C, P), eye).reshape(L, GL * C, GL * P)

    def outmap(c):
        return jnp.einsum("lgcp,gh->lgphc", c.reshape(L, GL, C, P), eye).reshape(L, GL * P, GL * C)

    bmat = jnp.concatenate([inmap(bbr), inmap(bbi)], axis=2).astype(BF16)
    cmat = jnp.concatenate([outmap(c_re), outmap(-c_im)], axis=1).astype(BF16)
    a_r = ar.reshape(L, 1, GL * P)
    a_i = ai.reshape(L, 1, GL * P)
    return bmat, cmat, a_r, a_i


def _block_diag_pairs(w):
    nb, bs, _ = w.shape
    half = nb // 2
    eye = jnp.eye(half, dtype=w.dtype)
    out = jnp.einsum("thij,hk->thikj", w.reshape(2, half, bs, bs), eye)
    return out.reshape(2, half * bs, half * bs).astype(BF16)


def kernel(x, p, ffn1_norm, ffn1_wg, ffn1_wu, ffn1_wd, mix_norm, ffn2_norm, ffn2_wg, ffn2_wu, ffn2_wd, ple_w, ple_norm, ple_gate_norm, ple_gate_w, ev_w_in, lru_conv_w, lru_conv_b, lru_wa, lru_ba, lru_wx, lru_bx, lru_lambda, fox_bf, fox_q_norm, fox_k_norm, ev_w_out, od_w_in, swa_q_norm, swa_k_norm, swa_sinks, s5_lambda_re, s5_lambda_im, s5_log_dt, s5_b_re, s5_b_im, s5_c_re, s5_c_im, s5_d, s5_glu_w, s5_glu_b, od_w_out):
    B, S, D = x.shape
    depth = p.shape[0]
    assert B == SUBLANES and D == D_MODEL and S % (S5_CHUNKS * TIME_CHUNK) == 0
    T = B * S
    bf = lambda a: a.astype(BF16)
    row = lambda a: a[:, None, :]
    per_head = lambda gain, heads: jnp.tile(gain, heads)[None, :]

    f1 = (row(ffn1_norm), bf(ffn1_wg), bf(ffn1_wu), bf(ffn1_wd))
    f2 = (row(ffn2_norm), bf(ffn2_wg), bf(ffn2_wu), bf(ffn2_wd))
    ple = (p, bf(ple_w), row(ple_norm), row(ple_gate_norm), bf(ple_gate_w))
    cos, sin = _rope_tables(S)

    for i in range(depth):
        j = i // 2
        x = _ffn(x.reshape(T, D), i, *f1).reshape(B, S, D)
        g = mix_norm[i][None, :]
        if i % 2 == 0:
            w_in = bf(ev_w_in[j])
            o1, o2 = 2 * LRU_WIDTH, 2 * LRU_WIDTH + 3 * FOX_DIM
            w_f = jnp.pad(w_in[:, o2:], ((0, 0), (0, LANES - FOX_HEADS)))
            b_f = jnp.pad(fox_bf[j], (0, LANES - FOX_HEADS))[None, :]
            xa, ya, q, k, qe, ke, v = _even_inproj(
                x, g, w_in[:, :o1], w_in[:, o1:o2], w_f, b_f,
                per_head(fox_q_norm[j], FOX_HEADS), per_head(fox_k_norm[j], FOX_HEADS))
            a_out = _lru(xa, ya, lru_conv_w[j], lru_conv_b[j][None, :],
                         _block_diag_pairs(lru_wa[j]), lru_ba[j][None, :],
                         _block_diag_pairs(lru_wx[j]), lru_bx[j][None, :],
                         lru_lambda[j][None, :])
            b_out = _fox_attention(q, qe, k, ke, _value_transpose(v, ATTN_TILE))
            w_out = bf(ev_w_out[j])
            mixed = (a_out, b_out, w_out[:LRU_WIDTH], w_out[LRU_WIDTH:])
        else:
            w_in = bf(od_w_in[j])
            kvd = SWA_KV_HEADS * HEAD_DIM
            o1, o2 = SWA_DIM, SWA_DIM + 2 * kvd
            q, kd, v, u = _odd_inproj(
                x, g, w_in[:, :o1], w_in[:, o1:o2], w_in[:, o2:],
                per_head(swa_q_norm[j], SWA_HEADS), per_head(swa_k_norm[j], SWA_KV_HEADS), cos, sin)
            c_out = _swa_attention(q, kd, _value_transpose(v, SWA_WINDOW), swa_sinks[j][None, :])
            ar, ai, bbr, bbi = _s5_prep(s5_lambda_re[j], s5_lambda_im[j], s5_log_dt[j],
                                        s5_b_re[j], s5_b_im[j])
            bmat, cmat, a_r, a_i = _s5_matrices(ar, ai, bbr, bbi, s5_c_re[j], s5_c_im[j])
            d_out = _s5(u, bmat, cmat, a_r, a_i, s5_d[j][None, :],
                        bf(s5_glu_w[j]), s5_glu_b[j][None, :])
            w_out = bf(od_w_out[j])
            mixed = (d_out, c_out, w_out[SWA_DIM:], w_out[:SWA_DIM])
        x = _mix_ffn_ple(x, *mixed, i, f2, ple)
    return x
```

```python
import numpy as np

import jax
import jax.numpy as jnp
from jax import lax
from jax.experimental import pallas as pl
from jax.experimental.pallas import tpu as pltpu

F32 = jnp.float32
BF16 = jnp.bfloat16

D_MODEL = 1024
HEAD_DIM = 64
LRU_WIDTH = 512
LRU_CONV = 4
LRU_C = 8.0
FOX_HEADS = 8
FOX_DIM = 512
SWA_HEADS = 8
SWA_KV_HEADS = 2
SWA_GROUP = SWA_HEADS // SWA_KV_HEADS
SWA_DIM = 512
SWA_WINDOW = 128
S5_WIDTH = 512
S5_GROUP = 16
S5_GROUPS = 32
S5_STATE = 64
D_FF = 2816
PLE_DIM = 256
ROPE_THETA = 10000.0
EPS = 1e-6
MACARON = 0.5
QK_SCALE = HEAD_DIM ** -0.5
LOG2E = 1.4426950408889634

SUBLANES = 8
LANES = 128
TIME_CHUNK = 64
ROW_TILE = SUBLANES * TIME_CHUNK
ATTN_TILE = 512
S5_LANE_GROUPS = 4
S5_CHUNKS = 2
BIAS_LANES = 6
VT_ROWS = 80
VMEM_LIMIT = 56 * 1024 * 1024


def _dot(a, b):
    return jnp.dot(a, b, preferred_element_type=F32)


def _dot_nt(a, b):
    return lax.dot_general(a, b, (((1,), (1,)), ((), ())), preferred_element_type=F32)


def _rms(x, g):
    ms = jnp.mean(x * x, axis=-1, keepdims=True)
    return x * lax.rsqrt(ms + EPS) * g


def _head_rms(x, ind, gain):
    ms = _dot((x * x).astype(BF16), ind)
    return x * lax.rsqrt(ms + EPS) * gain


def _softplus(x):
    return jnp.maximum(x, 0.0) + jnp.log1p(jnp.exp(-jnp.abs(x)))


def _log_sigmoid(x):
    return -_softplus(-x)


def _cumsum_time(x):
    n = x.shape[0]
    row = lax.broadcasted_iota(jnp.int32, x.shape, 0)
    d = SUBLANES
    while d < n:
        x = x + jnp.where(row >= d, pltpu.roll(x, d, axis=0), 0.0)
        d *= 2
    return x


def _time_major_perm(steps):
    r = np.arange(SUBLANES * steps)
    src = (r % SUBLANES) * steps + r // SUBLANES
    return jnp.asarray(src[:, None] == r[None, :], dtype=BF16)


def _permute_rows(perm, x):
    return _dot(perm, x).astype(BF16)


def _split3_bf16(c):
    hi = c.astype(BF16).astype(F32)
    r = c - hi
    mid = r.astype(BF16).astype(F32)
    return hi, mid, r - mid


def _keep_lanes(x, lo, hi):
    lane = lax.broadcasted_iota(jnp.int32, x.shape, 1)
    return jnp.where((lane >= lo) & (lane < hi), x.astype(F32), 0.0).astype(BF16)


def _params(*sem):
    return pltpu.CompilerParams(dimension_semantics=sem, vmem_limit_bytes=VMEM_LIMIT)


def _const_spec(shape):
    nd = len(shape)
    return pl.BlockSpec(shape, lambda *_: (0,) * nd, pipeline_mode=pl.Buffered(1))


def _layer_spec(shape, layer):
    nd = len(shape)
    return pl.BlockSpec((None,) + tuple(shape), lambda *_: (layer,) + (0,) * nd,
                        pipeline_mode=pl.Buffered(1))


def _head_mean_matrix(width):
    h = np.arange(width) // HEAD_DIM
    return jnp.asarray((h[:, None] == h[None, :]) / HEAD_DIM, dtype=BF16)


def _swiglu_update(x, g_ref, wg_ref, wu_ref, wd_ref):
    n = _rms(x, g_ref[...]).astype(BF16)
    hg = _dot(n, wg_ref[...])
    hu = _dot(n, wu_ref[...])
    act = (hg * jax.nn.sigmoid(hg) * hu).astype(BF16)
    return x + MACARON * _dot(act, wd_ref[...])


def _ffn_kernel(x_ref, g_ref, wg_ref, wu_ref, wd_ref, o_ref):
    o_ref[...] = _swiglu_update(x_ref[...], g_ref, wg_ref, wu_ref, wd_ref)


def _ffn(x2d, layer, norm, wg, wu, wd):
    T, D = x2d.shape
    tm = min(ROW_TILE, T)
    row = pl.BlockSpec((tm, D), lambda i: (i, 0))
    return pl.pallas_call(
        _ffn_kernel, grid=(T // tm,),
        in_specs=[row, _layer_spec((1, D), layer), _layer_spec((D, D_FF), layer),
                  _layer_spec((D, D_FF), layer), _layer_spec((D_FF, D), layer)],
        out_specs=row, out_shape=jax.ShapeDtypeStruct((T, D), F32),
        compiler_params=_params("parallel"), name="ffn",
    )(x2d, norm, wg, wu, wd)


def _mix_ffn_ple_kernel(x_ref, tmaj_ref, bmaj_ref, wt_ref, wb_ref, unperm_ref,
                        g_ref, wg_ref, wu_ref, wd_ref, p_ref, pw_ref, pn_ref, gn_ref, gw_ref, o_ref):
    B, ts, D = x_ref.shape
    rows = B * ts
    part_t = _permute_rows(unperm_ref[...], tmaj_ref[...])
    part_b = bmaj_ref[...].reshape(rows, bmaj_ref.shape[2])
    x = x_ref[...].reshape(rows, D) + _dot(part_t, wt_ref[...]) + _dot(part_b, wb_ref[...])
    x = _swiglu_update(x, g_ref, wg_ref, wu_ref, wd_ref)
    e = _rms(_dot(p_ref[...].reshape(rows, PLE_DIM).astype(BF16), pw_ref[...]), pn_ref[...])
    gate = jax.nn.sigmoid(_dot(_rms(x, gn_ref[...]).astype(BF16), gw_ref[...]))
    o_ref[...] = (x + gate * e).reshape(B, ts, D)


def _mix_ffn_ple(x, part_tmaj, part_bmaj, w_tmaj, w_bmaj, layer, ffn, ple):
    B, S, D = x.shape
    ts = TIME_CHUNK
    W = part_tmaj.shape[1]
    unperm = _time_major_perm(ts).T
    xs = pl.BlockSpec((B, ts, D), lambda s: (0, s, 0))
    in_specs = [xs, pl.BlockSpec((B * ts, W), lambda s: (s, 0)),
                pl.BlockSpec((B, ts, W), lambda s: (0, s, 0)),
                _const_spec((W, D)), _const_spec((W, D)), _const_spec(unperm.shape),
                _layer_spec((1, D), layer), _layer_spec((D, D_FF), layer),
                _layer_spec((D, D_FF), layer), _layer_spec((D_FF, D), layer),
                pl.BlockSpec((None, B, ts, PLE_DIM), lambda s: (layer, 0, s, 0)),
                _layer_spec((PLE_DIM, D), layer), _layer_spec((1, D), layer),
                _layer_spec((1, D), layer), _layer_spec((D, D), layer)]
    return pl.pallas_call(
        _mix_ffn_ple_kernel, grid=(S // ts,), in_specs=in_specs, out_specs=xs,
        out_shape=jax.ShapeDtypeStruct((B, S, D), F32),
        compiler_params=_params("parallel"), name="mix_ffn_ple",
    )(x, part_tmaj, part_bmaj, w_tmaj, w_bmaj, unperm, *ffn, *ple)


def _bias_lane_maps():
    eq = np.zeros((LANES, LANES), np.float32)
    ek = np.zeros((LANES, LANES), np.float32)
    oq = np.zeros((1, LANES), np.float32)
    ok = np.zeros((1, LANES), np.float32)
    for h in range(FOX_HEADS):
        for i in range(3):
            eq[i * FOX_HEADS + h, BIAS_LANES * h + i] = 1.0
            ek[i * FOX_HEADS + h, BIAS_LANES * h + 3 + i] = -1.0
            oq[0, BIAS_LANES * h + 3 + i] = 1.0
            ok[0, BIAS_LANES * h + i] = 1.0
    return jnp.asarray(eq, BF16), jnp.asarray(ek, BF16), jnp.asarray(oq), jnp.asarray(ok)


def _even_in_kernel(x_ref, g_ref, wxy_ref, wqkv_ref, wf_ref, bf_ref, qg_ref, kg_ref, ind_ref,
                    eq_ref, ek_ref, oq_ref, ok_ref, perm_ref, unperm_ref,
                    xa_ref, ya_ref, q_ref, k_ref, qe_ref, ke_ref, v_ref, carry_ref):
    B, ts, D = x_ref.shape
    rows = B * ts

    @pl.when(pl.program_id(0) == 0)
    def _():
        carry_ref[...] = jnp.zeros_like(carry_ref)

    n = _rms(x_ref[...].reshape(rows, D), g_ref[...]).astype(BF16)
    n_t = _permute_rows(perm_ref[...], n)
    xy = _dot(n_t, wxy_ref[...])
    xa_ref[...] = xy[:, :LRU_WIDTH]
    ya_ref[...] = xy[:, LRU_WIDTH:]

    logf = _log_sigmoid(_dot(n_t, wf_ref[...]) + bf_ref[...]) * LOG2E
    c = _cumsum_time(logf) + pltpu.repeat(carry_ref[...], ts, axis=0)
    carry_ref[...] = c[rows - SUBLANES:rows, :]
    hi, mid, lo = _split3_bf16(c)
    lane = lax.broadcasted_iota(jnp.int32, c.shape, 1)
    packed = jnp.where(lane < FOX_HEADS, hi, jnp.where(
        lane < 2 * FOX_HEADS, pltpu.roll(mid, FOX_HEADS, axis=1), jnp.where(
            lane < 3 * FOX_HEADS, pltpu.roll(lo, 2 * FOX_HEADS, axis=1), 0.0)))
    packed = _permute_rows(unperm_ref[...], packed.astype(BF16))
    qe_ref[...] = (_dot(packed, eq_ref[...]) + oq_ref[...]).astype(BF16).reshape(B, ts, LANES)
    ke_ref[...] = (_dot(packed, ek_ref[...]) + ok_ref[...]).astype(BF16).reshape(B, ts, LANES)

    qkv = _dot(n, wqkv_ref[...])
    ind = ind_ref[...]
    q = _head_rms(qkv[:, :FOX_DIM], ind, qg_ref[...]) * (QK_SCALE * LOG2E)
    k = _head_rms(qkv[:, FOX_DIM:2 * FOX_DIM], ind, kg_ref[...])
    q_ref[...] = q.astype(BF16).reshape(B, ts, FOX_DIM)
    k_ref[...] = k.astype(BF16).reshape(B, ts, FOX_DIM)
    v_ref[...] = qkv[:, 2 * FOX_DIM:].astype(BF16).reshape(B, ts, FOX_DIM)


def _even_inproj(x, g, w_xy, w_qkv, w_f, b_f, qg, kg):
    B, S, D = x.shape
    ts = TIME_CHUNK
    eq, ek, oq, ok = _bias_lane_maps()
    perm = _time_major_perm(ts)
    tmaj = pl.BlockSpec((B * ts, LRU_WIDTH), lambda s: (s, 0))
    bmaj = lambda w: pl.BlockSpec((B, ts, w), lambda s: (0, s, 0))
    consts = [g, w_xy, w_qkv, w_f, b_f, qg, kg, _head_mean_matrix(FOX_DIM), eq, ek, oq, ok,
              perm, perm.T]
    return pl.pallas_call(
        _even_in_kernel, grid=(S // ts,),
        in_specs=[bmaj(D)] + [_const_spec(a.shape) for a in consts],
        out_specs=[tmaj, tmaj, bmaj(FOX_DIM), bmaj(FOX_DIM), bmaj(LANES), bmaj(LANES),
                   bmaj(FOX_DIM)],
        out_shape=[jax.ShapeDtypeStruct((S * B, LRU_WIDTH), F32),
                   jax.ShapeDtypeStruct((S * B, LRU_WIDTH), F32),
                   jax.ShapeDtypeStruct((B, S, FOX_DIM), BF16),
                   jax.ShapeDtypeStruct((B, S, FOX_DIM), BF16),
                   jax.ShapeDtypeStruct((B, S, LANES), BF16),
                   jax.ShapeDtypeStruct((B, S, LANES), BF16),
                   jax.ShapeDtypeStruct((B, S, FOX_DIM), BF16)],
        scratch_shapes=[pltpu.VMEM((SUBLANES, LANES), F32)],
        compiler_params=_params("arbitrary"), name="even_inproj",
    )(x, *consts)


def _vt_kernel(v_ref, o_ref):
    heads, nchunk, _, chunk = o_ref.shape
    vt = v_ref[...].astype(F32).T
    ones = jnp.ones((VT_ROWS - HEAD_DIM, chunk), BF16)
    for h in range(heads):
        for c in range(nchunk):
            o_ref[h, c, 0:HEAD_DIM, :] = vt[h * HEAD_DIM:(h + 1) * HEAD_DIM,
                                            c * chunk:(c + 1) * chunk].astype(BF16)
            o_ref[h, c, HEAD_DIM:VT_ROWS, :] = ones


def _value_transpose(v, chunk):
    B, S, W = v.shape
    heads = W // HEAD_DIM
    tk = min(ATTN_TILE, S)
    chunk = min(chunk, tk)
    return pl.pallas_call(
        _vt_kernel, grid=(B, S // tk),
        in_specs=[pl.BlockSpec((None, tk, W), lambda b, j: (b, j, 0))],
        out_specs=pl.BlockSpec((None, heads, tk // chunk, VT_ROWS, chunk),
                               lambda b, j: (b, 0, j, 0, 0)),
        out_shape=jax.ShapeDtypeStruct((B, heads, S // chunk, VT_ROWS, chunk), BF16),
        compiler_params=_params("parallel", "parallel"), name="value_transpose",
    )(v)


def _lru_kernel(xa_ref, ya_ref, cw_ref, cb_ref, wa_ref, ba_ref, wx_ref, bx_ref, lam_ref, o_ref,
                xpad_ref, a_ref, h_ref, carry_ref):
    rows = xa_ref.shape[0]
    halo = (LRU_CONV - 1) * SUBLANES
    half = LRU_WIDTH // 2

    @pl.when(pl.program_id(0) == 0)
    def _():
        xpad_ref[0:halo, :] = jnp.zeros((halo, LRU_WIDTH), F32)
        carry_ref[...] = jnp.zeros_like(carry_ref)

    xpad_ref[halo:halo + rows, :] = xa_ref[...]
    xc = cb_ref[...]
    for tap in range(LRU_CONV):
        xc = xc + xpad_ref[tap * SUBLANES:tap * SUBLANES + rows, :] * cw_ref[tap:tap + 1, :]
    xpad_ref[0:halo, :] = xpad_ref[rows:rows + halo, :]

    xb = xc.astype(BF16)

    def gate(w_ref, b_ref):
        z = jnp.concatenate([_dot(xb[:, :half], w_ref[0]), _dot(xb[:, half:], w_ref[1])], axis=-1)
        return jax.nn.sigmoid(z + b_ref[...])

    r = gate(wa_ref, ba_ref)
    i = gate(wx_ref, bx_ref)
    log_a = -LRU_C * r * _softplus(lam_ref[...])
    a_ref[...] = jnp.exp(log_a)
    th = jnp.tanh(log_a)
    h_ref[...] = jnp.sqrt(-2.0 * th / (1.0 - th)) * (i * xc)

    def step(t, h):
        sl = pl.ds(pl.multiple_of(t * SUBLANES, SUBLANES), SUBLANES)
        h = a_ref[sl, :] * h + h_ref[sl, :]
        h_ref[sl, :] = h
        return h

    carry_ref[...] = lax.fori_loop(0, rows // SUBLANES, step, carry_ref[...], unroll=8)
    o_ref[...] = (jax.nn.gelu(ya_ref[...]) * h_ref[...]).astype(BF16)


def _lru(xa_t, ya_t, conv_w, conv_b, wa_bd, ba, wx_bd, bx, lam):
    R, W = xa_t.shape
    rows = TIME_CHUNK * SUBLANES
    halo = (LRU_CONV - 1) * SUBLANES
    blk = pl.BlockSpec((rows, W), lambda t: (t, 0))
    return pl.pallas_call(
        _lru_kernel, grid=(R // rows,),
        in_specs=[blk, blk, _const_spec((LRU_CONV, W)), _const_spec((1, W)),
                  _const_spec(wa_bd.shape), _const_spec((1, W)),
                  _const_spec(wx_bd.shape), _const_spec((1, W)), _const_spec((1, W))],
        out_specs=blk, out_shape=jax.ShapeDtypeStruct((R, W), BF16),
        scratch_shapes=[pltpu.VMEM((halo + rows, W), F32), pltpu.VMEM((rows, W), F32),
                        pltpu.VMEM((rows, W), F32), pltpu.VMEM((SUBLANES, W), F32)],
        compiler_params=_params("arbitrary"), name="rg_lru",
    )(xa_t, ya_t, conv_w, conv_b, wa_bd, ba, wx_bd, bx, lam)


def _fox_kernel(q_ref, qe_ref, k_ref, ke_ref, vt_ref, o_ref, acc_ref, s_ref):
    tq = q_ref.shape[0]
    qi = pl.program_id(1)
    qe = qe_ref[...]
    qq = []
    for h in range(FOX_HEADS):
        half = h % 2
        blk = q_ref[:, (h // 2) * LANES:(h // 2 + 1) * LANES]
        qq.append(jnp.concatenate(
            [_keep_lanes(blk, half * HEAD_DIM, (half + 1) * HEAD_DIM),
             _keep_lanes(qe, BIAS_LANES * h, BIAS_LANES * (h + 1))], axis=-1))
    key = lax.broadcasted_iota(jnp.int32, (tq, tq), 0)
    qry = lax.broadcasted_iota(jnp.int32, (tq, tq), 1)
    acc_ref[...] = jnp.zeros_like(acc_ref)

    def update(j, m_run, masked):
        ks = pl.ds(pl.multiple_of(j * tq, tq), tq)
        ke = ke_ref[ks, :]
        for h in range(FOX_HEADS):
            kk = jnp.concatenate([k_ref[ks, (h // 2) * LANES:(h // 2 + 1) * LANES], ke], axis=-1)
            s = _dot_nt(kk, qq[h])
            if masked:
                s = jnp.where(key <= qry, s, -jnp.inf)
            s_ref[h] = s
        new = []
        for h in range(FOX_HEADS):
            m_new = jnp.maximum(m_run[h], jnp.max(s_ref[h], axis=0, keepdims=True))
            alpha = jnp.exp2(m_run[h] - m_new)
            p = jnp.exp2(s_ref[h] - m_new).astype(BF16)
            acc_ref[h] = alpha * acc_ref[h] + _dot(vt_ref[h, j], p)
            new.append(m_new)
        return tuple(new)

    init = (jnp.full((1, tq), -jnp.inf, F32),) * FOX_HEADS
    m_run = lax.fori_loop(0, qi, lambda j, m: update(j, m, False), init)
    update(qi, m_run, True)
    for pair in range(FOX_HEADS // 2):
        out = [acc_ref[h, 0:HEAD_DIM, :] / acc_ref[h, HEAD_DIM:HEAD_DIM + 1, :]
               for h in (2 * pair, 2 * pair + 1)]
        o_ref[:, pair * LANES:(pair + 1) * LANES] = jnp.concatenate(out, axis=0).T.astype(BF16)


def _fox_attention(q, qe, k, ke, vt):
    B, S, _ = q.shape
    nk, tq = vt.shape[2], vt.shape[4]
    qblk = lambda w: pl.BlockSpec((None, tq, w), lambda b, i: (b, i, 0))
    kblk = lambda w: pl.BlockSpec((None, S, w), lambda b, i: (b, 0, 0))
    return pl.pallas_call(
        _fox_kernel, grid=(B, S // tq),
        in_specs=[qblk(FOX_DIM), qblk(LANES), kblk(FOX_DIM), kblk(LANES),
                  pl.BlockSpec((None, FOX_HEADS, nk, VT_ROWS, tq), lambda b, i: (b, 0, 0, 0, 0))],
        out_specs=qblk(FOX_DIM),
        out_shape=jax.ShapeDtypeStruct((B, S, FOX_DIM), BF16),
        scratch_shapes=[pltpu.VMEM((FOX_HEADS, VT_ROWS, tq), F32),
                        pltpu.VMEM((FOX_HEADS, tq, tq), F32)],
        compiler_params=_params("parallel", "arbitrary"), name="fox_attention",
    )(q, qe, k, ke, vt)


def _rope_table_kernel(inv_ref, cos_ref, sin_ref):
    rows = cos_ref.shape[0]
    pos = pl.program_id(0) * rows + lax.broadcasted_iota(jnp.int32, cos_ref.shape, 0)
    lane = lax.broadcasted_iota(jnp.int32, cos_ref.shape, 1)
    ang = pos.astype(F32) * inv_ref[...]
    cos_ref[...] = jnp.cos(ang)
    sin_ref[...] = jnp.where((lane & (HEAD_DIM - 1)) < HEAD_DIM // 2, -1.0, 1.0) * jnp.sin(ang)


def _rope_tables(S):
    half = HEAD_DIM // 2
    inv = jnp.power(ROPE_THETA, -jnp.arange(half, dtype=F32) / half)
    inv4 = jnp.tile(inv, LANES // half)[None, :]
    rows = min(ROW_TILE, S)
    blk = pl.BlockSpec((rows, LANES), lambda i: (i, 0))
    return pl.pallas_call(
        _rope_table_kernel, grid=(S // rows,),
        in_specs=[pl.BlockSpec((1, LANES), lambda i: (0, 0))], out_specs=[blk, blk],
        out_shape=[jax.ShapeDtypeStruct((S, LANES), F32)] * 2,
        compiler_params=_params("parallel"), name="rope_tables",
    )(inv4)


def _rope(x, cos, sin):
    half = HEAD_DIM // 2
    lane = lax.broadcasted_iota(jnp.int32, x.shape, 1)
    first = (lane & (HEAD_DIM - 1)) < half
    swapped = jnp.where(first, pltpu.roll(x, LANES - half, axis=1), pltpu.roll(x, half, axis=1))
    return x * cos + swapped * sin


def _dup_halves(x):
    lane = lax.broadcasted_iota(jnp.int32, x.shape, 1)
    r = pltpu.roll(x, HEAD_DIM, axis=1)
    lo = lane < HEAD_DIM
    return jnp.concatenate([jnp.where(lo, x, r), jnp.where(lo, r, x)], axis=-1)


def _odd_in_kernel(x_ref, g_ref, wq_ref, wkv_ref, wu_ref, qg_ref, kg_ref, ind_ref, perm_ref,
                   cos_ref, sin_ref, q_ref, kd_ref, v_ref, u_ref):
    B, ts, D = x_ref.shape
    rows = B * ts
    n = _rms(x_ref[...].reshape(rows, D), g_ref[...]).astype(BF16)
    u_ref[...] = _dot(_permute_rows(perm_ref[...], n), wu_ref[...])
    cos = jnp.concatenate([cos_ref[...]] * B, axis=0)
    sin = jnp.concatenate([sin_ref[...]] * B, axis=0)
    q = _head_rms(_dot(n, wq_ref[...]), ind_ref[...], qg_ref[...])
    q = jnp.concatenate([_rope(q[:, blk * LANES:(blk + 1) * LANES], cos, sin)
                         for blk in range(SWA_DIM // LANES)], axis=-1) * (QK_SCALE * LOG2E)
    q_ref[...] = q.astype(BF16).reshape(B, ts, SWA_DIM)
    kv = _dot(n, wkv_ref[...])
    k = _head_rms(kv[:, :LANES], ind_ref[0:LANES, 0:LANES], kg_ref[...])
    kd_ref[...] = _dup_halves(_rope(k, cos, sin)).astype(BF16).reshape(B, ts, 2 * LANES)
    v_ref[...] = kv[:, LANES:].astype(BF16).reshape(B, ts, LANES)


def _odd_inproj(x, g, w_q, w_kv, w_u, qg, kg, cos, sin):
    B, S, D = x.shape
    ts = TIME_CHUNK
    bmaj = lambda w: pl.BlockSpec((B, ts, w), lambda s: (0, s, 0))
    tab = pl.BlockSpec((ts, LANES), lambda s: (s, 0))
    consts = [g, w_q, w_kv, w_u, qg, kg, _head_mean_matrix(SWA_DIM), _time_major_perm(ts)]
    return pl.pallas_call(
        _odd_in_kernel, grid=(S // ts,),
        in_specs=[bmaj(D)] + [_const_spec(a.shape) for a in consts] + [tab, tab],
        out_specs=[bmaj(SWA_DIM), bmaj(2 * LANES), bmaj(LANES),
                   pl.BlockSpec((B * ts, S5_WIDTH), lambda s: (s, 0))],
        out_shape=[jax.ShapeDtypeStruct((B, S, SWA_DIM), BF16),
                   jax.ShapeDtypeStruct((B, S, 2 * LANES), BF16),
                   jax.ShapeDtypeStruct((B, S, LANES), BF16),
                   jax.ShapeDtypeStruct((S * B, S5_WIDTH), F32)],
        compiler_params=_params("parallel"), name="odd_inproj",
    )(x, *consts, cos, sin)


def _swa_kernel(q_ref, kd_ref, vt_ref, sink_ref, o_ref):
    tq = q_ref.shape[0]
    W = SWA_WINDOW
    G = SWA_GROUP
    base = pl.program_id(1) * tq
    key = lax.broadcasted_iota(jnp.int32, (2 * W, G * W), 0)
    qoff = lax.broadcasted_iota(jnp.int32, (2 * W, G * W), 1) & (W - 1)
    cgrp = lax.broadcasted_iota(jnp.int32, (1, G * W), 1) // W
    sinks = sink_ref[...] * LOG2E
    blocks = [(n, kvh) for n in range(tq // W) for kvh in range(SWA_KV_HEADS)]
    kstarts = [pl.multiple_of(jnp.maximum(base + n * W - W, 0), W) for n in range(tq // W)]
    scores = []
    for n, kvh in blocks:
        parts = []
        for g in range(G):
            head = kvh * G + g
            blk = q_ref[n * W:(n + 1) * W, (head // 2) * LANES:(head // 2 + 1) * LANES]
            half = head % 2
            parts.append(_keep_lanes(blk, half * HEAD_DIM, (half + 1) * HEAD_DIM))
        scores.append(_dot_nt(kd_ref[pl.ds(kstarts[n], 2 * W), kvh * LANES:(kvh + 1) * LANES],
                              jnp.concatenate(parts, axis=0)))
    for (n, kvh), s in zip(blocks, scores):
        rows = slice(n * W, (n + 1) * W)
        chunk = kstarts[n] // W
        diff = (base + n * W + qoff) - (kstarts[n] + key)
        s = jnp.where((diff >= 0) & (diff < W), s, -jnp.inf)
        sink = jnp.zeros((1, G * W), F32)
        for g in range(G):
            head = kvh * G + g
            sink = jnp.where(cgrp == g, sinks[:, head:head + 1], sink)
        m = jnp.maximum(jnp.max(s, axis=0, keepdims=True), sink)
        p = jnp.exp2(s - m).astype(BF16)
        acc = _dot(vt_ref[kvh, chunk], p[0:W, :]) + _dot(vt_ref[kvh, chunk + 1], p[W:2 * W, :])
        den = acc[HEAD_DIM:HEAD_DIM + 1, :] + jnp.exp2(sink - m)
        o = acc[0:HEAD_DIM, :] / den
        for pair in range(G // 2):
            both = jnp.concatenate([o[:, (2 * pair) * W:(2 * pair + 1) * W],
                                    o[:, (2 * pair + 1) * W:(2 * pair + 2) * W]], axis=0)
            lb = kvh * (G // 2) + pair
            o_ref[rows, lb * LANES:(lb + 1) * LANES] = both.T.astype(BF16)


def _swa_attention(q, kd, vt, sinks):
    B, S, _ = q.shape
    tq = min(ATTN_TILE, S)
    return pl.pallas_call(
        _swa_kernel, grid=(B, S // tq),
        in_specs=[pl.BlockSpec((None, tq, SWA_DIM), lambda b, i: (b, i, 0)),
                  pl.BlockSpec((None, S, 2 * LANES), lambda b, i: (b, 0, 0)),
                  pl.BlockSpec((None,) + vt.shape[1:], lambda b, i: (b, 0, 0, 0, 0)),
                  pl.BlockSpec((1, SWA_HEADS), lambda b, i: (0, 0))],
        out_specs=pl.BlockSpec((None, tq, SWA_DIM), lambda b, i: (b, i, 0)),
        out_shape=jax.ShapeDtypeStruct((B, S, SWA_DIM), BF16),
        compiler_params=_params("parallel", "arbitrary"), name="swa_attention",
    )(q, kd, vt, sinks)


def _s5_prep_kernel(lr_ref, li_ref, ldt_ref, br_ref, bi_ref, ar_ref, ai_ref, bbr_ref, bbi_ref):
    lr, li = lr_ref[...], li_ref[...]
    dt = jnp.exp(ldt_ref[...])
    mag = jnp.exp(lr * dt)
    ar = mag * jnp.cos(li * dt)
    ai = mag * jnp.sin(li * dt)
    den = lr * lr + li * li
    cr = ((ar - 1.0) * lr + ai * li) / den
    ci = (ai * lr - (ar - 1.0) * li) / den
    br, bi = br_ref[...], bi_ref[...]
    ar_ref[...] = ar
    ai_ref[...] = ai
    bbr_ref[...] = cr * br - ci * bi
    bbi_ref[...] = cr * bi + ci * br


def _s5_prep(lam_re, lam_im, log_dt, b_re, b_im):
    G, P, C = b_re.shape
    rep = lambda a: jnp.repeat(a, C, axis=0)
    bt = lambda a: a.transpose(0, 2, 1).reshape(G * C, P)
    ldt = jnp.broadcast_to(log_dt[:, None], (G, P))
    full = pl.BlockSpec((G * C, P), lambda: (0, 0))
    outs = pl.pallas_call(
        _s5_prep_kernel, in_specs=[full] * 5, out_specs=[full] * 4,
        out_shape=[jax.ShapeDtypeStruct((G * C, P), F32)] * 4, name="s5_prep",
    )(rep(lam_re), rep(lam_im), rep(ldt), bt(b_re), bt(b_im))
    ar, ai, bbr, bbi = [o.reshape(G, C, P) for o in outs]
    return ar[:, 0], ai[:, 0], bbr, bbi


def _s5_kernel(u_ref, bm_ref, cm_ref, ar_ref, ai_ref, d_ref, gw_ref, gb_ref, o_ref,
               h_ref, carry_ref):
    rows = u_ref.shape[0] // S5_CHUNKS
    steps = rows // SUBLANES
    half = h_ref.shape[3] // 2

    @pl.when(pl.program_id(0) == 0)
    def _():
        carry_ref[...] = jnp.zeros_like(carry_ref)

    def chunk_rows(c):
        return slice(c * rows, (c + 1) * rows)

    def input_map(c):
        ub = u_ref[chunk_rows(c), :].astype(BF16)
        for g in range(S5_LANE_GROUPS):
            h_ref[c, g] = _dot(ub[:, g * LANES:(g + 1) * LANES], bm_ref[g])

    def scan(c):
        for g in range(S5_LANE_GROUPS):
            ar = jnp.broadcast_to(ar_ref[g], (SUBLANES, half))
            ai = jnp.broadcast_to(ai_ref[g], (SUBLANES, half))
            hr = carry_ref[g, :, 0:half]
            hi = carry_ref[g, :, half:2 * half]
            for t in range(steps):
                sl = slice(t * SUBLANES, (t + 1) * SUBLANES)
                nr = ar * hr - ai * hi + h_ref[c, g, sl, 0:half]
                ni = ar * hi + ai * hr + h_ref[c, g, sl, half:2 * half]
                h_ref[c, g, sl, 0:half] = nr
                h_ref[c, g, sl, half:2 * half] = ni
                hr, hi = nr, ni
            carry_ref[g, :, 0:half] = hr
            carry_ref[g, :, half:2 * half] = hi

    def output_map(c):
        u = u_ref[chunk_rows(c), :]
        ys = [_dot(h_ref[c, g].astype(BF16), cm_ref[g]) for g in range(S5_LANE_GROUPS)]
        y = jnp.concatenate(ys, axis=-1) + d_ref[...] * u
        z = jax.nn.gelu(y)
        gate = jax.nn.sigmoid(_dot(z.astype(BF16), gw_ref[...]) + gb_ref[...])
        o_ref[chunk_rows(c), :] = (z * gate).astype(BF16)

    for c in range(S5_CHUNKS):
        input_map(c)
    for c in range(S5_CHUNKS):
        scan(c)
        output_map(c)


def _s5(u_t, bmat, cmat, ar, ai, d, glu_w, glu_b):
    R, W = u_t.shape
    rows = S5_CHUNKS * TIME_CHUNK * SUBLANES
    nstate = bmat.shape[2]
    blk = pl.BlockSpec((rows, W), lambda t: (t, 0))
    return pl.pallas_call(
        _s5_kernel, grid=(R // rows,),
        in_specs=[blk, _const_spec(bmat.shape), _const_spec(cmat.shape), _const_spec(ar.shape),
                  _const_spec(ai.shape), _const_spec((1, W)), _const_spec((W, W)),
                  _const_spec((1, W))],
        out_specs=blk, out_shape=jax.ShapeDtypeStruct((R, W), BF16),
        scratch_shapes=[pltpu.VMEM((S5_CHUNKS, S5_LANE_GROUPS, rows // S5_CHUNKS, nstate), F32),
                        pltpu.VMEM((S5_LANE_GROUPS, SUBLANES, nstate), F32)],
        compiler_params=_params("arbitrary"), name="s5_glu",
    )(u_t, bmat, cmat, ar, ai, d, glu_w, glu_b)


def _s5_matrices(ar, ai, bbr, bbi, c_re, c_im):
    L, GL = S5_LANE_GROUPS, S5_GROUPS // S5_LANE_GROUPS
    C, P = S5_GROUP, S5_STATE
    eye = jnp.eye(GL, dtype=F32)

    def inmap(b):
        return jnp.einsum("lgcp,gh->lgchp", b.reshape(L, GL, C, P), eye).reshape(L, GL * C, GL * P)

    def outmap(c):
        return jnp.einsum("lgcp,gh->lgphc", c.reshape(L, GL, C, P), eye).reshape(L, GL * P, GL * C)

    bmat = jnp.concatenate([inmap(bbr), inmap(bbi)], axis=2).astype(BF16)
    cmat = jnp.concatenate([outmap(c_re), outmap(-c_im)], axis=1).astype(BF16)
    a_r = ar.reshape(L, 1, GL * P)
    a_i = ai.reshape(L, 1, GL * P)
    return bmat, cmat, a_r, a_i


def _block_diag_pairs(w):
    nb, bs, _ = w.shape
    half = nb // 2
    eye = jnp.eye(half, dtype=w.dtype)
    out = jnp.einsum("thij,hk->thikj", w.reshape(2, half, bs, bs), eye)
    return out.reshape(2, half * bs, half * bs).astype(BF16)


def kernel(x, p, ffn1_norm, ffn1_wg, ffn1_wu, ffn1_wd, mix_norm, ffn2_norm, ffn2_wg, ffn2_wu, ffn2_wd, ple_w, ple_norm, ple_gate_norm, ple_gate_w, ev_w_in, lru_conv_w, lru_conv_b, lru_wa, lru_ba, lru_wx, lru_bx, lru_lambda, fox_bf, fox_q_norm, fox_k_norm, ev_w_out, od_w_in, swa_q_norm, swa_k_norm, swa_sinks, s5_lambda_re, s5_lambda_im, s5_log_dt, s5_b_re, s5_b_im, s5_c_re, s5_c_im, s5_d, s5_glu_w, s5_glu_b, od_w_out):
    B, S, D = x.shape
    depth = p.shape[0]
    assert B == SUBLANES and D == D_MODEL and S % (S5_CHUNKS * TIME_CHUNK) == 0
    T = B * S
    bf = lambda a: a.astype(BF16)
    row = lambda a: a[:, None, :]
    per_head = lambda gain, heads: jnp.tile(gain, heads)[None, :]

    f1 = (row(ffn1_norm), bf(ffn1_wg), bf(ffn1_wu), bf(ffn1_wd))
    f2 = (row(ffn2_norm), bf(ffn2_wg), bf(ffn2_wu), bf(ffn2_wd))
    ple = (p, bf(ple_w), row(ple_norm), row(ple_gate_norm), bf(ple_gate_w))
    cos, sin = _rope_tables(S)

    for i in range(depth):
        j = i // 2
        x = _ffn(x.reshape(T, D), i, *f1).reshape(B, S, D)
        g = mix_norm[i][None, :]
        if i % 2 == 0:
            w_in = bf(ev_w_in[j])
            o1, o2 = 2 * LRU_WIDTH, 2 * LRU_WIDTH + 3 * FOX_DIM
            w_f = jnp.pad(w_in[:, o2:], ((0, 0), (0, LANES - FOX_HEADS)))
            b_f = jnp.pad(fox_bf[j], (0, LANES - FOX_HEADS))[None, :]
            xa, ya, q, k, qe, ke, v = _even_inproj(
                x, g, w_in[:, :o1], w_in[:, o1:o2], w_f, b_f,
                per_head(fox_q_norm[j], FOX_HEADS), per_head(fox_k_norm[j], FOX_HEADS))
            a_out = _lru(xa, ya, lru_conv_w[j], lru_conv_b[j][None, :],
                         _block_diag_pairs(lru_wa[j]), lru_ba[j][None, :],
                         _block_diag_pairs(lru_wx[j]), lru_bx[j][None, :],
                         lru_lambda[j][None, :])
            b_out = _fox_attention(q, qe, k, ke, _value_transpose(v, ATTN_TILE))
            w_out = bf(ev_w_out[j])
            mixed = (a_out, b_out, w_out[:LRU_WIDTH], w_out[LRU_WIDTH:])
        else:
            w_in = bf(od_w_in[j])
            kvd = SWA_KV_HEADS * HEAD_DIM
            o1, o2 = SWA_DIM, SWA_DIM + 2 * kvd
            q, kd, v, u = _odd_inproj(
                x, g, w_in[:, :o1], w_in[:, o1:o2], w_in[:, o2:],
                per_head(swa_q_norm[j], SWA_HEADS), per_head(swa_k_norm[j], SWA_KV_HEADS), cos, sin)
            c_out = _swa_attention(q, kd, _value_transpose(v, SWA_WINDOW), swa_sinks[j][None, :])
            ar, ai, bbr, bbi = _s5_prep(s5_lambda_re[j], s5_lambda_im[j], s5_log_dt[j],
                                        s5_b_re[j], s5_b_im[j])
            bmat, cmat, a_r, a_i = _s5_matrices(ar, ai, bbr, bbi, s5_c_re[j], s5_c_im[j])
            d_out = _s5(u, bmat, cmat, a_r, a_i, s5_d[j][None, :],
                        bf(s5_glu_w[j]), s5_glu_b[j][None, :])
            w_out = bf(od_w_out[j])
            mixed = (d_out, c_out, w_out[SWA_DIM:], w_out[:SWA_DIM])
        x = _mix_ffn_ple(x, *mixed, i, f2, ple)
    return x
```

```python
import numpy as np

import jax
import jax.numpy as jnp
from jax import lax
from jax.experimental import pallas as pl
from jax.experimental.pallas import tpu as pltpu

F32 = jnp.float32
BF16 = jnp.bfloat16

D_MODEL = 1024
HEAD_DIM = 64
LRU_WIDTH = 512
LRU_CONV = 4
LRU_C = 8.0
FOX_HEADS = 8
FOX_DIM = 512
SWA_HEADS = 8
SWA_KV_HEADS = 2
SWA_GROUP = SWA_HEADS // SWA_KV_HEADS
SWA_DIM = 512
SWA_WINDOW = 128
S5_WIDTH = 512
S5_GROUP = 16
S5_GROUPS = 32
S5_STATE = 64
D_FF = 2816
PLE_DIM = 256
ROPE_THETA = 10000.0
EPS = 1e-6
MACARON = 0.5
QK_SCALE = HEAD_DIM ** -0.5
LOG2E = 1.4426950408889634

SUBLANES = 8
LANES = 128
TIME_CHUNK = 64
ROW_PITCH = TIME_CHUNK + 8
ROW_TILE = SUBLANES * TIME_CHUNK
ATTN_TILE = 512
S5_LANE_GROUPS = 4
S5_CHUNKS = 2
BIAS_LANES = 6
VT_ROWS = 80
VMEM_LIMIT = 56 * 1024 * 1024


def _dot(a, b):
    return jnp.dot(a, b, preferred_element_type=F32)


def _dot_nt(a, b):
    return lax.dot_general(a, b, (((1,), (1,)), ((), ())), preferred_element_type=F32)


def _rms(x, g):
    ms = jnp.mean(x * x, axis=-1, keepdims=True)
    return x * lax.rsqrt(ms + EPS) * g


def _head_rms(x, ind, gain):
    ms = _dot((x * x).astype(BF16), ind)
    return x * lax.rsqrt(ms + EPS) * gain


def _softplus(x):
    return jnp.maximum(x, 0.0) + jnp.log1p(jnp.exp(-jnp.abs(x)))


def _log_sigmoid(x):
    return -_softplus(-x)


def _cumsum_time(x):
    n = x.shape[0]
    row = lax.broadcasted_iota(jnp.int32, x.shape, 0)
    d = SUBLANES
    while d < n:
        x = x + jnp.where(row >= d, pltpu.roll(x, d, axis=0), 0.0)
        d *= 2
    return x


def _to_time_major(val, slab_ref):
    rows, width = val.shape
    steps = rows // SUBLANES
    for s in range(width // LANES):
        for b in range(SUBLANES):
            slab_ref[s, b * ROW_PITCH:b * ROW_PITCH + steps, :] = (
                val[b * steps:(b + 1) * steps, s * LANES:(s + 1) * LANES])
    return jnp.concatenate(
        [jnp.concatenate([slab_ref[s, pl.ds(t, SUBLANES, stride=ROW_PITCH), :]
                          for s in range(width // LANES)], axis=-1) for t in range(steps)], axis=0)


def _to_batch_major(val, slab_ref):
    rows, width = val.shape
    steps = rows // SUBLANES
    for s in range(width // LANES):
        slab_ref[s, 0:rows, :] = val[:, s * LANES:(s + 1) * LANES]
    return jnp.concatenate(
        [jnp.concatenate([slab_ref[s, pl.ds(b, steps, stride=SUBLANES), :]
                          for s in range(width // LANES)], axis=-1) for b in range(SUBLANES)], axis=0)


def _split3_bf16(c):
    hi = c.astype(BF16).astype(F32)
    r = c - hi
    mid = r.astype(BF16).astype(F32)
    return hi, mid, r - mid


def _keep_lanes(x, lo, hi):
    lane = lax.broadcasted_iota(jnp.int32, x.shape, 1)
    return jnp.where((lane >= lo) & (lane < hi), x.astype(F32), 0.0).astype(BF16)


def _params(*sem):
    return pltpu.CompilerParams(dimension_semantics=sem, vmem_limit_bytes=VMEM_LIMIT)


def _const_spec(shape):
    nd = len(shape)
    return pl.BlockSpec(shape, lambda *_: (0,) * nd, pipeline_mode=pl.Buffered(1))


def _layer_spec(shape, layer):
    nd = len(shape)
    return pl.BlockSpec((None,) + tuple(shape), lambda *_: (layer,) + (0,) * nd,
                        pipeline_mode=pl.Buffered(1))


def _head_mean_matrix(width):
    h = np.arange(width) // HEAD_DIM
    return jnp.asarray((h[:, None] == h[None, :]) / HEAD_DIM, dtype=BF16)


def _swiglu_update(x, g_ref, wg_ref, wu_ref, wd_ref):
    n = _rms(x, g_ref[...]).astype(BF16)
    hg = _dot(n, wg_ref[...])
    hu = _dot(n, wu_ref[...])
    act = (hg * jax.nn.sigmoid(hg) * hu).astype(BF16)
    return x + MACARON * _dot(act, wd_ref[...])


def _ffn_kernel(x_ref, g_ref, wg_ref, wu_ref, wd_ref, o_ref):
    o_ref[...] = _swiglu_update(x_ref[...], g_ref, wg_ref, wu_ref, wd_ref)


def _ffn(x2d, layer, norm, wg, wu, wd):
    T, D = x2d.shape
    tm = min(ROW_TILE, T)
    row = pl.BlockSpec((tm, D), lambda i: (i, 0))
    return pl.pallas_call(
        _ffn_kernel, grid=(T // tm,),
        in_specs=[row, _layer_spec((1, D), layer), _layer_spec((D, D_FF), layer),
                  _layer_spec((D, D_FF), layer), _layer_spec((D_FF, D), layer)],
        out_specs=row, out_shape=jax.ShapeDtypeStruct((T, D), F32),
        compiler_params=_params("parallel"), name="ffn",
    )(x2d, norm, wg, wu, wd)


def _mix_ffn_ple_kernel(x_ref, tmaj_ref, bmaj_ref, wt_ref, wb_ref,
                        g_ref, wg_ref, wu_ref, wd_ref, p_ref, pw_ref, pn_ref, gn_ref, gw_ref, o_ref,
                        slab_ref):
    B, ts, D = x_ref.shape
    rows = B * ts
    part_t = _to_batch_major(tmaj_ref[...].astype(F32), slab_ref).astype(BF16)
    part_b = bmaj_ref[...].reshape(rows, bmaj_ref.shape[2])
    x = x_ref[...].reshape(rows, D) + _dot(part_t, wt_ref[...]) + _dot(part_b, wb_ref[...])
    x = _swiglu_update(x, g_ref, wg_ref, wu_ref, wd_ref)
    e = _rms(_dot(p_ref[...].reshape(rows, PLE_DIM).astype(BF16), pw_ref[...]), pn_ref[...])
    gate = jax.nn.sigmoid(_dot(_rms(x, gn_ref[...]).astype(BF16), gw_ref[...]))
    o_ref[...] = (x + gate * e).reshape(B, ts, D)


def _mix_ffn_ple(x, part_tmaj, part_bmaj, w_tmaj, w_bmaj, layer, ffn, ple):
    B, S, D = x.shape
    ts = TIME_CHUNK
    W = part_tmaj.shape[1]
    xs = pl.BlockSpec((B, ts, D), lambda s: (0, s, 0))
    in_specs = [xs, pl.BlockSpec((B * ts, W), lambda s: (s, 0)),
                pl.BlockSpec((B, ts, W), lambda s: (0, s, 0)),
                _const_spec((W, D)), _const_spec((W, D)),
                _layer_spec((1, D), layer), _layer_spec((D, D_FF), layer),
                _layer_spec((D, D_FF), layer), _layer_spec((D_FF, D), layer),
                pl.BlockSpec((None, B, ts, PLE_DIM), lambda s: (layer, 0, s, 0)),
                _layer_spec((PLE_DIM, D), layer), _layer_spec((1, D), layer),
                _layer_spec((1, D), layer), _layer_spec((D, D), layer)]
    return pl.pallas_call(
        _mix_ffn_ple_kernel, grid=(S // ts,), in_specs=in_specs, out_specs=xs,
        out_shape=jax.ShapeDtypeStruct((B, S, D), F32),
        scratch_shapes=[pltpu.VMEM((W // LANES, B * ts, LANES), F32)],
        compiler_params=_params("parallel"), name="mix_ffn_ple",
    )(x, part_tmaj, part_bmaj, w_tmaj, w_bmaj, *ffn, *ple)


def _bias_lane_maps():
    eq = np.zeros((LANES, LANES), np.float32)
    ek = np.zeros((LANES, LANES), np.float32)
    oq = np.zeros((1, LANES), np.float32)
    ok = np.zeros((1, LANES), np.float32)
    for h in range(FOX_HEADS):
        for i in range(3):
            eq[i * FOX_HEADS + h, BIAS_LANES * h + i] = 1.0
            ek[i * FOX_HEADS + h, BIAS_LANES * h + 3 + i] = -1.0
            oq[0, BIAS_LANES * h + 3 + i] = 1.0
            ok[0, BIAS_LANES * h + i] = 1.0
    return jnp.asarray(eq, BF16), jnp.asarray(ek, BF16), jnp.asarray(oq), jnp.asarray(ok)


def _even_in_kernel(x_ref, g_ref, wxyf_ref, wqkv_ref, bf_ref, qg_ref, kg_ref, ind_ref,
                    eqk_ref, oqk_ref,
                    xa_ref, ya_ref, q_ref, k_ref, qe_ref, ke_ref, v_ref, carry_ref, slab_ref):
    B, ts, D = x_ref.shape
    rows = B * ts

    @pl.when(pl.program_id(0) == 0)
    def _():
        carry_ref[...] = jnp.zeros_like(carry_ref)

    n = _rms(x_ref[...].reshape(rows, D), g_ref[...]).astype(BF16)
    f_cols = _dot(n, wxyf_ref[:, 2 * LRU_WIDTH:])
    xyf = jnp.concatenate([_dot(n, wxyf_ref[:, 0:2 * LRU_WIDTH]), f_cols], axis=-1)
    xyf = _to_time_major(xyf, slab_ref)
    xa_ref[...] = xyf[:, :LRU_WIDTH]
    ya_ref[...] = xyf[:, LRU_WIDTH:2 * LRU_WIDTH]

    logf = _log_sigmoid(xyf[:, 2 * LRU_WIDTH:] + bf_ref[...]) * LOG2E
    c = _cumsum_time(logf) + pltpu.repeat(carry_ref[...], ts, axis=0)
    carry_ref[...] = c[rows - SUBLANES:rows, :]
    hi, mid, lo = _split3_bf16(c)
    lane = lax.broadcasted_iota(jnp.int32, c.shape, 1)
    packed = jnp.where(lane < FOX_HEADS, hi, jnp.where(
        lane < 2 * FOX_HEADS, pltpu.roll(mid, FOX_HEADS, axis=1), jnp.where(
            lane < 3 * FOX_HEADS, pltpu.roll(lo, 2 * FOX_HEADS, axis=1), 0.0)))
    packed = _to_batch_major(packed, slab_ref).astype(BF16)
    qke = (_dot(packed, eqk_ref[...]) + oqk_ref[...]).astype(BF16)
    qe_ref[...] = qke[:, :LANES].reshape(B, ts, LANES)
    ke_ref[...] = qke[:, LANES:].reshape(B, ts, LANES)

    qkv = _dot(n, wqkv_ref[...])
    ind = ind_ref[...]
    q = _head_rms(qkv[:, :FOX_DIM], ind, qg_ref[...]) * (QK_SCALE * LOG2E)
    k = _head_rms(qkv[:, FOX_DIM:2 * FOX_DIM], ind, kg_ref[...])
    q_ref[...] = q.astype(BF16).reshape(B, ts, FOX_DIM)
    k_ref[...] = k.astype(BF16).reshape(B, ts, FOX_DIM)
    v_ref[...] = qkv[:, 2 * FOX_DIM:].astype(BF16).reshape(B, ts, FOX_DIM)


def _even_inproj(x, g, w_xy, w_qkv, w_f, b_f, qg, kg):
    B, S, D = x.shape
    ts = TIME_CHUNK
    eq, ek, oq, ok = _bias_lane_maps()
    tmaj = pl.BlockSpec((B * ts, LRU_WIDTH), lambda s: (s, 0))
    bmaj = lambda w: pl.BlockSpec((B, ts, w), lambda s: (0, s, 0))
    consts = [g, jnp.concatenate([w_xy, w_f], axis=1), w_qkv, b_f, qg, kg,
              _head_mean_matrix(FOX_DIM), jnp.concatenate([eq, ek], axis=1),
              jnp.concatenate([oq, ok], axis=1)]
    return pl.pallas_call(
        _even_in_kernel, grid=(S // ts,),
        in_specs=[bmaj(D)] + [_const_spec(a.shape) for a in consts],
        out_specs=[tmaj, tmaj, bmaj(FOX_DIM), bmaj(FOX_DIM), bmaj(LANES), bmaj(LANES),
                   bmaj(FOX_DIM)],
        out_shape=[jax.ShapeDtypeStruct((S * B, LRU_WIDTH), F32),
                   jax.ShapeDtypeStruct((S * B, LRU_WIDTH), F32),
                   jax.ShapeDtypeStruct((B, S, FOX_DIM), BF16),
                   jax.ShapeDtypeStruct((B, S, FOX_DIM), BF16),
                   jax.ShapeDtypeStruct((B, S, LANES), BF16),
                   jax.ShapeDtypeStruct((B, S, LANES), BF16),
                   jax.ShapeDtypeStruct((B, S, FOX_DIM), BF16)],
        scratch_shapes=[pltpu.VMEM((SUBLANES, LANES), F32),
                        pltpu.VMEM((2 * LRU_WIDTH // LANES + 1, SUBLANES * ROW_PITCH, LANES), F32)],
        compiler_params=_params("arbitrary"), name="even_inproj",
    )(x, *consts)


def _vt_kernel(v_ref, o_ref):
    heads, nchunk, _, chunk = o_ref.shape
    vt = v_ref[...].astype(F32).T
    ones = jnp.ones((VT_ROWS - HEAD_DIM, chunk), BF16)
    for h in range(heads):
        for c in range(nchunk):
            o_ref[h, c, 0:HEAD_DIM, :] = vt[h * HEAD_DIM:(h + 1) * HEAD_DIM,
                                            c * chunk:(c + 1) * chunk].astype(BF16)
            o_ref[h, c, HEAD_DIM:VT_ROWS, :] = ones


def _value_transpose(v, chunk):
    B, S, W = v.shape
    heads = W // HEAD_DIM
    tk = min(ATTN_TILE, S)
    chunk = min(chunk, tk)
    return pl.pallas_call(
        _vt_kernel, grid=(B, S // tk),
        in_specs=[pl.BlockSpec((None, tk, W), lambda b, j: (b, j, 0))],
        out_specs=pl.BlockSpec((None, heads, tk // chunk, VT_ROWS, chunk),
                               lambda b, j: (b, 0, j, 0, 0)),
        out_shape=jax.ShapeDtypeStruct((B, heads, S // chunk, VT_ROWS, chunk), BF16),
        compiler_params=_params("parallel", "parallel"), name="value_transpose",
    )(v)


def _lru_kernel(xa_ref, ya_ref, cw_ref, cb_ref, wa_ref, ba_ref, wx_ref, bx_ref, lam_ref, o_ref,
                xpad_ref, a_ref, h_ref, carry_ref):
    rows = xa_ref.shape[0]
    halo = (LRU_CONV - 1) * SUBLANES
    half = LRU_WIDTH // 2

    @pl.when(pl.program_id(0) == 0)
    def _():
        xpad_ref[0:halo, :] = jnp.zeros((halo, LRU_WIDTH), F32)
        carry_ref[...] = jnp.zeros_like(carry_ref)

    xpad_ref[halo:halo + rows, :] = xa_ref[...]
    xc = cb_ref[...]
    for tap in range(LRU_CONV):
        xc = xc + xpad_ref[tap * SUBLANES:tap * SUBLANES + rows, :] * cw_ref[tap:tap + 1, :]
    xpad_ref[0:halo, :] = xpad_ref[rows:rows + halo, :]

    xb = xc.astype(BF16)

    def gate(w_ref, b_ref):
        z = jnp.concatenate([_dot(xb[:, :half], w_ref[0]), _dot(xb[:, half:], w_ref[1])], axis=-1)
        return jax.nn.sigmoid(z + b_ref[...])

    r = gate(wa_ref, ba_ref)
    i = gate(wx_ref, bx_ref)
    log_a = -LRU_C * r * _softplus(lam_ref[...])
    a_ref[...] = jnp.exp(log_a)
    th = jnp.tanh(log_a)
    h_ref[...] = jnp.sqrt(-2.0 * th / (1.0 - th)) * (i * xc)

    def step(t, h):
        sl = pl.ds(pl.multiple_of(t * SUBLANES, SUBLANES), SUBLANES)
        h = a_ref[sl, :] * h + h_ref[sl, :]
        h_ref[sl, :] = h
        return h

    carry_ref[...] = lax.fori_loop(0, rows // SUBLANES, step, carry_ref[...], unroll=8)
    o_ref[...] = (jax.nn.gelu(ya_ref[...]) * h_ref[...]).astype(BF16)


def _lru(xa_t, ya_t, conv_w, conv_b, wa_bd, ba, wx_bd, bx, lam):
    R, W = xa_t.shape
    rows = TIME_CHUNK * SUBLANES
    halo = (LRU_CONV - 1) * SUBLANES
    blk = pl.BlockSpec((rows, W), lambda t: (t, 0))
    return pl.pallas_call(
        _lru_kernel, grid=(R // rows,),
        in_specs=[blk, blk, _const_spec((LRU_CONV, W)), _const_spec((1, W)),
                  _const_spec(wa_bd.shape), _const_spec((1, W)),
                  _const_spec(wx_bd.shape), _const_spec((1, W)), _const_spec((1, W))],
        out_specs=blk, out_shape=jax.ShapeDtypeStruct((R, W), BF16),
        scratch_shapes=[pltpu.VMEM((halo + rows, W), F32), pltpu.VMEM((rows, W), F32),
                        pltpu.VMEM((rows, W), F32), pltpu.VMEM((SUBLANES, W), F32)],
        compiler_params=_params("arbitrary"), name="rg_lru",
    )(xa_t, ya_t, conv_w, conv_b, wa_bd, ba, wx_bd, bx, lam)


def _fox_kernel(q_ref, qe_ref, k_ref, ke_ref, vt_ref, o_ref, acc_ref, s_ref):
    tq = q_ref.shape[0]
    qi = pl.program_id(1)
    qe = qe_ref[...]
    qq = []
    for h in range(FOX_HEADS):
        half = h % 2
        blk = q_ref[:, (h // 2) * LANES:(h // 2 + 1) * LANES]
        qq.append(jnp.concatenate(
            [_keep_lanes(blk, half * HEAD_DIM, (half + 1) * HEAD_DIM),
             _keep_lanes(qe, BIAS_LANES * h, BIAS_LANES * (h + 1))], axis=-1))
    key = lax.broadcasted_iota(jnp.int32, (tq, tq), 0)
    qry = lax.broadcasted_iota(jnp.int32, (tq, tq), 1)
    acc_ref[...] = jnp.zeros_like(acc_ref)

    def update(j, m_run, masked):
        ks = pl.ds(pl.multiple_of(j * tq, tq), tq)
        ke = ke_ref[ks, :]
        for h in range(FOX_HEADS):
            kk = jnp.concatenate([k_ref[ks, (h // 2) * LANES:(h // 2 + 1) * LANES], ke], axis=-1)
            s = _dot_nt(kk, qq[h])
            if masked:
                s = jnp.where(key <= qry, s, -jnp.inf)
            s_ref[h] = s
        new = []
        for h in range(FOX_HEADS):
            m_new = jnp.maximum(m_run[h], jnp.max(s_ref[h], axis=0, keepdims=True))
            alpha = jnp.exp2(m_run[h] - m_new)
            p = jnp.exp2(s_ref[h] - m_new).astype(BF16)
            acc_ref[h] = alpha * acc_ref[h] + _dot(vt_ref[h, j], p)
            new.append(m_new)
        return tuple(new)

    init = (jnp.full((1, tq), -jnp.inf, F32),) * FOX_HEADS
    m_run = lax.fori_loop(0, qi, lambda j, m: update(j, m, False), init)
    update(qi, m_run, True)
    for pair in range(FOX_HEADS // 2):
        out = [acc_ref[h, 0:HEAD_DIM, :] / acc_ref[h, HEAD_DIM:HEAD_DIM + 1, :]
               for h in (2 * pair, 2 * pair + 1)]
        o_ref[:, pair * LANES:(pair + 1) * LANES] = jnp.concatenate(out, axis=0).T.astype(BF16)


def _fox_attention(q, qe, k, ke, vt):
    B, S, _ = q.shape
    nk, tq = vt.shape[2], vt.shape[4]
    qblk = lambda w: pl.BlockSpec((None, tq, w), lambda b, i: (b, i, 0))
    kblk = lambda w: pl.BlockSpec((None, S, w), lambda b, i: (b, 0, 0))
    return pl.pallas_call(
        _fox_kernel, grid=(B, S // tq),
        in_specs=[qblk(FOX_DIM), qblk(LANES), kblk(FOX_DIM), kblk(LANES),
                  pl.BlockSpec((None, FOX_HEADS, nk, VT_ROWS, tq), lambda b, i: (b, 0, 0, 0, 0))],
        out_specs=qblk(FOX_DIM),
        out_shape=jax.ShapeDtypeStruct((B, S, FOX_DIM), BF16),
        scratch_shapes=[pltpu.VMEM((FOX_HEADS, VT_ROWS, tq), F32),
                        pltpu.VMEM((FOX_HEADS, tq, tq), F32)],
        compiler_params=_params("parallel", "arbitrary"), name="fox_attention",
    )(q, qe, k, ke, vt)


def _rope_table_kernel(inv_ref, cos_ref, sin_ref):
    rows = cos_ref.shape[0]
    pos = pl.program_id(0) * rows + lax.broadcasted_iota(jnp.int32, cos_ref.shape, 0)
    lane = lax.broadcasted_iota(jnp.int32, cos_ref.shape, 1)
    ang = pos.astype(F32) * inv_ref[...]
    cos_ref[...] = jnp.cos(ang)
    sin_ref[...] = jnp.where((lane & (HEAD_DIM - 1)) < HEAD_DIM // 2, -1.0, 1.0) * jnp.sin(ang)


def _rope_tables(S):
    half = HEAD_DIM // 2
    inv = jnp.power(ROPE_THETA, -jnp.arange(half, dtype=F32) / half)
    inv4 = jnp.tile(inv, LANES // half)[None, :]
    rows = min(ROW_TILE, S)
    blk = pl.BlockSpec((rows, LANES), lambda i: (i, 0))
    return pl.pallas_call(
        _rope_table_kernel, grid=(S // rows,),
        in_specs=[pl.BlockSpec((1, LANES), lambda i: (0, 0))], out_specs=[blk, blk],
        out_shape=[jax.ShapeDtypeStruct((S, LANES), F32)] * 2,
        compiler_params=_params("parallel"), name="rope_tables",
    )(inv4)


def _rope(x, cos, sin):
    half = HEAD_DIM // 2
    lane = lax.broadcasted_iota(jnp.int32, x.shape, 1)
    first = (lane & (HEAD_DIM - 1)) < half
    swapped = jnp.where(first, pltpu.roll(x, LANES - half, axis=1), pltpu.roll(x, half, axis=1))
    return x * cos + swapped * sin


def _dup_halves(x):
    lane = lax.broadcasted_iota(jnp.int32, x.shape, 1)
    r = pltpu.roll(x, HEAD_DIM, axis=1)
    lo = lane < HEAD_DIM
    return jnp.concatenate([jnp.where(lo, x, r), jnp.where(lo, r, x)], axis=-1)


def _odd_in_kernel(x_ref, g_ref, wq_ref, wkv_ref, wu_ref, qg_ref, kg_ref, ind_ref,
                   cos_ref, sin_ref, q_ref, kd_ref, v_ref, u_ref, slab_ref):
    B, ts, D = x_ref.shape
    rows = B * ts
    n = _rms(x_ref[...].reshape(rows, D), g_ref[...]).astype(BF16)
    u_ref[...] = _to_time_major(_dot(n, wu_ref[...]), slab_ref)
    cos = jnp.concatenate([cos_ref[...]] * B, axis=0)
    sin = jnp.concatenate([sin_ref[...]] * B, axis=0)
    q = _head_rms(_dot(n, wq_ref[...]), ind_ref[...], qg_ref[...])
    q = jnp.concatenate([_rope(q[:, blk * LANES:(blk + 1) * LANES], cos, sin)
                         for blk in range(SWA_DIM // LANES)], axis=-1) * (QK_SCALE * LOG2E)
    q_ref[...] = q.astype(BF16).reshape(B, ts, SWA_DIM)
    kv = _dot(n, wkv_ref[...])
    k = _head_rms(kv[:, :LANES], ind_ref[0:LANES, 0:LANES], kg_ref[...])
    kd_ref[...] = _dup_halves(_rope(k, cos, sin)).astype(BF16).reshape(B, ts, 2 * LANES)
    v_ref[...] = kv[:, LANES:].astype(BF16).reshape(B, ts, LANES)


def _odd_inproj(x, g, w_q, w_kv, w_u, qg, kg, cos, sin):
    B, S, D = x.shape
    ts = TIME_CHUNK
    bmaj = lambda w: pl.BlockSpec((B, ts, w), lambda s: (0, s, 0))
    tab = pl.BlockSpec((ts, LANES), lambda s: (s, 0))
    consts = [g, w_q, w_kv, w_u, qg, kg, _head_mean_matrix(SWA_DIM)]
    return pl.pallas_call(
        _odd_in_kernel, grid=(S // ts,),
        in_specs=[bmaj(D)] + [_const_spec(a.shape) for a in consts] + [tab, tab],
        out_specs=[bmaj(SWA_DIM), bmaj(2 * LANES), bmaj(LANES),
                   pl.BlockSpec((B * ts, S5_WIDTH), lambda s: (s, 0))],
        out_shape=[jax.ShapeDtypeStruct((B, S, SWA_DIM), BF16),
                   jax.ShapeDtypeStruct((B, S, 2 * LANES), BF16),
                   jax.ShapeDtypeStruct((B, S, LANES), BF16),
                   jax.ShapeDtypeStruct((S * B, S5_WIDTH), F32)],
        scratch_shapes=[pltpu.VMEM((S5_WIDTH // LANES, SUBLANES * ROW_PITCH, LANES), F32)],
        compiler_params=_params("parallel"), name="odd_inproj",
    )(x, *consts, cos, sin)


def _swa_kernel(q_ref, kd_ref, vt_ref, sink_ref, o_ref):
    tq = q_ref.shape[0]
    W = SWA_WINDOW
    G = SWA_GROUP
    base = pl.program_id(1) * tq
    key = lax.broadcasted_iota(jnp.int32, (2 * W, G * W), 0)
    qoff = lax.broadcasted_iota(jnp.int32, (2 * W, G * W), 1) & (W - 1)
    cgrp = lax.broadcasted_iota(jnp.int32, (1, G * W), 1) // W
    sinks = sink_ref[...] * LOG2E
    blocks = [(n, kvh) for n in range(tq // W) for kvh in range(SWA_KV_HEADS)]
    kstarts = [pl.multiple_of(jnp.maximum(base + n * W - W, 0), W) for n in range(tq // W)]
    scores = []
    for n, kvh in blocks:
        parts = []
        for g in range(G):
            head = kvh * G + g
            blk = q_ref[n * W:(n + 1) * W, (head // 2) * LANES:(head // 2 + 1) * LANES]
            half = head % 2
            parts.append(_keep_lanes(blk, half * HEAD_DIM, (half + 1) * HEAD_DIM))
        scores.append(_dot_nt(kd_ref[pl.ds(kstarts[n], 2 * W), kvh * LANES:(kvh + 1) * LANES],
                              jnp.concatenate(parts, axis=0)))
    for (n, kvh), s in zip(blocks, scores):
        rows = slice(n * W, (n + 1) * W)
        chunk = kstarts[n] // W
        diff = (base + n * W + qoff) - (kstarts[n] + key)
        s = jnp.where((diff >= 0) & (diff < W), s, -jnp.inf)
        sink = jnp.zeros((1, G * W), F32)
        for g in range(G):
            head = kvh * G + g
            sink = jnp.where(cgrp == g, sinks[:, head:head + 1], sink)
        m = jnp.maximum(jnp.max(s, axis=0, keepdims=True), sink)
        p = jnp.exp2(s - m).astype(BF16)
        acc = _dot(vt_ref[kvh, chunk], p[0:W, :]) + _dot(vt_ref[kvh, chunk + 1], p[W:2 * W, :])
        den = acc[HEAD_DIM:HEAD_DIM + 1, :] + jnp.exp2(sink - m)
        o = acc[0:HEAD_DIM, :] / den
        for pair in range(G // 2):
            both = jnp.concatenate([o[:, (2 * pair) * W:(2 * pair + 1) * W],
                                    o[:, (2 * pair + 1) * W:(2 * pair + 2) * W]], axis=0)
            lb = kvh * (G // 2) + pair
            o_ref[rows, lb * LANES:(lb + 1) * LANES] = both.T.astype(BF16)


def _swa_attention(q, kd, vt, sinks):
    B, S, _ = q.shape
    tq = min(ATTN_TILE, S)
    return pl.pallas_call(
        _swa_kernel, grid=(B, S // tq),
        in_specs=[pl.BlockSpec((None, tq, SWA_DIM), lambda b, i: (b, i, 0)),
                  pl.BlockSpec((None, S, 2 * LANES), lambda b, i: (b, 0, 0)),
                  pl.BlockSpec((None,) + vt.shape[1:], lambda b, i: (b, 0, 0, 0, 0)),
                  pl.BlockSpec((1, SWA_HEADS), lambda b, i: (0, 0))],
        out_specs=pl.BlockSpec((None, tq, SWA_DIM), lambda b, i: (b, i, 0)),
        out_shape=jax.ShapeDtypeStruct((B, S, SWA_DIM), BF16),
        compiler_params=_params("parallel", "arbitrary"), name="swa_attention",
    )(q, kd, vt, sinks)


def _s5_prep_kernel(lr_ref, li_ref, ldt_ref, br_ref, bi_ref, ar_ref, ai_ref, bbr_ref, bbi_ref):
    lr, li = lr_ref[...], li_ref[...]
    dt = jnp.exp(ldt_ref[...])
    mag = jnp.exp(lr * dt)
    ar = mag * jnp.cos(li * dt)
    ai = mag * jnp.sin(li * dt)
    den = lr * lr + li * li
    cr = ((ar - 1.0) * lr + ai * li) / den
    ci = (ai * lr - (ar - 1.0) * li) / den
    br, bi = br_ref[...], bi_ref[...]
    ar_ref[...] = ar
    ai_ref[...] = ai
    bbr_ref[...] = cr * br - ci * bi
    bbi_ref[...] = cr * bi + ci * br


def _s5_prep(lam_re, lam_im, log_dt, b_re, b_im):
    G, P, C = b_re.shape
    rep = lambda a: jnp.repeat(a, C, axis=0)
    bt = lambda a: a.transpose(0, 2, 1).reshape(G * C, P)
    ldt = jnp.broadcast_to(log_dt[:, None], (G, P))
    full = pl.BlockSpec((G * C, P), lambda: (0, 0))
    outs = pl.pallas_call(
        _s5_prep_kernel, in_specs=[full] * 5, out_specs=[full] * 4,
        out_shape=[jax.ShapeDtypeStruct((G * C, P), F32)] * 4, name="s5_prep",
    )(rep(lam_re), rep(lam_im), rep(ldt), bt(b_re), bt(b_im))
    ar, ai, bbr, bbi = [o.reshape(G, C, P) for o in outs]
    return ar[:, 0], ai[:, 0], bbr, bbi


def _s5_kernel(u_ref, bm_ref, cm_ref, ar_ref, ai_ref, d_ref, gw_ref, gb_ref, o_ref,
               h_ref, carry_ref):
    rows = u_ref.shape[0] // S5_CHUNKS
    steps = rows // SUBLANES
    half = h_ref.shape[3] // 2

    @pl.when(pl.program_id(0) == 0)
    def _():
        carry_ref[...] = jnp.zeros_like(carry_ref)

    def chunk_rows(c):
        return slice(c * rows, (c + 1) * rows)

    def input_map(c):
        ub = u_ref[chunk_rows(c), :].astype(BF16)
        for g in range(S5_LANE_GROUPS):
            h_ref[c, g] = _dot(ub[:, g * LANES:(g + 1) * LANES], bm_ref[g])

    def scan(c):
        for g in range(S5_LANE_GROUPS):
            ar = jnp.broadcast_to(ar_ref[g], (SUBLANES, half))
            ai = jnp.broadcast_to(ai_ref[g], (SUBLANES, half))
            hr = carry_ref[g, :, 0:half]
            hi = carry_ref[g, :, half:2 * half]
            for t in range(steps):
                sl = slice(t * SUBLANES, (t + 1) * SUBLANES)
                nr = ar * hr - ai * hi + h_ref[c, g, sl, 0:half]
                ni = ar * hi + ai * hr + h_ref[c, g, sl, half:2 * half]
                h_ref[c, g, sl, 0:half] = nr
                h_ref[c, g, sl, half:2 * half] = ni
                hr, hi = nr, ni
            carry_ref[g, :, 0:half] = hr
            carry_ref[g, :, half:2 * half] = hi

    def output_map(c):
        u = u_ref[chunk_rows(c), :]
        ys = [_dot(h_ref[c, g].astype(BF16), cm_ref[g]) for g in range(S5_LANE_GROUPS)]
        y = jnp.concatenate(ys, axis=-1) + d_ref[...] * u
        z = jax.nn.gelu(y)
        gate = jax.nn.sigmoid(_dot(z.astype(BF16), gw_ref[...]) + gb_ref[...])
        o_ref[chunk_rows(c), :] = (z * gate).astype(BF16)

    for c in range(S5_CHUNKS):
        input_map(c)
    for c in range(S5_CHUNKS):
        scan(c)
        output_map(c)


def _s5(u_t, bmat, cmat, ar, ai, d, glu_w, glu_b):
    R, W = u_t.shape
    rows = S5_CHUNKS * TIME_CHUNK * SUBLANES
    nstate = bmat.shape[2]
    blk = pl.BlockSpec((rows, W), lambda t: (t, 0))
    return pl.pallas_call(
        _s5_kernel, grid=(R // rows,),
        in_specs=[blk, _const_spec(bmat.shape), _const_spec(cmat.shape), _const_spec(ar.shape),
                  _const_spec(ai.shape), _const_spec((1, W)), _const_spec((W, W)),
                  _const_spec((1, W))],
        out_specs=blk, out_shape=jax.ShapeDtypeStruct((R, W), BF16),
        scratch_shapes=[pltpu.VMEM((S5_CHUNKS, S5_LANE_GROUPS, rows // S5_CHUNKS, nstate), F32),
                        pltpu.VMEM((S5_LANE_GROUPS, SUBLANES, nstate), F32)],
        compiler_params=_params("arbitrary"), name="s5_glu",
    )(u_t, bmat, cmat, ar, ai, d, glu_w, glu_b)


def _s5_matrices(ar, ai, bbr, bbi, c_re, c_im):
    L, GL = S5_LANE_GROUPS, S5_GROUPS // S5_LANE_GROUPS
    C, P = S5_GROUP, S5_STATE
    eye = jnp.eye(GL, dtype=F32)

    def inmap(b):
        return jnp.einsum("lgcp,gh->lgchp", b.reshape(L, GL, C, P), eye).reshape(L, GL * C, GL * P)

    def outmap(c):
        return jnp.einsum("lgcp,gh->lgphc", c.reshape(L, GL, C, P), eye).reshape(L, GL * P, GL * C)

    bmat = jnp.concatenate([inmap(bbr), inmap(bbi)], axis=2).astype(BF16)
    cmat = jnp.concatenate([outmap(c_re), outmap(-c_im)], axis=1).astype(BF16)
    a_r = ar.reshape(L, 1, GL * P)
    a_i = ai.reshape(L, 1, GL * P)
    return bmat, cmat, a_r, a_i


def _block_diag_pairs(w):
    nb, bs, _ = w.shape
    half = nb // 2
    eye = jnp.eye(half, dtype=w.dtype)
    out = jnp.einsum("thij,hk->thikj", w.reshape(2, half, bs, bs), eye)
    return out.reshape(2, half * bs, half * bs).astype(BF16)


def kernel(x, p, ffn1_norm, ffn1_wg, ffn1_wu, ffn1_wd, mix_norm, ffn2_norm, ffn2_wg, ffn2_wu, ffn2_wd, ple_w, ple_norm, ple_gate_norm, ple_gate_w, ev_w_in, lru_conv_w, lru_conv_b, lru_wa, lru_ba, lru_wx, lru_bx, lru_lambda, fox_bf, fox_q_norm, fox_k_norm, ev_w_out, od_w_in, swa_q_norm, swa_k_norm, swa_sinks, s5_lambda_re, s5_lambda_im, s5_log_dt, s5_b_re, s5_b_im, s5_c_re, s5_c_im, s5_d, s5_glu_w, s5_glu_b, od_w_out):
    B, S, D = x.shape
    depth = p.shape[0]
    assert B == SUBLANES and D == D_MODEL and S % (S5_CHUNKS * TIME_CHUNK) == 0
    T = B * S
    bf = lambda a: a.astype(BF16)
    row = lambda a: a[:, None, :]
    per_head = lambda gain, heads: jnp.tile(gain, heads)[None, :]

    f1 = (row(ffn1_norm), bf(ffn1_wg), bf(ffn1_wu), bf(ffn1_wd))
    f2 = (row(ffn2_norm), bf(ffn2_wg), bf(ffn2_wu), bf(ffn2_wd))
    ple = (p, bf(ple_w), row(ple_norm), row(ple_gate_norm), bf(ple_gate_w))
    cos, sin = _rope_tables(S)

    for i in range(depth):
        j = i // 2
        x = _ffn(x.reshape(T, D), i, *f1).reshape(B, S, D)
        g = mix_norm[i][None, :]
        if i % 2 == 0:
            w_in = bf(ev_w_in[j])
            o1, o2 = 2 * LRU_WIDTH, 2 * LRU_WIDTH + 3 * FOX_DIM
            w_f = jnp.pad(w_in[:, o2:], ((0, 0), (0, LANES - FOX_HEADS)))
            b_f = jnp.pad(fox_bf[j], (0, LANES - FOX_HEADS))[None, :]
            xa, ya, q, k, qe, ke, v = _even_inproj(
                x, g, w_in[:, :o1], w_in[:, o1:o2], w_f, b_f,
                per_head(fox_q_norm[j], FOX_HEADS), per_head(fox_k_norm[j], FOX_HEADS))
            a_out = _lru(xa, ya, lru_conv_w[j], lru_conv_b[j][None, :],
                         _block_diag_pairs(lru_wa[j]), lru_ba[j][None, :],
                         _block_diag_pairs(lru_wx[j]), lru_bx[j][None, :],
                         lru_lambda[j][None, :])
            b_out = _fox_attention(q, qe, k, ke, _value_transpose(v, ATTN_TILE))
            w_out = bf(ev_w_out[j])
            mixed = (a_out, b_out, w_out[:LRU_WIDTH], w_out[LRU_WIDTH:])
        else:
            w_in = bf(od_w_in[j])
            kvd = SWA_KV_HEADS * HEAD_DIM
            o1, o2 = SWA_DIM, SWA_DIM + 2 * kvd
            q, kd, v, u = _odd_inproj(
                x, g, w_in[:, :o1], w_in[:, o1:o2], w_in[:, o2:],
                per_head(swa_q_norm[j], SWA_HEADS), per_head(swa_k_norm[j], SWA_KV_HEADS), cos, sin)
            c_out = _swa_attention(q, kd, _value_transpose(v, SWA_WINDOW), swa_sinks[j][None, :])
            ar, ai, bbr, bbi = _s5_prep(s5_lambda_re[j], s5_lambda_im[j], s5_log_dt[j],
                                        s5_b_re[j], s5_b_im[j])
            bmat, cmat, a_r, a_i = _s5_matrices(ar, ai, bbr, bbi, s5_c_re[j], s5_c_im[j])
            d_out = _s5(u, bmat, cmat, a_r, a_i, s5_d[j][None, :],
                        bf(s5_glu_w[j]), s5_glu_b[j][None, :])
            w_out = bf(od_w_out[j])
            mixed = (d_out, c_out, w_out[SWA_DIM:], w_out[:SWA_DIM])
        x = _mix_ffn_ple(x, *mixed, i, f2, ple)
    return x
```

```python
import numpy as np

import jax
import jax.numpy as jnp
from jax import lax
from jax.experimental import pallas as pl
from jax.experimental.pallas import tpu as pltpu

F32 = jnp.float32
BF16 = jnp.bfloat16

D_MODEL = 1024
HEAD_DIM = 64
LRU_WIDTH = 512
LRU_CONV = 4
LRU_C = 8.0
FOX_HEADS = 8
FOX_DIM = 512
SWA_HEADS = 8
SWA_KV_HEADS = 2
SWA_GROUP = SWA_HEADS // SWA_KV_HEADS
SWA_DIM = 512
SWA_WINDOW = 128
S5_WIDTH = 512
S5_GROUP = 16
S5_GROUPS = 32
S5_STATE = 64
D_FF = 2816
PLE_DIM = 256
ROPE_THETA = 10000.0
EPS = 1e-6
MACARON = 0.5
QK_SCALE = HEAD_DIM ** -0.5
LOG2E = 1.4426950408889634

SUBLANES = 8
LANES = 128
TIME_CHUNK = 64
ROW_PITCH = TIME_CHUNK + 8
ROW_TILE = SUBLANES * TIME_CHUNK
ATTN_TILE = 512
S5_LANE_GROUPS = 4
S5_CHUNKS = 2
BIAS_LANES = 6
VT_ROWS = 80
VMEM_LIMIT = 56 * 1024 * 1024


def _dot(a, b):
    return jnp.dot(a, b, preferred_element_type=F32)


def _dot_nt(a, b):
    return lax.dot_general(a, b, (((1,), (1,)), ((), ())), preferred_element_type=F32)


def _rms(x, g):
    ms = jnp.mean(x * x, axis=-1, keepdims=True)
    return x * lax.rsqrt(ms + EPS) * g


def _head_rms(x, ind, gain):
    ms = _dot((x * x).astype(BF16), ind)
    return x * lax.rsqrt(ms + EPS) * gain


def _softplus(x):
    return jnp.maximum(x, 0.0) + jnp.log1p(jnp.exp(-jnp.abs(x)))


def _log_sigmoid(x):
    return -_softplus(-x)


def _cumsum_time(x):
    n = x.shape[0]
    row = lax.broadcasted_iota(jnp.int32, x.shape, 0)
    d = SUBLANES
    while d < n:
        x = x + jnp.where(row >= d, pltpu.roll(x, d, axis=0), 0.0)
        d *= 2
    return x


def _to_time_major(val, slab_ref):
    rows, width = val.shape
    steps = rows // SUBLANES
    for s in range(width // LANES):
        for b in range(SUBLANES):
            slab_ref[s, b * ROW_PITCH:b * ROW_PITCH + steps, :] = (
                val[b * steps:(b + 1) * steps, s * LANES:(s + 1) * LANES])
    return jnp.concatenate(
        [jnp.concatenate([slab_ref[s, pl.ds(t, SUBLANES, stride=ROW_PITCH), :]
                          for s in range(width // LANES)], axis=-1) for t in range(steps)], axis=0)


def _to_batch_major(val, slab_ref):
    rows, width = val.shape
    steps = rows // SUBLANES
    for s in range(width // LANES):
        slab_ref[s, 0:rows, :] = val[:, s * LANES:(s + 1) * LANES]
    return jnp.concatenate(
        [jnp.concatenate([slab_ref[s, pl.ds(b, steps, stride=SUBLANES), :]
                          for s in range(width // LANES)], axis=-1) for b in range(SUBLANES)], axis=0)


def _split3_bf16(c):
    hi = c.astype(BF16).astype(F32)
    r = c - hi
    mid = r.astype(BF16).astype(F32)
    return hi, mid, r - mid


def _keep_lanes(x, lo, hi):
    lane = lax.broadcasted_iota(jnp.int32, x.shape, 1)
    return jnp.where((lane >= lo) & (lane < hi), x.astype(F32), 0.0).astype(BF16)


def _params(*sem):
    return pltpu.CompilerParams(dimension_semantics=sem, vmem_limit_bytes=VMEM_LIMIT)


def _const_spec(shape):
    nd = len(shape)
    return pl.BlockSpec(shape, lambda *_: (0,) * nd, pipeline_mode=pl.Buffered(1))


def _layer_spec(shape, layer):
    nd = len(shape)
    return pl.BlockSpec((None,) + tuple(shape), lambda *_: (layer,) + (0,) * nd,
                        pipeline_mode=pl.Buffered(1))


def _head_mean_matrix(width):
    h = np.arange(width) // HEAD_DIM
    return jnp.asarray((h[:, None] == h[None, :]) / HEAD_DIM, dtype=BF16)


def _swiglu_update(x, g_ref, wg_ref, wu_ref, wd_ref):
    n = _rms(x, g_ref[...]).astype(BF16)
    hg = _dot(n, wg_ref[...])
    hu = _dot(n, wu_ref[...])
    act = (hg * jax.nn.sigmoid(hg) * hu).astype(BF16)
    return x + MACARON * _dot(act, wd_ref[...])


def _ffn_kernel(x_ref, g_ref, wg_ref, wu_ref, wd_ref, o_ref):
    half = x_ref.shape[0] // 2
    for i in range(2):
        rs = slice(i * half, (i + 1) * half)
        o_ref[rs, :] = _swiglu_update(x_ref[rs, :], g_ref, wg_ref, wu_ref, wd_ref)


def _ffn(x2d, layer, norm, wg, wu, wd):
    T, D = x2d.shape
    tm = min(ROW_TILE, T)
    row = pl.BlockSpec((tm, D), lambda i: (i, 0))
    return pl.pallas_call(
        _ffn_kernel, grid=(T // tm,),
        in_specs=[row, _layer_spec((1, D), layer), _layer_spec((D, D_FF), layer),
                  _layer_spec((D, D_FF), layer), _layer_spec((D_FF, D), layer)],
        out_specs=row, out_shape=jax.ShapeDtypeStruct((T, D), F32),
        compiler_params=_params("parallel"), name="ffn",
    )(x2d, norm, wg, wu, wd)


def _mix_ffn_ple_kernel(x_ref, tmaj_ref, bmaj_ref, wt_ref, wb_ref,
                        g_ref, wg_ref, wu_ref, wd_ref, p_ref, pw_ref, pn_ref, gn_ref, gw_ref, o_ref,
                        slab_ref):
    B, ts, D = x_ref.shape
    rows = B * ts
    part_t = _to_batch_major(tmaj_ref[...].astype(F32), slab_ref).astype(BF16)
    part_b = bmaj_ref[...].reshape(rows, bmaj_ref.shape[2])
    x = x_ref[...].reshape(rows, D) + _dot(part_t, wt_ref[...]) + _dot(part_b, wb_ref[...])
    x = _swiglu_update(x, g_ref, wg_ref, wu_ref, wd_ref)
    e = _rms(_dot(p_ref[...].reshape(rows, PLE_DIM).astype(BF16), pw_ref[...]), pn_ref[...])
    gate = jax.nn.sigmoid(_dot(_rms(x, gn_ref[...]).astype(BF16), gw_ref[...]))
    o_ref[...] = (x + gate * e).reshape(B, ts, D)


def _mix_ffn_ple(x, part_tmaj, part_bmaj, w_tmaj, w_bmaj, layer, ffn, ple):
    B, S, D = x.shape
    ts = TIME_CHUNK
    W = part_tmaj.shape[1]
    xs = pl.BlockSpec((B, ts, D), lambda s: (0, s, 0))
    in_specs = [xs, pl.BlockSpec((B * ts, W), lambda s: (s, 0)),
                pl.BlockSpec((B, ts, W), lambda s: (0, s, 0)),
                _const_spec((W, D)), _const_spec((W, D)),
                _layer_spec((1, D), layer), _layer_spec((D, D_FF), layer),
                _layer_spec((D, D_FF), layer), _layer_spec((D_FF, D), layer),
                pl.BlockSpec((None, B, ts, PLE_DIM), lambda s: (layer, 0, s, 0)),
                _layer_spec((PLE_DIM, D), layer), _layer_spec((1, D), layer),
                _layer_spec((1, D), layer), _layer_spec((D, D), layer)]
    return pl.pallas_call(
        _mix_ffn_ple_kernel, grid=(S // ts,), in_specs=in_specs, out_specs=xs,
        out_shape=jax.ShapeDtypeStruct((B, S, D), F32),
        scratch_shapes=[pltpu.VMEM((W // LANES, B * ts, LANES), F32)],
        compiler_params=_params("parallel"), name="mix_ffn_ple",
    )(x, part_tmaj, part_bmaj, w_tmaj, w_bmaj, *ffn, *ple)


def _bias_lane_maps():
    eq = np.zeros((LANES, LANES), np.float32)
    ek = np.zeros((LANES, LANES), np.float32)
    oq = np.zeros((1, LANES), np.float32)
    ok = np.zeros((1, LANES), np.float32)
    for h in range(FOX_HEADS):
        for i in range(3):
            eq[i * FOX_HEADS + h, BIAS_LANES * h + i] = 1.0
            ek[i * FOX_HEADS + h, BIAS_LANES * h + 3 + i] = -1.0
            oq[0, BIAS_LANES * h + 3 + i] = 1.0
            ok[0, BIAS_LANES * h + i] = 1.0
    return jnp.asarray(eq, BF16), jnp.asarray(ek, BF16), jnp.asarray(oq), jnp.asarray(ok)


def _rg_lru_chunk(xa, ya, cw_ref, cb_ref, wa_ref, ba_ref, wx_ref, bx_ref, lam_ref, xprev_ref, h_ref):
    rows = xa.shape[0]
    halo = (LRU_CONV - 1) * SUBLANES
    half = LRU_WIDTH // 2
    xfull = jnp.concatenate([xprev_ref[...], xa], axis=0)
    xprev_ref[...] = xa[rows - halo:rows, :]
    xc = cb_ref[...]
    for tap in range(LRU_CONV):
        xc = xc + xfull[tap * SUBLANES:tap * SUBLANES + rows, :] * cw_ref[tap:tap + 1, :]
    xb = xc.astype(BF16)

    def gate(w_ref, b_ref):
        z = jnp.concatenate([_dot(xb[:, :half], w_ref[0]), _dot(xb[:, half:], w_ref[1])], axis=-1)
        return jax.nn.sigmoid(z + b_ref[...])

    r = gate(wa_ref, ba_ref)
    i = gate(wx_ref, bx_ref)
    log_a = -LRU_C * r * _softplus(lam_ref[...])
    a = jnp.exp(log_a)
    th = jnp.tanh(log_a)
    b = jnp.sqrt(-2.0 * th / (1.0 - th)) * (i * xc)
    h = h_ref[...]
    hs = []
    for t in range(rows // SUBLANES):
        sl = slice(t * SUBLANES, (t + 1) * SUBLANES)
        h = a[sl, :] * h + b[sl, :]
        hs.append(h)
    h_ref[...] = h
    return jax.nn.gelu(ya) * jnp.concatenate(hs, axis=0)


def _even_in_kernel(x_ref, g_ref, wxyf_ref, wqkv_ref, bf_ref, qg_ref, kg_ref, ind_ref,
                    eqk_ref, oqk_ref, cw_ref, cb_ref, wa_ref, ba_ref, wx_ref, bx_ref, lam_ref,
                    a_out_ref, q_ref, k_ref, qe_ref, ke_ref, v_ref,
                    carry_ref, slab_ref, xprev_ref, h_ref):
    B, ts, D = x_ref.shape
    rows = B * ts

    @pl.when(pl.program_id(0) == 0)
    def _():
        carry_ref[...] = jnp.zeros_like(carry_ref)
        xprev_ref[...] = jnp.zeros_like(xprev_ref)
        h_ref[...] = jnp.zeros_like(h_ref)

    n = _rms(x_ref[...].reshape(rows, D), g_ref[...]).astype(BF16)
    f_cols = _dot(n, wxyf_ref[:, 2 * LRU_WIDTH:])
    xyf = jnp.concatenate([_dot(n, wxyf_ref[:, 0:2 * LRU_WIDTH]), f_cols], axis=-1)
    xyf = _to_time_major(xyf, slab_ref)
    a_out_ref[...] = _rg_lru_chunk(xyf[:, :LRU_WIDTH], xyf[:, LRU_WIDTH:2 * LRU_WIDTH], cw_ref, cb_ref,
                                   wa_ref, ba_ref, wx_ref, bx_ref, lam_ref, xprev_ref,
                                   h_ref).astype(BF16)

    logf = _log_sigmoid(xyf[:, 2 * LRU_WIDTH:] + bf_ref[...]) * LOG2E
    c = _cumsum_time(logf) + pltpu.repeat(carry_ref[...], ts, axis=0)
    carry_ref[...] = c[rows - SUBLANES:rows, :]
    hi, mid, lo = _split3_bf16(c)
    lane = lax.broadcasted_iota(jnp.int32, c.shape, 1)
    packed = jnp.where(lane < FOX_HEADS, hi, jnp.where(
        lane < 2 * FOX_HEADS, pltpu.roll(mid, FOX_HEADS, axis=1), jnp.where(
            lane < 3 * FOX_HEADS, pltpu.roll(lo, 2 * FOX_HEADS, axis=1), 0.0)))
    packed = _to_batch_major(packed, slab_ref).astype(BF16)
    qke = (_dot(packed, eqk_ref[...]) + oqk_ref[...]).astype(BF16)
    qe_ref[...] = qke[:, :LANES].reshape(B, ts, LANES)
    ke_ref[...] = qke[:, LANES:].reshape(B, ts, LANES)

    qkv = _dot(n, wqkv_ref[...])
    ind = ind_ref[...]
    q = _head_rms(qkv[:, :FOX_DIM], ind, qg_ref[...]) * (QK_SCALE * LOG2E)
    k = _head_rms(qkv[:, FOX_DIM:2 * FOX_DIM], ind, kg_ref[...])
    q_ref[...] = q.astype(BF16).reshape(B, ts, FOX_DIM)
    k_ref[...] = k.astype(BF16).reshape(B, ts, FOX_DIM)
    v_ref[...] = qkv[:, 2 * FOX_DIM:].astype(BF16).reshape(B, ts, FOX_DIM)


def _even_inproj(x, g, w_xy, w_qkv, w_f, b_f, qg, kg, lru):
    B, S, D = x.shape
    ts = TIME_CHUNK
    eq, ek, oq, ok = _bias_lane_maps()
    tmaj = pl.BlockSpec((B * ts, LRU_WIDTH), lambda s: (s, 0))
    bmaj = lambda w: pl.BlockSpec((B, ts, w), lambda s: (0, s, 0))
    consts = [g, jnp.concatenate([w_xy, w_f], axis=1), w_qkv, b_f, qg, kg,
              _head_mean_matrix(FOX_DIM), jnp.concatenate([eq, ek], axis=1),
              jnp.concatenate([oq, ok], axis=1), *lru]
    return pl.pallas_call(
        _even_in_kernel, grid=(S // ts,),
        in_specs=[bmaj(D)] + [_const_spec(a.shape) for a in consts],
        out_specs=[tmaj, bmaj(FOX_DIM), bmaj(FOX_DIM), bmaj(LANES), bmaj(LANES), bmaj(FOX_DIM)],
        out_shape=[jax.ShapeDtypeStruct((S * B, LRU_WIDTH), BF16),
                   jax.ShapeDtypeStruct((B, S, FOX_DIM), BF16),
                   jax.ShapeDtypeStruct((B, S, FOX_DIM), BF16),
                   jax.ShapeDtypeStruct((B, S, LANES), BF16),
                   jax.ShapeDtypeStruct((B, S, LANES), BF16),
                   jax.ShapeDtypeStruct((B, S, FOX_DIM), BF16)],
        scratch_shapes=[pltpu.VMEM((SUBLANES, LANES), F32),
                        pltpu.VMEM((2 * LRU_WIDTH // LANES + 1, SUBLANES * ROW_PITCH, LANES), F32),
                        pltpu.VMEM(((LRU_CONV - 1) * SUBLANES, LRU_WIDTH), F32),
                        pltpu.VMEM((SUBLANES, LRU_WIDTH), F32)],
        compiler_params=_params("arbitrary"), name="even_inproj",
    )(x, *consts)


def _vt_kernel(v_ref, o_ref):
    heads, nchunk, _, chunk = o_ref.shape
    vt = v_ref[...].astype(F32).T
    ones = jnp.ones((VT_ROWS - HEAD_DIM, chunk), BF16)
    for h in range(heads):
        for c in range(nchunk):
            o_ref[h, c, 0:HEAD_DIM, :] = vt[h * HEAD_DIM:(h + 1) * HEAD_DIM,
                                            c * chunk:(c + 1) * chunk].astype(BF16)
            o_ref[h, c, HEAD_DIM:VT_ROWS, :] = ones


def _value_transpose(v, chunk):
    B, S, W = v.shape
    heads = W // HEAD_DIM
    tk = min(ATTN_TILE, S)
    chunk = min(chunk, tk)
    return pl.pallas_call(
        _vt_kernel, grid=(B, S // tk),
        in_specs=[pl.BlockSpec((None, tk, W), lambda b, j: (b, j, 0))],
        out_specs=pl.BlockSpec((None, heads, tk // chunk, VT_ROWS, chunk),
                               lambda b, j: (b, 0, j, 0, 0)),
        out_shape=jax.ShapeDtypeStruct((B, heads, S // chunk, VT_ROWS, chunk), BF16),
        compiler_params=_params("parallel", "parallel"), name="value_transpose",
    )(v)


def _fox_kernel(q_ref, qe_ref, k_ref, ke_ref, vt_ref, o_ref, acc_ref, s_ref):
    tq = q_ref.shape[0]
    qi = pl.program_id(1)
    qe = qe_ref[...]
    qq = []
    for h in range(FOX_HEADS):
        half = h % 2
        blk = q_ref[:, (h // 2) * LANES:(h // 2 + 1) * LANES]
        qq.append(jnp.concatenate(
            [_keep_lanes(blk, half * HEAD_DIM, (half + 1) * HEAD_DIM),
             _keep_lanes(qe, BIAS_LANES * h, BIAS_LANES * (h + 1))], axis=-1))
    key = lax.broadcasted_iota(jnp.int32, (tq, tq), 0)
    qry = lax.broadcasted_iota(jnp.int32, (tq, tq), 1)
    acc_ref[...] = jnp.zeros_like(acc_ref)

    def update(j, m_run, masked):
        ks = pl.ds(pl.multiple_of(j * tq, tq), tq)
        ke = ke_ref[ks, :]
        for h in range(FOX_HEADS):
            kk = jnp.concatenate([k_ref[ks, (h // 2) * LANES:(h // 2 + 1) * LANES], ke], axis=-1)
            s = _dot_nt(kk, qq[h])
            if masked:
                s = jnp.where(key <= qry, s, -jnp.inf)
            s_ref[h] = s
        new = []
        for h in range(FOX_HEADS):
            m_new = jnp.maximum(m_run[h], jnp.max(s_ref[h], axis=0, keepdims=True))
            alpha = jnp.exp2(m_run[h] - m_new)
            p = jnp.exp2(s_ref[h] - m_new).astype(BF16)
            acc_ref[h] = alpha * acc_ref[h] + _dot(vt_ref[h, j], p)
            new.append(m_new)
        return tuple(new)

    init = (jnp.full((1, tq), -jnp.inf, F32),) * FOX_HEADS
    m_run = lax.fori_loop(0, qi, lambda j, m: update(j, m, False), init)
    update(qi, m_run, True)
    for pair in range(FOX_HEADS // 2):
        out = [acc_ref[h, 0:HEAD_DIM, :] / acc_ref[h, HEAD_DIM:HEAD_DIM + 1, :]
               for h in (2 * pair, 2 * pair + 1)]
        o_ref[:, pair * LANES:(pair + 1) * LANES] = jnp.concatenate(out, axis=0).T.astype(BF16)


def _fox_attention(q, qe, k, ke, vt):
    B, S, _ = q.shape
    nk, tq = vt.shape[2], vt.shape[4]
    qblk = lambda w: pl.BlockSpec((None, tq, w), lambda b, i: (b, i, 0))
    kblk = lambda w: pl.BlockSpec((None, S, w), lambda b, i: (b, 0, 0))
    return pl.pallas_call(
        _fox_kernel, grid=(B, S // tq),
        in_specs=[qblk(FOX_DIM), qblk(LANES), kblk(FOX_DIM), kblk(LANES),
                  pl.BlockSpec((None, FOX_HEADS, nk, VT_ROWS, tq), lambda b, i: (b, 0, 0, 0, 0))],
        out_specs=qblk(FOX_DIM),
        out_shape=jax.ShapeDtypeStruct((B, S, FOX_DIM), BF16),
        scratch_shapes=[pltpu.VMEM((FOX_HEADS, VT_ROWS, tq), F32),
                        pltpu.VMEM((FOX_HEADS, tq, tq), F32)],
        compiler_params=_params("parallel", "arbitrary"), name="fox_attention",
    )(q, qe, k, ke, vt)


def _rope_table_kernel(inv_ref, cos_ref, sin_ref):
    rows = cos_ref.shape[0]
    pos = pl.program_id(0) * rows + lax.broadcasted_iota(jnp.int32, cos_ref.shape, 0)
    lane = lax.broadcasted_iota(jnp.int32, cos_ref.shape, 1)
    ang = pos.astype(F32) * inv_ref[...]
    cos_ref[...] = jnp.cos(ang)
    sin_ref[...] = jnp.where((lane & (HEAD_DIM - 1)) < HEAD_DIM // 2, -1.0, 1.0) * jnp.sin(ang)


def _rope_tables(S):
    half = HEAD_DIM // 2
    inv = jnp.power(ROPE_THETA, -jnp.arange(half, dtype=F32) / half)
    inv4 = jnp.tile(inv, LANES // half)[None, :]
    rows = min(ROW_TILE, S)
    blk = pl.BlockSpec((rows, LANES), lambda i: (i, 0))
    return pl.pallas_call(
        _rope_table_kernel, grid=(S // rows,),
        in_specs=[pl.BlockSpec((1, LANES), lambda i: (0, 0))], out_specs=[blk, blk],
        out_shape=[jax.ShapeDtypeStruct((S, LANES), F32)] * 2,
        compiler_params=_params("parallel"), name="rope_tables",
    )(inv4)


def _rope(x, cos, sin):
    half = HEAD_DIM // 2
    lane = lax.broadcasted_iota(jnp.int32, x.shape, 1)
    first = (lane & (HEAD_DIM - 1)) < half
    swapped = jnp.where(first, pltpu.roll(x, LANES - half, axis=1), pltpu.roll(x, half, axis=1))
    return x * cos + swapped * sin


def _dup_halves(x):
    lane = lax.broadcasted_iota(jnp.int32, x.shape, 1)
    r = pltpu.roll(x, HEAD_DIM, axis=1)
    lo = lane < HEAD_DIM
    return jnp.concatenate([jnp.where(lo, x, r), jnp.where(lo, r, x)], axis=-1)


def _odd_in_kernel(x_ref, g_ref, wq_ref, wkv_ref, wu_ref, qg_ref, kg_ref, ind_ref,
                   cos_ref, sin_ref, q_ref, kd_ref, v_ref, u_ref, slab_ref):
    B, ts, D = x_ref.shape
    rows = B * ts
    n = _rms(x_ref[...].reshape(rows, D), g_ref[...]).astype(BF16)
    u_ref[...] = _to_time_major(_dot(n, wu_ref[...]), slab_ref)
    cos = jnp.concatenate([cos_ref[...]] * B, axis=0)
    sin = jnp.concatenate([sin_ref[...]] * B, axis=0)
    q = _head_rms(_dot(n, wq_ref[...]), ind_ref[...], qg_ref[...])
    q = jnp.concatenate([_rope(q[:, blk * LANES:(blk + 1) * LANES], cos, sin)
                         for blk in range(SWA_DIM // LANES)], axis=-1) * (QK_SCALE * LOG2E)
    q_ref[...] = q.astype(BF16).reshape(B, ts, SWA_DIM)
    kv = _dot(n, wkv_ref[...])
    k = _head_rms(kv[:, :LANES], ind_ref[0:LANES, 0:LANES], kg_ref[...])
    kd_ref[...] = _dup_halves(_rope(k, cos, sin)).astype(BF16).reshape(B, ts, 2 * LANES)
    v_ref[...] = kv[:, LANES:].astype(BF16).reshape(B, ts, LANES)


def _odd_inproj(x, g, w_q, w_kv, w_u, qg, kg, cos, sin):
    B, S, D = x.shape
    ts = TIME_CHUNK
    bmaj = lambda w: pl.BlockSpec((B, ts, w), lambda s: (0, s, 0))
    tab = pl.BlockSpec((ts, LANES), lambda s: (s, 0))
    consts = [g, w_q, w_kv, w_u, qg, kg, _head_mean_matrix(SWA_DIM)]
    return pl.pallas_call(
        _odd_in_kernel, grid=(S // ts,),
        in_specs=[bmaj(D)] + [_const_spec(a.shape) for a in consts] + [tab, tab],
        out_specs=[bmaj(SWA_DIM), bmaj(2 * LANES), bmaj(LANES),
                   pl.BlockSpec((B * ts, S5_WIDTH), lambda s: (s, 0))],
        out_shape=[jax.ShapeDtypeStruct((B, S, SWA_DIM), BF16),
                   jax.ShapeDtypeStruct((B, S, 2 * LANES), BF16),
                   jax.ShapeDtypeStruct((B, S, LANES), BF16),
                   jax.ShapeDtypeStruct((S * B, S5_WIDTH), F32)],
        scratch_shapes=[pltpu.VMEM((S5_WIDTH // LANES, SUBLANES * ROW_PITCH, LANES), F32)],
        compiler_params=_params("parallel"), name="odd_inproj",
    )(x, *consts, cos, sin)


def _swa_kernel(q_ref, kd_ref, vt_ref, sink_ref, o_ref):
    tq = q_ref.shape[0]
    W = SWA_WINDOW
    G = SWA_GROUP
    base = pl.program_id(1) * tq
    key = lax.broadcasted_iota(jnp.int32, (2 * W, G * W), 0)
    qoff = lax.broadcasted_iota(jnp.int32, (2 * W, G * W), 1) & (W - 1)
    cgrp = lax.broadcasted_iota(jnp.int32, (1, G * W), 1) // W
    sinks = sink_ref[...] * LOG2E
    blocks = [(n, kvh) for n in range(tq // W) for kvh in range(SWA_KV_HEADS)]
    kstarts = [pl.multiple_of(jnp.maximum(base + n * W - W, 0), W) for n in range(tq // W)]
    scores = []
    for n, kvh in blocks:
        parts = []
        for g in range(G):
            head = kvh * G + g
            blk = q_ref[n * W:(n + 1) * W, (head // 2) * LANES:(head // 2 + 1) * LANES]
            half = head % 2
            parts.append(_keep_lanes(blk, half * HEAD_DIM, (half + 1) * HEAD_DIM))
        scores.append(_dot_nt(kd_ref[pl.ds(kstarts[n], 2 * W), kvh * LANES:(kvh + 1) * LANES],
                              jnp.concatenate(parts, axis=0)))
    for (n, kvh), s in zip(blocks, scores):
        rows = slice(n * W, (n + 1) * W)
        chunk = kstarts[n] // W
        diff = (base + n * W + qoff) - (kstarts[n] + key)
        s = jnp.where((diff >= 0) & (diff < W), s, -jnp.inf)
        sink = jnp.zeros((1, G * W), F32)
        for g in range(G):
            head = kvh * G + g
            sink = jnp.where(cgrp == g, sinks[:, head:head + 1], sink)
        m = jnp.maximum(jnp.max(s, axis=0, keepdims=True), sink)
        p = jnp.exp2(s - m).astype(BF16)
        acc = _dot(vt_ref[kvh, chunk], p[0:W, :]) + _dot(vt_ref[kvh, chunk + 1], p[W:2 * W, :])
        den = acc[HEAD_DIM:HEAD_DIM + 1, :] + jnp.exp2(sink - m)
        o = acc[0:HEAD_DIM, :] / den
        for pair in range(G // 2):
            both = jnp.concatenate([o[:, (2 * pair) * W:(2 * pair + 1) * W],
                                    o[:, (2 * pair + 1) * W:(2 * pair + 2) * W]], axis=0)
            lb = kvh * (G // 2) + pair
            o_ref[rows, lb * LANES:(lb + 1) * LANES] = both.T.astype(BF16)


def _swa_attention(q, kd, vt, sinks):
    B, S, _ = q.shape
    tq = min(ATTN_TILE, S)
    return pl.pallas_call(
        _swa_kernel, grid=(B, S // tq),
        in_specs=[pl.BlockSpec((None, tq, SWA_DIM), lambda b, i: (b, i, 0)),
                  pl.BlockSpec((None, S, 2 * LANES), lambda b, i: (b, 0, 0)),
                  pl.BlockSpec((None,) + vt.shape[1:], lambda b, i: (b, 0, 0, 0, 0)),
                  pl.BlockSpec((1, SWA_HEADS), lambda b, i: (0, 0))],
        out_specs=pl.BlockSpec((None, tq, SWA_DIM), lambda b, i: (b, i, 0)),
        out_shape=jax.ShapeDtypeStruct((B, S, SWA_DIM), BF16),
        compiler_params=_params("parallel", "arbitrary"), name="swa_attention",
    )(q, kd, vt, sinks)


def _s5_prep_kernel(lr_ref, li_ref, ldt_ref, br_ref, bi_ref, ar_ref, ai_ref, bbr_ref, bbi_ref):
    lr, li = lr_ref[...], li_ref[...]
    dt = jnp.exp(ldt_ref[...])
    mag = jnp.exp(lr * dt)
    ar = mag * jnp.cos(li * dt)
    ai = mag * jnp.sin(li * dt)
    den = lr * lr + li * li
    cr = ((ar - 1.0) * lr + ai * li) / den
    ci = (ai * lr - (ar - 1.0) * li) / den
    br, bi = br_ref[...], bi_ref[...]
    ar_ref[...] = ar
    ai_ref[...] = ai
    bbr_ref[...] = cr * br - ci * bi
    bbi_ref[...] = cr * bi + ci * br


def _s5_prep(lam_re, lam_im, log_dt, b_re, b_im):
    G, P, C = b_re.shape
    rep = lambda a: jnp.repeat(a, C, axis=0)
    bt = lambda a: a.transpose(0, 2, 1).reshape(G * C, P)
    ldt = jnp.broadcast_to(log_dt[:, None], (G, P))
    full = pl.BlockSpec((G * C, P), lambda: (0, 0))
    outs = pl.pallas_call(
        _s5_prep_kernel, in_specs=[full] * 5, out_specs=[full] * 4,
        out_shape=[jax.ShapeDtypeStruct((G * C, P), F32)] * 4, name="s5_prep",
    )(rep(lam_re), rep(lam_im), rep(ldt), bt(b_re), bt(b_im))
    ar, ai, bbr, bbi = [o.reshape(G, C, P) for o in outs]
    return ar[:, 0], ai[:, 0], bbr, bbi


def _s5_kernel(u_ref, bm_ref, cm_ref, ar_ref, ai_ref, d_ref, gw_ref, gb_ref, o_ref,
               h_ref, carry_ref):
    rows = u_ref.shape[0] // S5_CHUNKS
    steps = rows // SUBLANES
    half = h_ref.shape[3] // 2

    @pl.when(pl.program_id(0) == 0)
    def _():
        carry_ref[...] = jnp.zeros_like(carry_ref)

    def chunk_rows(c):
        return slice(c * rows, (c + 1) * rows)

    def input_map(c):
        ub = u_ref[chunk_rows(c), :].astype(BF16)
        for g in range(S5_LANE_GROUPS):
            h_ref[c, g] = _dot(ub[:, g * LANES:(g + 1) * LANES], bm_ref[g])

    def scan(c):
        for g in range(S5_LANE_GROUPS):
            ar = jnp.broadcast_to(ar_ref[g], (SUBLANES, half))
            ai = jnp.broadcast_to(ai_ref[g], (SUBLANES, half))
            hr = carry_ref[g, :, 0:half]
            hi = carry_ref[g, :, half:2 * half]
            for t in range(steps):
                sl = slice(t * SUBLANES, (t + 1) * SUBLANES)
                nr = ar * hr - ai * hi + h_ref[c, g, sl, 0:half]
                ni = ar * hi + ai * hr + h_ref[c, g, sl, half:2 * half]
                h_ref[c, g, sl, 0:half] = nr
                h_ref[c, g, sl, half:2 * half] = ni
                hr, hi = nr, ni
            carry_ref[g, :, 0:half] = hr
            carry_ref[g, :, half:2 * half] = hi

    def output_map(c):
        u = u_ref[chunk_rows(c), :]
        ys = [_dot(h_ref[c, g].astype(BF16), cm_ref[g]) for g in range(S5_LANE_GROUPS)]
        y = jnp.concatenate(ys, axis=-1) + d_ref[...] * u
        z = jax.nn.gelu(y)
        gate = jax.nn.sigmoid(_dot(z.astype(BF16), gw_ref[...]) + gb_ref[...])
        o_ref[chunk_rows(c), :] = (z * gate).astype(BF16)

    for c in range(S5_CHUNKS):
        input_map(c)
    for c in range(S5_CHUNKS):
        scan(c)
        output_map(c)


def _s5(u_t, bmat, cmat, ar, ai, d, glu_w, glu_b):
    R, W = u_t.shape
    rows = S5_CHUNKS * TIME_CHUNK * SUBLANES
    nstate = bmat.shape[2]
    blk = pl.BlockSpec((rows, W), lambda t: (t, 0))
    return pl.pallas_call(
        _s5_kernel, grid=(R // rows,),
        in_specs=[blk, _const_spec(bmat.shape), _const_spec(cmat.shape), _const_spec(ar.shape),
                  _const_spec(ai.shape), _const_spec((1, W)), _const_spec((W, W)),
                  _const_spec((1, W))],
        out_specs=blk, out_shape=jax.ShapeDtypeStruct((R, W), BF16),
        scratch_shapes=[pltpu.VMEM((S5_CHUNKS, S5_LANE_GROUPS, rows // S5_CHUNKS, nstate), F32),
                        pltpu.VMEM((S5_LANE_GROUPS, SUBLANES, nstate), F32)],
        compiler_params=_params("arbitrary"), name="s5_glu",
    )(u_t, bmat, cmat, ar, ai, d, glu_w, glu_b)


def _s5_matrices(ar, ai, bbr, bbi, c_re, c_im):
    L, GL = S5_LANE_GROUPS, S5_GROUPS // S5_LANE_GROUPS
    C, P = S5_GROUP, S5_STATE
    eye = jnp.eye(GL, dtype=F32)

    def inmap(b):
        return jnp.einsum("lgcp,gh->lgchp", b.reshape(L, GL, C, P), eye).reshape(L, GL * C, GL * P)

    def outmap(c):
        return jnp.einsum("lgcp,gh->lgphc", c.reshape(L, GL, C, P), eye).reshape(L, GL * P, GL * C)

    bmat = jnp.concatenate([inmap(bbr), inmap(bbi)], axis=2).astype(BF16)
    cmat = jnp.concatenate([outmap(c_re), outmap(-c_im)], axis=1).astype(BF16)
    a_r = ar.reshape(L, 1, GL * P)
    a_i = ai.reshape(L, 1, GL * P)
    return bmat, cmat, a_r, a_i


def _block_diag_pairs(w):
    nb, bs, _ = w.shape
    half = nb // 2
    eye = jnp.eye(half, dtype=w.dtype)
    out = jnp.einsum("thij,hk->thikj", w.reshape(2, half, bs, bs), eye)
    return out.reshape(2, half * bs, half * bs).astype(BF16)


def kernel(x, p, ffn1_norm, ffn1_wg, ffn1_wu, ffn1_wd, mix_norm, ffn2_norm, ffn2_wg, ffn2_wu, ffn2_wd, ple_w, ple_norm, ple_gate_norm, ple_gate_w, ev_w_in, lru_conv_w, lru_conv_b, lru_wa, lru_ba, lru_wx, lru_bx, lru_lambda, fox_bf, fox_q_norm, fox_k_norm, ev_w_out, od_w_in, swa_q_norm, swa_k_norm, swa_sinks, s5_lambda_re, s5_lambda_im, s5_log_dt, s5_b_re, s5_b_im, s5_c_re, s5_c_im, s5_d, s5_glu_w, s5_glu_b, od_w_out):
    B, S, D = x.shape
    depth = p.shape[0]
    assert B == SUBLANES and D == D_MODEL and S % (S5_CHUNKS * TIME_CHUNK) == 0
    T = B * S
    bf = lambda a: a.astype(BF16)
    row = lambda a: a[:, None, :]
    per_head = lambda gain, heads: jnp.tile(gain, heads)[None, :]

    f1 = (row(ffn1_norm), bf(ffn1_wg), bf(ffn1_wu), bf(ffn1_wd))
    f2 = (row(ffn2_norm), bf(ffn2_wg), bf(ffn2_wu), bf(ffn2_wd))
    ple = (p, bf(ple_w), row(ple_norm), row(ple_gate_norm), bf(ple_gate_w))
    cos, sin = _rope_tables(S)

    for i in range(depth):
        j = i // 2
        x = _ffn(x.reshape(T, D), i, *f1).reshape(B, S, D)
        g = mix_norm[i][None, :]
        if i % 2 == 0:
            w_in = bf(ev_w_in[j])
            o1, o2 = 2 * LRU_WIDTH, 2 * LRU_WIDTH + 3 * FOX_DIM
            w_f = jnp.pad(w_in[:, o2:], ((0, 0), (0, LANES - FOX_HEADS)))
            b_f = jnp.pad(fox_bf[j], (0, LANES - FOX_HEADS))[None, :]
            lru = (lru_conv_w[j], lru_conv_b[j][None, :],
                   _block_diag_pairs(lru_wa[j]), lru_ba[j][None, :],
                   _block_diag_pairs(lru_wx[j]), lru_bx[j][None, :], lru_lambda[j][None, :])
            a_out, q, k, qe, ke, v = _even_inproj(
                x, g, w_in[:, :o1], w_in[:, o1:o2], w_f, b_f,
                per_head(fox_q_norm[j], FOX_HEADS), per_head(fox_k_norm[j], FOX_HEADS), lru)
            b_out = _fox_attention(q, qe, k, ke, _value_transpose(v, ATTN_TILE))
            w_out = bf(ev_w_out[j])
            mixed = (a_out, b_out, w_out[:LRU_WIDTH], w_out[LRU_WIDTH:])
        else:
            w_in = bf(od_w_in[j])
            kvd = SWA_KV_HEADS * HEAD_DIM
            o1, o2 = SWA_DIM, SWA_DIM + 2 * kvd
            q, kd, v, u = _odd_inproj(
                x, g, w_in[:, :o1], w_in[:, o1:o2], w_in[:, o2:],
                per_head(swa_q_norm[j], SWA_HEADS), per_head(swa_k_norm[j], SWA_KV_HEADS), cos, sin)
            c_out = _swa_attention(q, kd, _value_transpose(v, SWA_WINDOW), swa_sinks[j][None, :])
            ar, ai, bbr, bbi = _s5_prep(s5_lambda_re[j], s5_lambda_im[j], s5_log_dt[j],
                                        s5_b_re[j], s5_b_im[j])
            bmat, cmat, a_r, a_i = _s5_matrices(ar, ai, bbr, bbi, s5_c_re[j], s5_c_im[j])
            d_out = _s5(u, bmat, cmat, a_r, a_i, s5_d[j][None, :],
                        bf(s5_glu_w[j]), s5_glu_b[j][None, :])
            w_out = bf(od_w_out[j])
            mixed = (d_out, c_out, w_out[SWA_DIM:], w_out[:SWA_DIM])
        x = _mix_ffn_ple(x, *mixed, i, f2, ple)
    return x
```

```python
import numpy as np

import jax
import jax.numpy as jnp
from jax import lax
from jax.experimental import pallas as pl
from jax.experimental.pallas import tpu as pltpu

F32 = jnp.float32
BF16 = jnp.bfloat16

D_MODEL = 1024
HEAD_DIM = 64
LRU_WIDTH = 512
LRU_CONV = 4
LRU_C = 8.0
FOX_HEADS = 8
FOX_DIM = 512
SWA_HEADS = 8
SWA_KV_HEADS = 2
SWA_GROUP = SWA_HEADS // SWA_KV_HEADS
SWA_DIM = 512
SWA_WINDOW = 128
S5_WIDTH = 512
S5_GROUP = 16
S5_GROUPS = 32
S5_STATE = 64
D_FF = 2816
PLE_DIM = 256
ROPE_THETA = 10000.0
EPS = 1e-6
MACARON = 0.5
QK_SCALE = HEAD_DIM ** -0.5
LOG2E = 1.4426950408889634

SUBLANES = 8
LANES = 128
TIME_CHUNK = 64
ROW_PITCH = TIME_CHUNK + 8
ROW_TILE = SUBLANES * TIME_CHUNK
ATTN_TILE = 512
S5_LANE_GROUPS = 4
BIAS_LANES = 6
VT_ROWS = 80
VMEM_LIMIT = 56 * 1024 * 1024


def _dot(a, b):
    return jnp.dot(a, b, preferred_element_type=F32)


def _dot_nt(a, b):
    return lax.dot_general(a, b, (((1,), (1,)), ((), ())), preferred_element_type=F32)


def _rms(x, g):
    ms = jnp.mean(x * x, axis=-1, keepdims=True)
    return x * lax.rsqrt(ms + EPS) * g


def _head_rms(x, ind, gain):
    ms = _dot((x * x).astype(BF16), ind)
    return x * lax.rsqrt(ms + EPS) * gain


def _softplus(x):
    return jnp.maximum(x, 0.0) + jnp.log1p(jnp.exp(-jnp.abs(x)))


def _log_sigmoid(x):
    return -_softplus(-x)


def _cumsum_time(x):
    n = x.shape[0]
    row = lax.broadcasted_iota(jnp.int32, x.shape, 0)
    d = SUBLANES
    while d < n:
        x = x + jnp.where(row >= d, pltpu.roll(x, d, axis=0), 0.0)
        d *= 2
    return x


def _to_time_major(val, slab_ref):
    rows, width = val.shape
    steps = rows // SUBLANES
    for s in range(width // LANES):
        for b in range(SUBLANES):
            slab_ref[s, b * ROW_PITCH:b * ROW_PITCH + steps, :] = (
                val[b * steps:(b + 1) * steps, s * LANES:(s + 1) * LANES])
    return jnp.concatenate(
        [jnp.concatenate([slab_ref[s, pl.ds(t, SUBLANES, stride=ROW_PITCH), :]
                          for s in range(width // LANES)], axis=-1) for t in range(steps)], axis=0)


def _to_batch_major(val, slab_ref):
    rows, width = val.shape
    steps = rows // SUBLANES
    for s in range(width // LANES):
        slab_ref[s, 0:rows, :] = val[:, s * LANES:(s + 1) * LANES]
    return jnp.concatenate(
        [jnp.concatenate([slab_ref[s, pl.ds(b, steps, stride=SUBLANES), :]
                          for s in range(width // LANES)], axis=-1) for b in range(SUBLANES)], axis=0)


def _split3_bf16(c):
    hi = c.astype(BF16).astype(F32)
    r = c - hi
    mid = r.astype(BF16).astype(F32)
    return hi, mid, r - mid


def _keep_lanes(x, lo, hi):
    lane = lax.broadcasted_iota(jnp.int32, x.shape, 1)
    return jnp.where((lane >= lo) & (lane < hi), x.astype(F32), 0.0).astype(BF16)


def _params(*sem):
    return pltpu.CompilerParams(dimension_semantics=sem, vmem_limit_bytes=VMEM_LIMIT)


def _const_spec(shape):
    nd = len(shape)
    return pl.BlockSpec(shape, lambda *_: (0,) * nd, pipeline_mode=pl.Buffered(1))


def _layer_spec(shape, layer):
    nd = len(shape)
    return pl.BlockSpec((None,) + tuple(shape), lambda *_: (layer,) + (0,) * nd,
                        pipeline_mode=pl.Buffered(1))


def _head_mean_matrix(width):
    h = np.arange(width) // HEAD_DIM
    return jnp.asarray((h[:, None] == h[None, :]) / HEAD_DIM, dtype=BF16)


def _swiglu_update(x, g_ref, wg_ref, wu_ref, wd_ref):
    n = _rms(x, g_ref[...]).astype(BF16)
    hg = _dot(n, wg_ref[...])
    hu = _dot(n, wu_ref[...])
    act = (hg * jax.nn.sigmoid(hg) * hu).astype(BF16)
    return x + MACARON * _dot(act, wd_ref[...])


def _ffn_kernel(x_ref, g_ref, wg_ref, wu_ref, wd_ref, o_ref):
    half = x_ref.shape[0] // 2
    for i in range(2):
        rs = slice(i * half, (i + 1) * half)
        o_ref[rs, :] = _swiglu_update(x_ref[rs, :], g_ref, wg_ref, wu_ref, wd_ref)


def _ffn(x2d, layer, norm, wg, wu, wd):
    T, D = x2d.shape
    tm = min(ROW_TILE, T)
    row = pl.BlockSpec((tm, D), lambda i: (i, 0))
    return pl.pallas_call(
        _ffn_kernel, grid=(T // tm,),
        in_specs=[row, _layer_spec((1, D), layer), _layer_spec((D, D_FF), layer),
                  _layer_spec((D, D_FF), layer), _layer_spec((D_FF, D), layer)],
        out_specs=row, out_shape=jax.ShapeDtypeStruct((T, D), F32),
        compiler_params=_params("parallel"), name="ffn",
    )(x2d, norm, wg, wu, wd)


def _mix_ffn_ple_kernel(x_ref, tmaj_ref, bmaj_ref, wt_ref, wb_ref,
                        g_ref, wg_ref, wu_ref, wd_ref, p_ref, pw_ref, pn_ref, gn_ref, gw_ref, o_ref,
                        slab_ref):
    B, ts, D = x_ref.shape
    rows = B * ts
    part_t = _to_batch_major(tmaj_ref[...].astype(F32), slab_ref).astype(BF16)
    part_b = bmaj_ref[...].reshape(rows, bmaj_ref.shape[2])
    x = x_ref[...].reshape(rows, D) + _dot(part_t, wt_ref[...]) + _dot(part_b, wb_ref[...])
    x = _swiglu_update(x, g_ref, wg_ref, wu_ref, wd_ref)
    e = _rms(_dot(p_ref[...].reshape(rows, PLE_DIM).astype(BF16), pw_ref[...]), pn_ref[...])
    gate = jax.nn.sigmoid(_dot(_rms(x, gn_ref[...]).astype(BF16), gw_ref[...]))
    o_ref[...] = (x + gate * e).reshape(B, ts, D)


def _mix_ffn_ple(x, part_tmaj, part_bmaj, w_tmaj, w_bmaj, layer, ffn, ple):
    B, S, D = x.shape
    ts = TIME_CHUNK
    W = part_tmaj.shape[1]
    xs = pl.BlockSpec((B, ts, D), lambda s: (0, s, 0))
    in_specs = [xs, pl.BlockSpec((B * ts, W), lambda s: (s, 0)),
                pl.BlockSpec((B, ts, W), lambda s: (0, s, 0)),
                _const_spec((W, D)), _const_spec((W, D)),
                _layer_spec((1, D), layer), _layer_spec((D, D_FF), layer),
                _layer_spec((D, D_FF), layer), _layer_spec((D_FF, D), layer),
                pl.BlockSpec((None, B, ts, PLE_DIM), lambda s: (layer, 0, s, 0)),
                _layer_spec((PLE_DIM, D), layer), _layer_spec((1, D), layer),
                _layer_spec((1, D), layer), _layer_spec((D, D), layer)]
    return pl.pallas_call(
        _mix_ffn_ple_kernel, grid=(S // ts,), in_specs=in_specs, out_specs=xs,
        out_shape=jax.ShapeDtypeStruct((B, S, D), F32),
        scratch_shapes=[pltpu.VMEM((W // LANES, B * ts, LANES), F32)],
        compiler_params=_params("parallel"), name="mix_ffn_ple",
    )(x, part_tmaj, part_bmaj, w_tmaj, w_bmaj, *ffn, *ple)


def _bias_lane_maps():
    eq = np.zeros((LANES, LANES), np.float32)
    ek = np.zeros((LANES, LANES), np.float32)
    oq = np.zeros((1, LANES), np.float32)
    ok = np.zeros((1, LANES), np.float32)
    for h in range(FOX_HEADS):
        for i in range(3):
            eq[i * FOX_HEADS + h, BIAS_LANES * h + i] = 1.0
            ek[i * FOX_HEADS + h, BIAS_LANES * h + 3 + i] = -1.0
            oq[0, BIAS_LANES * h + 3 + i] = 1.0
            ok[0, BIAS_LANES * h + i] = 1.0
    return jnp.asarray(eq, BF16), jnp.asarray(ek, BF16), jnp.asarray(oq), jnp.asarray(ok)


def _rg_lru_chunk(xa, ya, cw_ref, cb_ref, wa_ref, ba_ref, wx_ref, bx_ref, lam_ref, xprev_ref, h_ref):
    rows = xa.shape[0]
    halo = (LRU_CONV - 1) * SUBLANES
    half = LRU_WIDTH // 2
    xfull = jnp.concatenate([xprev_ref[...], xa], axis=0)
    xprev_ref[...] = xa[rows - halo:rows, :]
    xc = cb_ref[...]
    for tap in range(LRU_CONV):
        xc = xc + xfull[tap * SUBLANES:tap * SUBLANES + rows, :] * cw_ref[tap:tap + 1, :]
    xb = xc.astype(BF16)

    def gate(w_ref, b_ref):
        z = jnp.concatenate([_dot(xb[:, :half], w_ref[0]), _dot(xb[:, half:], w_ref[1])], axis=-1)
        return jax.nn.sigmoid(z + b_ref[...])

    r = gate(wa_ref, ba_ref)
    i = gate(wx_ref, bx_ref)
    log_a = -LRU_C * r * _softplus(lam_ref[...])
    a = jnp.exp(log_a)
    th = jnp.tanh(log_a)
    b = jnp.sqrt(-2.0 * th / (1.0 - th)) * (i * xc)
    h = h_ref[...]
    hs = []
    for t in range(rows // SUBLANES):
        sl = slice(t * SUBLANES, (t + 1) * SUBLANES)
        h = a[sl, :] * h + b[sl, :]
        hs.append(h)
    h_ref[...] = h
    return jax.nn.gelu(ya) * jnp.concatenate(hs, axis=0)


def _even_in_kernel(x_ref, g_ref, wxyf_ref, wqkv_ref, bf_ref, qg_ref, kg_ref, ind_ref,
                    eqk_ref, oqk_ref, cw_ref, cb_ref, wa_ref, ba_ref, wx_ref, bx_ref, lam_ref,
                    a_out_ref, q_ref, k_ref, qe_ref, ke_ref, v_ref,
                    carry_ref, slab_ref, xprev_ref, h_ref):
    B, ts, D = x_ref.shape
    rows = B * ts

    @pl.when(pl.program_id(0) == 0)
    def _():
        carry_ref[...] = jnp.zeros_like(carry_ref)
        xprev_ref[...] = jnp.zeros_like(xprev_ref)
        h_ref[...] = jnp.zeros_like(h_ref)

    n = _rms(x_ref[...].reshape(rows, D), g_ref[...]).astype(BF16)
    f_cols = _dot(n, wxyf_ref[:, 2 * LRU_WIDTH:])
    xyf = jnp.concatenate([_dot(n, wxyf_ref[:, 0:2 * LRU_WIDTH]), f_cols], axis=-1)
    xyf = _to_time_major(xyf, slab_ref)
    a_out_ref[...] = _rg_lru_chunk(xyf[:, :LRU_WIDTH], xyf[:, LRU_WIDTH:2 * LRU_WIDTH], cw_ref, cb_ref,
                                   wa_ref, ba_ref, wx_ref, bx_ref, lam_ref, xprev_ref,
                                   h_ref).astype(BF16)

    logf = _log_sigmoid(xyf[:, 2 * LRU_WIDTH:] + bf_ref[...]) * LOG2E
    c = _cumsum_time(logf) + pltpu.repeat(carry_ref[...], ts, axis=0)
    carry_ref[...] = c[rows - SUBLANES:rows, :]
    hi, mid, lo = _split3_bf16(c)
    lane = lax.broadcasted_iota(jnp.int32, c.shape, 1)
    packed = jnp.where(lane < FOX_HEADS, hi, jnp.where(
        lane < 2 * FOX_HEADS, pltpu.roll(mid, FOX_HEADS, axis=1), jnp.where(
            lane < 3 * FOX_HEADS, pltpu.roll(lo, 2 * FOX_HEADS, axis=1), 0.0)))
    packed = _to_batch_major(packed, slab_ref).astype(BF16)
    qke = (_dot(packed, eqk_ref[...]) + oqk_ref[...]).astype(BF16)
    qe_ref[...] = qke[:, :LANES].reshape(B, ts, LANES)
    ke_ref[...] = qke[:, LANES:].reshape(B, ts, LANES)

    qkv = _dot(n, wqkv_ref[...])
    ind = ind_ref[...]
    q = _head_rms(qkv[:, :FOX_DIM], ind, qg_ref[...]) * (QK_SCALE * LOG2E)
    k = _head_rms(qkv[:, FOX_DIM:2 * FOX_DIM], ind, kg_ref[...])
    q_ref[...] = q.astype(BF16).reshape(B, ts, FOX_DIM)
    k_ref[...] = k.astype(BF16).reshape(B, ts, FOX_DIM)
    v_ref[...] = qkv[:, 2 * FOX_DIM:].astype(BF16).reshape(B, ts, FOX_DIM)


def _even_inproj(x, g, w_xy, w_qkv, w_f, b_f, qg, kg, lru):
    B, S, D = x.shape
    ts = TIME_CHUNK
    eq, ek, oq, ok = _bias_lane_maps()
    tmaj = pl.BlockSpec((B * ts, LRU_WIDTH), lambda s: (s, 0))
    bmaj = lambda w: pl.BlockSpec((B, ts, w), lambda s: (0, s, 0))
    consts = [g, jnp.concatenate([w_xy, w_f], axis=1), w_qkv, b_f, qg, kg,
              _head_mean_matrix(FOX_DIM), jnp.concatenate([eq, ek], axis=1),
              jnp.concatenate([oq, ok], axis=1), *lru]
    return pl.pallas_call(
        _even_in_kernel, grid=(S // ts,),
        in_specs=[bmaj(D)] + [_const_spec(a.shape) for a in consts],
        out_specs=[tmaj, bmaj(FOX_DIM), bmaj(FOX_DIM), bmaj(LANES), bmaj(LANES), bmaj(FOX_DIM)],
        out_shape=[jax.ShapeDtypeStruct((S * B, LRU_WIDTH), BF16),
                   jax.ShapeDtypeStruct((B, S, FOX_DIM), BF16),
                   jax.ShapeDtypeStruct((B, S, FOX_DIM), BF16),
                   jax.ShapeDtypeStruct((B, S, LANES), BF16),
                   jax.ShapeDtypeStruct((B, S, LANES), BF16),
                   jax.ShapeDtypeStruct((B, S, FOX_DIM), BF16)],
        scratch_shapes=[pltpu.VMEM((SUBLANES, LANES), F32),
                        pltpu.VMEM((2 * LRU_WIDTH // LANES + 1, SUBLANES * ROW_PITCH, LANES), F32),
                        pltpu.VMEM(((LRU_CONV - 1) * SUBLANES, LRU_WIDTH), F32),
                        pltpu.VMEM((SUBLANES, LRU_WIDTH), F32)],
        compiler_params=_params("arbitrary"), name="even_inproj",
    )(x, *consts)


def _vt_kernel(v_ref, o_ref):
    heads, nchunk, _, chunk = o_ref.shape
    vt = v_ref[...].astype(F32).T
    ones = jnp.ones((VT_ROWS - HEAD_DIM, chunk), BF16)
    for h in range(heads):
        for c in range(nchunk):
            o_ref[h, c, 0:HEAD_DIM, :] = vt[h * HEAD_DIM:(h + 1) * HEAD_DIM,
                                            c * chunk:(c + 1) * chunk].astype(BF16)
            o_ref[h, c, HEAD_DIM:VT_ROWS, :] = ones


def _value_transpose(v, chunk):
    B, S, W = v.shape
    heads = W // HEAD_DIM
    tk = min(ATTN_TILE, S)
    chunk = min(chunk, tk)
    return pl.pallas_call(
        _vt_kernel, grid=(B, S // tk),
        in_specs=[pl.BlockSpec((None, tk, W), lambda b, j: (b, j, 0))],
        out_specs=pl.BlockSpec((None, heads, tk // chunk, VT_ROWS, chunk),
                               lambda b, j: (b, 0, j, 0, 0)),
        out_shape=jax.ShapeDtypeStruct((B, heads, S // chunk, VT_ROWS, chunk), BF16),
        compiler_params=_params("parallel", "parallel"), name="value_transpose",
    )(v)


def _fox_kernel(q_ref, qe_ref, k_ref, ke_ref, vt_ref, o_ref, acc_ref, s_ref):
    tq = q_ref.shape[0]
    qi = pl.program_id(1)
    qe = qe_ref[...]
    qq = []
    for h in range(FOX_HEADS):
        half = h % 2
        blk = q_ref[:, (h // 2) * LANES:(h // 2 + 1) * LANES]
        qq.append(jnp.concatenate(
            [_keep_lanes(blk, half * HEAD_DIM, (half + 1) * HEAD_DIM),
             _keep_lanes(qe, BIAS_LANES * h, BIAS_LANES * (h + 1))], axis=-1))
    key = lax.broadcasted_iota(jnp.int32, (tq, tq), 0)
    qry = lax.broadcasted_iota(jnp.int32, (tq, tq), 1)
    acc_ref[...] = jnp.zeros_like(acc_ref)

    def update(j, m_run, masked):
        ks = pl.ds(pl.multiple_of(j * tq, tq), tq)
        ke = ke_ref[ks, :]
        for h in range(FOX_HEADS):
            kk = jnp.concatenate([k_ref[ks, (h // 2) * LANES:(h // 2 + 1) * LANES], ke], axis=-1)
            s = _dot_nt(kk, qq[h])
            if masked:
                s = jnp.where(key <= qry, s, -jnp.inf)
            s_ref[h] = s
        new = []
        for h in range(FOX_HEADS):
            m_new = jnp.maximum(m_run[h], jnp.max(s_ref[h], axis=0, keepdims=True))
            alpha = jnp.exp2(m_run[h] - m_new)
            p = jnp.exp2(s_ref[h] - m_new).astype(BF16)
            acc_ref[h] = alpha * acc_ref[h] + _dot(vt_ref[h, j], p)
            new.append(m_new)
        return tuple(new)

    init = (jnp.full((1, tq), -jnp.inf, F32),) * FOX_HEADS
    m_run = lax.fori_loop(0, qi, lambda j, m: update(j, m, False), init)
    update(qi, m_run, True)
    for pair in range(FOX_HEADS // 2):
        out = [acc_ref[h, 0:HEAD_DIM, :] / acc_ref[h, HEAD_DIM:HEAD_DIM + 1, :]
               for h in (2 * pair, 2 * pair + 1)]
        o_ref[:, pair * LANES:(pair + 1) * LANES] = jnp.concatenate(out, axis=0).T.astype(BF16)


def _fox_attention(q, qe, k, ke, vt):
    B, S, _ = q.shape
    nk, tq = vt.shape[2], vt.shape[4]
    qblk = lambda w: pl.BlockSpec((None, tq, w), lambda b, i: (b, i, 0))
    kblk = lambda w: pl.BlockSpec((None, S, w), lambda b, i: (b, 0, 0))
    return pl.pallas_call(
        _fox_kernel, grid=(B, S // tq),
        in_specs=[qblk(FOX_DIM), qblk(LANES), kblk(FOX_DIM), kblk(LANES),
                  pl.BlockSpec((None, FOX_HEADS, nk, VT_ROWS, tq), lambda b, i: (b, 0, 0, 0, 0))],
        out_specs=qblk(FOX_DIM),
        out_shape=jax.ShapeDtypeStruct((B, S, FOX_DIM), BF16),
        scratch_shapes=[pltpu.VMEM((FOX_HEADS, VT_ROWS, tq), F32),
                        pltpu.VMEM((FOX_HEADS, tq, tq), F32)],
        compiler_params=_params("parallel", "arbitrary"), name="fox_attention",
    )(q, qe, k, ke, vt)


def _rope_table_kernel(inv_ref, cos_ref, sin_ref):
    rows = cos_ref.shape[0]
    pos = pl.program_id(0) * rows + lax.broadcasted_iota(jnp.int32, cos_ref.shape, 0)
    lane = lax.broadcasted_iota(jnp.int32, cos_ref.shape, 1)
    ang = pos.astype(F32) * inv_ref[...]
    cos_ref[...] = jnp.cos(ang)
    sin_ref[...] = jnp.where((lane & (HEAD_DIM - 1)) < HEAD_DIM // 2, -1.0, 1.0) * jnp.sin(ang)


def _rope_tables(S):
    half = HEAD_DIM // 2
    inv = jnp.power(ROPE_THETA, -jnp.arange(half, dtype=F32) / half)
    inv4 = jnp.tile(inv, LANES // half)[None, :]
    rows = min(ROW_TILE, S)
    blk = pl.BlockSpec((rows, LANES), lambda i: (i, 0))
    return pl.pallas_call(
        _rope_table_kernel, grid=(S // rows,),
        in_specs=[pl.BlockSpec((1, LANES), lambda i: (0, 0))], out_specs=[blk, blk],
        out_shape=[jax.ShapeDtypeStruct((S, LANES), F32)] * 2,
        compiler_params=_params("parallel"), name="rope_tables",
    )(inv4)


def _rope(x, cos, sin):
    half = HEAD_DIM // 2
    lane = lax.broadcasted_iota(jnp.int32, x.shape, 1)
    first = (lane & (HEAD_DIM - 1)) < half
    swapped = jnp.where(first, pltpu.roll(x, LANES - half, axis=1), pltpu.roll(x, half, axis=1))
    return x * cos + swapped * sin


def _dup_halves(x):
    lane = lax.broadcasted_iota(jnp.int32, x.shape, 1)
    r = pltpu.roll(x, HEAD_DIM, axis=1)
    lo = lane < HEAD_DIM
    return jnp.concatenate([jnp.where(lo, x, r), jnp.where(lo, r, x)], axis=-1)


def _s5_chunk(u, bm_ref, cm_ref, ar_ref, ai_ref, d_ref, gw_ref, gb_ref, h_ref, state_ref):
    rows = u.shape[0]
    half = h_ref.shape[2] // 2
    ub = u.astype(BF16)
    for g in range(S5_LANE_GROUPS):
        h_ref[g] = _dot(ub[:, g * LANES:(g + 1) * LANES], bm_ref[g])
    ys = []
    for g in range(S5_LANE_GROUPS):
        ar = jnp.broadcast_to(ar_ref[g], (SUBLANES, half))
        ai = jnp.broadcast_to(ai_ref[g], (SUBLANES, half))
        hr = state_ref[g, :, 0:half]
        hi = state_ref[g, :, half:2 * half]
        for t in range(rows // SUBLANES):
            sl = slice(t * SUBLANES, (t + 1) * SUBLANES)
            nr = ar * hr - ai * hi + h_ref[g, sl, 0:half]
            ni = ar * hi + ai * hr + h_ref[g, sl, half:2 * half]
            h_ref[g, sl, 0:half] = nr
            h_ref[g, sl, half:2 * half] = ni
            hr, hi = nr, ni
        state_ref[g, :, 0:half] = hr
        state_ref[g, :, half:2 * half] = hi
        ys.append(_dot(h_ref[g].astype(BF16), cm_ref[g]))
    y = jnp.concatenate(ys, axis=-1) + d_ref[...] * u
    z = jax.nn.gelu(y)
    return z * jax.nn.sigmoid(_dot(z.astype(BF16), gw_ref[...]) + gb_ref[...])


def _odd_in_kernel(x_ref, g_ref, wq_ref, wkv_ref, wu_ref, qg_ref, kg_ref, ind_ref,
                   bm_ref, cm_ref, ar_ref, ai_ref, d_ref, gw_ref, gb_ref,
                   cos_ref, sin_ref, q_ref, kd_ref, v_ref, d_out_ref, slab_ref, h_ref, state_ref):
    B, ts, D = x_ref.shape
    rows = B * ts

    @pl.when(pl.program_id(0) == 0)
    def _():
        state_ref[...] = jnp.zeros_like(state_ref)

    n = _rms(x_ref[...].reshape(rows, D), g_ref[...]).astype(BF16)
    u = _to_time_major(_dot(n, wu_ref[...]), slab_ref)
    d_out_ref[...] = _s5_chunk(u, bm_ref, cm_ref, ar_ref, ai_ref, d_ref, gw_ref, gb_ref,
                               h_ref, state_ref).astype(BF16)
    cos = jnp.concatenate([cos_ref[...]] * B, axis=0)
    sin = jnp.concatenate([sin_ref[...]] * B, axis=0)
    q = _head_rms(_dot(n, wq_ref[...]), ind_ref[...], qg_ref[...])
    q = jnp.concatenate([_rope(q[:, blk * LANES:(blk + 1) * LANES], cos, sin)
                         for blk in range(SWA_DIM // LANES)], axis=-1) * (QK_SCALE * LOG2E)
    q_ref[...] = q.astype(BF16).reshape(B, ts, SWA_DIM)
    kv = _dot(n, wkv_ref[...])
    k = _head_rms(kv[:, :LANES], ind_ref[0:LANES, 0:LANES], kg_ref[...])
    kd_ref[...] = _dup_halves(_rope(k, cos, sin)).astype(BF16).reshape(B, ts, 2 * LANES)
    v_ref[...] = kv[:, LANES:].astype(BF16).reshape(B, ts, LANES)


def _odd_inproj(x, g, w_q, w_kv, w_u, qg, kg, s5, cos, sin):
    B, S, D = x.shape
    ts = TIME_CHUNK
    bmaj = lambda w: pl.BlockSpec((B, ts, w), lambda s: (0, s, 0))
    tab = pl.BlockSpec((ts, LANES), lambda s: (s, 0))
    consts = [g, w_q, w_kv, w_u, qg, kg, _head_mean_matrix(SWA_DIM), *s5]
    nstate = s5[0].shape[2]
    return pl.pallas_call(
        _odd_in_kernel, grid=(S // ts,),
        in_specs=[bmaj(D)] + [_const_spec(a.shape) for a in consts] + [tab, tab],
        out_specs=[bmaj(SWA_DIM), bmaj(2 * LANES), bmaj(LANES),
                   pl.BlockSpec((B * ts, S5_WIDTH), lambda s: (s, 0))],
        out_shape=[jax.ShapeDtypeStruct((B, S, SWA_DIM), BF16),
                   jax.ShapeDtypeStruct((B, S, 2 * LANES), BF16),
                   jax.ShapeDtypeStruct((B, S, LANES), BF16),
                   jax.ShapeDtypeStruct((S * B, S5_WIDTH), BF16)],
        scratch_shapes=[pltpu.VMEM((S5_WIDTH // LANES, SUBLANES * ROW_PITCH, LANES), F32),
                        pltpu.VMEM((S5_LANE_GROUPS, B * ts, nstate), F32),
                        pltpu.VMEM((S5_LANE_GROUPS, SUBLANES, nstate), F32)],
        compiler_params=_params("arbitrary"), name="odd_inproj",
    )(x, *consts, cos, sin)


def _swa_kernel(q_ref, kd_ref, vt_ref, sink_ref, o_ref):
    tq = q_ref.shape[0]
    W = SWA_WINDOW
    G = SWA_GROUP
    base = pl.program_id(1) * tq
    key = lax.broadcasted_iota(jnp.int32, (2 * W, G * W), 0)
    qoff = lax.broadcasted_iota(jnp.int32, (2 * W, G * W), 1) & (W - 1)
    cgrp = lax.broadcasted_iota(jnp.int32, (1, G * W), 1) // W
    sinks = sink_ref[...] * LOG2E
    blocks = [(n, kvh) for n in range(tq // W) for kvh in range(SWA_KV_HEADS)]
    kstarts = [pl.multiple_of(jnp.maximum(base + n * W - W, 0), W) for n in range(tq // W)]
    scores = []
    for n, kvh in blocks:
        parts = []
        for g in range(G):
            head = kvh * G + g
            blk = q_ref[n * W:(n + 1) * W, (head // 2) * LANES:(head // 2 + 1) * LANES]
            half = head % 2
            parts.append(_keep_lanes(blk, half * HEAD_DIM, (half + 1) * HEAD_DIM))
        scores.append(_dot_nt(kd_ref[pl.ds(kstarts[n], 2 * W), kvh * LANES:(kvh + 1) * LANES],
                              jnp.concatenate(parts, axis=0)))
    for (n, kvh), s in zip(blocks, scores):
        rows = slice(n * W, (n + 1) * W)
        chunk = kstarts[n] // W
        diff = (base + n * W + qoff) - (kstarts[n] + key)
        s = jnp.where((diff >= 0) & (diff < W), s, -jnp.inf)
        sink = jnp.zeros((1, G * W), F32)
        for g in range(G):
            head = kvh * G + g
            sink = jnp.where(cgrp == g, sinks[:, head:head + 1], sink)
        m = jnp.maximum(jnp.max(s, axis=0, keepdims=True), sink)
        p = jnp.exp2(s - m).astype(BF16)
        acc = _dot(vt_ref[kvh, chunk], p[0:W, :]) + _dot(vt_ref[kvh, chunk + 1], p[W:2 * W, :])
        den = acc[HEAD_DIM:HEAD_DIM + 1, :] + jnp.exp2(sink - m)
        o = acc[0:HEAD_DIM, :] / den
        for pair in range(G // 2):
            both = jnp.concatenate([o[:, (2 * pair) * W:(2 * pair + 1) * W],
                                    o[:, (2 * pair + 1) * W:(2 * pair + 2) * W]], axis=0)
            lb = kvh * (G // 2) + pair
            o_ref[rows, lb * LANES:(lb + 1) * LANES] = both.T.astype(BF16)


def _swa_attention(q, kd, vt, sinks):
    B, S, _ = q.shape
    tq = min(ATTN_TILE, S)
    return pl.pallas_call(
        _swa_kernel, grid=(B, S // tq),
        in_specs=[pl.BlockSpec((None, tq, SWA_DIM), lambda b, i: (b, i, 0)),
                  pl.BlockSpec((None, S, 2 * LANES), lambda b, i: (b, 0, 0)),
                  pl.BlockSpec((None,) + vt.shape[1:], lambda b, i: (b, 0, 0, 0, 0)),
                  pl.BlockSpec((1, SWA_HEADS), lambda b, i: (0, 0))],
        out_specs=pl.BlockSpec((None, tq, SWA_DIM), lambda b, i: (b, i, 0)),
        out_shape=jax.ShapeDtypeStruct((B, S, SWA_DIM), BF16),
        compiler_params=_params("parallel", "arbitrary"), name="swa_attention",
    )(q, kd, vt, sinks)


def _s5_prep_kernel(lr_ref, li_ref, ldt_ref, br_ref, bi_ref, ar_ref, ai_ref, bbr_ref, bbi_ref):
    lr, li = lr_ref[...], li_ref[...]
    dt = jnp.exp(ldt_ref[...])
    mag = jnp.exp(lr * dt)
    ar = mag * jnp.cos(li * dt)
    ai = mag * jnp.sin(li * dt)
    den = lr * lr + li * li
    cr = ((ar - 1.0) * lr + ai * li) / den
    ci = (ai * lr - (ar - 1.0) * li) / den
    br, bi = br_ref[...], bi_ref[...]
    ar_ref[...] = ar
    ai_ref[...] = ai
    bbr_ref[...] = cr * br - ci * bi
    bbi_ref[...] = cr * bi + ci * br


def _s5_prep(lam_re, lam_im, log_dt, b_re, b_im):
    G, P, C = b_re.shape
    rep = lambda a: jnp.repeat(a, C, axis=0)
    bt = lambda a: a.transpose(0, 2, 1).reshape(G * C, P)
    ldt = jnp.broadcast_to(log_dt[:, None], (G, P))
    full = pl.BlockSpec((G * C, P), lambda: (0, 0))
    outs = pl.pallas_call(
        _s5_prep_kernel, in_specs=[full] * 5, out_specs=[full] * 4,
        out_shape=[jax.ShapeDtypeStruct((G * C, P), F32)] * 4, name="s5_prep",
    )(rep(lam_re), rep(lam_im), rep(ldt), bt(b_re), bt(b_im))
    ar, ai, bbr, bbi = [o.reshape(G, C, P) for o in outs]
    return ar[:, 0], ai[:, 0], bbr, bbi


def _s5_matrices(ar, ai, bbr, bbi, c_re, c_im):
    L, GL = S5_LANE_GROUPS, S5_GROUPS // S5_LANE_GROUPS
    C, P = S5_GROUP, S5_STATE
    eye = jnp.eye(GL, dtype=F32)

    def inmap(b):
        return jnp.einsum("lgcp,gh->lgchp", b.reshape(L, GL, C, P), eye).reshape(L, GL * C, GL * P)

    def outmap(c):
        return jnp.einsum("lgcp,gh->lgphc", c.reshape(L, GL, C, P), eye).reshape(L, GL * P, GL * C)

    bmat = jnp.concatenate([inmap(bbr), inmap(bbi)], axis=2).astype(BF16)
    cmat = jnp.concatenate([outmap(c_re), outmap(-c_im)], axis=1).astype(BF16)
    a_r = ar.reshape(L, 1, GL * P)
    a_i = ai.reshape(L, 1, GL * P)
    return bmat, cmat, a_r, a_i


def _block_diag_pairs(w):
    nb, bs, _ = w.shape
    half = nb // 2
    eye = jnp.eye(half, dtype=w.dtype)
    out = jnp.einsum("thij,hk->thikj", w.reshape(2, half, bs, bs), eye)
    return out.reshape(2, half * bs, half * bs).astype(BF16)


def kernel(x, p, ffn1_norm, ffn1_wg, ffn1_wu, ffn1_wd, mix_norm, ffn2_norm, ffn2_wg, ffn2_wu, ffn2_wd, ple_w, ple_norm, ple_gate_norm, ple_gate_w, ev_w_in, lru_conv_w, lru_conv_b, lru_wa, lru_ba, lru_wx, lru_bx, lru_lambda, fox_bf, fox_q_norm, fox_k_norm, ev_w_out, od_w_in, swa_q_norm, swa_k_norm, swa_sinks, s5_lambda_re, s5_lambda_im, s5_log_dt, s5_b_re, s5_b_im, s5_c_re, s5_c_im, s5_d, s5_glu_w, s5_glu_b, od_w_out):
    B, S, D = x.shape
    depth = p.shape[0]
    assert B == SUBLANES and D == D_MODEL and S % TIME_CHUNK == 0
    T = B * S
    bf = lambda a: a.astype(BF16)
    row = lambda a: a[:, None, :]
    per_head = lambda gain, heads: jnp.tile(gain, heads)[None, :]

    f1 = (row(ffn1_norm), bf(ffn1_wg), bf(ffn1_wu), bf(ffn1_wd))
    f2 = (row(ffn2_norm), bf(ffn2_wg), bf(ffn2_wu), bf(ffn2_wd))
    ple = (p, bf(ple_w), row(ple_norm), row(ple_gate_norm), bf(ple_gate_w))
    cos, sin = _rope_tables(S)

    for i in range(depth):
        j = i // 2
        x = _ffn(x.reshape(T, D), i, *f1).reshape(B, S, D)
        g = mix_norm[i][None, :]
        if i % 2 == 0:
            w_in = bf(ev_w_in[j])
            o1, o2 = 2 * LRU_WIDTH, 2 * LRU_WIDTH + 3 * FOX_DIM
            w_f = jnp.pad(w_in[:, o2:], ((0, 0), (0, LANES - FOX_HEADS)))
            b_f = jnp.pad(fox_bf[j], (0, LANES - FOX_HEADS))[None, :]
            lru = (lru_conv_w[j], lru_conv_b[j][None, :],
                   _block_diag_pairs(lru_wa[j]), lru_ba[j][None, :],
                   _block_diag_pairs(lru_wx[j]), lru_bx[j][None, :], lru_lambda[j][None, :])
            a_out, q, k, qe, ke, v = _even_inproj(
                x, g, w_in[:, :o1], w_in[:, o1:o2], w_f, b_f,
                per_head(fox_q_norm[j], FOX_HEADS), per_head(fox_k_norm[j], FOX_HEADS), lru)
            b_out = _fox_attention(q, qe, k, ke, _value_transpose(v, ATTN_TILE))
            w_out = bf(ev_w_out[j])
            mixed = (a_out, b_out, w_out[:LRU_WIDTH], w_out[LRU_WIDTH:])
        else:
            w_in = bf(od_w_in[j])
            kvd = SWA_KV_HEADS * HEAD_DIM
            o1, o2 = SWA_DIM, SWA_DIM + 2 * kvd
            ar, ai, bbr, bbi = _s5_prep(s5_lambda_re[j], s5_lambda_im[j], s5_log_dt[j],
                                        s5_b_re[j], s5_b_im[j])
            bmat, cmat, a_r, a_i = _s5_matrices(ar, ai, bbr, bbi, s5_c_re[j], s5_c_im[j])
            s5 = (bmat, cmat, a_r, a_i, s5_d[j][None, :], bf(s5_glu_w[j]), s5_glu_b[j][None, :])
            q, kd, v, d_out = _odd_inproj(
                x, g, w_in[:, :o1], w_in[:, o1:o2], w_in[:, o2:],
                per_head(swa_q_norm[j], SWA_HEADS), per_head(swa_k_norm[j], SWA_KV_HEADS), s5,
                cos, sin)
            c_out = _swa_attention(q, kd, _value_transpose(v, SWA_WINDOW), swa_sinks[j][None, :])
            w_out = bf(od_w_out[j])
            mixed = (d_out, c_out, w_out[SWA_DIM:], w_out[:SWA_DIM])
        x = _mix_ffn_ple(x, *mixed, i, f2, ple)
    return x
```

```python
import numpy as np

import jax
import jax.numpy as jnp
from jax import lax
from jax.experimental import pallas as pl
from jax.experimental.pallas import tpu as pltpu

F32 = jnp.float32
BF16 = jnp.bfloat16

D_MODEL = 1024
HEAD_DIM = 64
LRU_WIDTH = 512
LRU_CONV = 4
LRU_C = 8.0
FOX_HEADS = 8
FOX_DIM = 512
SWA_HEADS = 8
SWA_KV_HEADS = 2
SWA_GROUP = SWA_HEADS // SWA_KV_HEADS
SWA_DIM = 512
SWA_WINDOW = 128
S5_WIDTH = 512
S5_GROUP = 16
S5_GROUPS = 32
S5_STATE = 64
D_FF = 2816
PLE_DIM = 256
ROPE_THETA = 10000.0
EPS = 1e-6
MACARON = 0.5
QK_SCALE = HEAD_DIM ** -0.5
LOG2E = 1.4426950408889634

SUBLANES = 8
LANES = 128
TIME_CHUNK = 64
ROW_PITCH = TIME_CHUNK + 8
ROW_TILE = SUBLANES * TIME_CHUNK
ATTN_TILE = 512
S5_LANE_GROUPS = 4
BIAS_LANES = 6
VT_ROWS = 80
VMEM_LIMIT = 56 * 1024 * 1024


def _dot(a, b):
    return jnp.dot(a, b, preferred_element_type=F32)


def _dot_nt(a, b):
    return lax.dot_general(a, b, (((1,), (1,)), ((), ())), preferred_element_type=F32)


def _rms(x, g):
    ms = jnp.mean(x * x, axis=-1, keepdims=True)
    return x * lax.rsqrt(ms + EPS) * g


def _head_rms(x, ind, gain):
    ms = _dot((x * x).astype(BF16), ind)
    return x * lax.rsqrt(ms + EPS) * gain


def _softplus(x):
    return jnp.maximum(x, 0.0) + jnp.log1p(jnp.exp(-jnp.abs(x)))


def _log_sigmoid(x):
    return -_softplus(-x)


def _cumsum_time(x):
    n = x.shape[0]
    row = lax.broadcasted_iota(jnp.int32, x.shape, 0)
    d = SUBLANES
    while d < n:
        x = x + jnp.where(row >= d, pltpu.roll(x, d, axis=0), 0.0)
        d *= 2
    return x


def _to_time_major(val, slab_ref):
    rows, width = val.shape
    steps = rows // SUBLANES
    for s in range(width // LANES):
        for b in range(SUBLANES):
            slab_ref[s, b * ROW_PITCH:b * ROW_PITCH + steps, :] = (
                val[b * steps:(b + 1) * steps, s * LANES:(s + 1) * LANES])
    return jnp.concatenate(
        [jnp.concatenate([slab_ref[s, pl.ds(t, SUBLANES, stride=ROW_PITCH), :]
                          for s in range(width // LANES)], axis=-1) for t in range(steps)], axis=0)


def _to_batch_major(val, slab_ref):
    rows, width = val.shape
    steps = rows // SUBLANES
    for s in range(width // LANES):
        slab_ref[s, 0:rows, :] = val[:, s * LANES:(s + 1) * LANES]
    return jnp.concatenate(
        [jnp.concatenate([slab_ref[s, pl.ds(b, steps, stride=SUBLANES), :]
                          for s in range(width // LANES)], axis=-1) for b in range(SUBLANES)], axis=0)


def _split3_bf16(c):
    hi = c.astype(BF16).astype(F32)
    r = c - hi
    mid = r.astype(BF16).astype(F32)
    return hi, mid, r - mid


def _keep_lanes(x, lo, hi):
    lane = lax.broadcasted_iota(jnp.int32, x.shape, 1)
    return jnp.where((lane >= lo) & (lane < hi), x.astype(F32), 0.0).astype(BF16)


def _params(*sem):
    return pltpu.CompilerParams(dimension_semantics=sem, vmem_limit_bytes=VMEM_LIMIT)


def _const_spec(shape):
    nd = len(shape)
    return pl.BlockSpec(shape, lambda *_: (0,) * nd, pipeline_mode=pl.Buffered(1))


def _layer_spec(shape, layer):
    nd = len(shape)
    return pl.BlockSpec((None,) + tuple(shape), lambda *_: (layer,) + (0,) * nd,
                        pipeline_mode=pl.Buffered(1))


def _head_mean_matrix(width):
    h = np.arange(width) // HEAD_DIM
    return jnp.asarray((h[:, None] == h[None, :]) / HEAD_DIM, dtype=BF16)


def _swiglu_update(x, g_ref, wg_ref, wu_ref, wd_ref):
    n = _rms(x, g_ref[...]).astype(BF16)
    hg = _dot(n, wg_ref[...])
    hu = _dot(n, wu_ref[...])
    act = (hg * jax.nn.sigmoid(hg) * hu).astype(BF16)
    return x + MACARON * _dot(act, wd_ref[...])


def _ffn_kernel(x_ref, g_ref, wg_ref, wu_ref, wd_ref, o_ref):
    half = x_ref.shape[0] // 2
    for i in range(2):
        rs = slice(i * half, (i + 1) * half)
        o_ref[rs, :] = _swiglu_update(x_ref[rs, :], g_ref, wg_ref, wu_ref, wd_ref)


def _ffn(x2d, layer, norm, wg, wu, wd):
    T, D = x2d.shape
    tm = min(ROW_TILE, T)
    row = pl.BlockSpec((tm, D), lambda i: (i, 0))
    return pl.pallas_call(
        _ffn_kernel, grid=(T // tm,),
        in_specs=[row, _layer_spec((1, D), layer), _layer_spec((D, D_FF), layer),
                  _layer_spec((D, D_FF), layer), _layer_spec((D_FF, D), layer)],
        out_specs=row, out_shape=jax.ShapeDtypeStruct((T, D), F32),
        compiler_params=_params("parallel"), name="ffn",
    )(x2d, norm, wg, wu, wd)


def _mix_ffn_ple_kernel(x_ref, tmaj_ref, bmaj_ref, wt_ref, wb_ref,
                        g_ref, wg_ref, wu_ref, wd_ref, p_ref, pw_ref, pn_ref, gn_ref, gw_ref, o_ref,
                        slab_ref):
    B, ts, D = x_ref.shape
    rows = B * ts
    part_t = _to_batch_major(tmaj_ref[...].astype(F32), slab_ref).astype(BF16)
    part_b = bmaj_ref[...].reshape(rows, bmaj_ref.shape[2])
    x = x_ref[...].reshape(rows, D) + _dot(part_t, wt_ref[...]) + _dot(part_b, wb_ref[...])
    half = rows // 2
    pb = p_ref[...].reshape(rows, PLE_DIM).astype(BF16)
    halves = [_swiglu_update(x[i * half:(i + 1) * half, :], g_ref, wg_ref, wu_ref, wd_ref)
              for i in range(2)]
    outs = []
    for i, xh in enumerate(halves):
        e = _rms(_dot(pb[i * half:(i + 1) * half, :], pw_ref[...]), pn_ref[...])
        gate = jax.nn.sigmoid(_dot(_rms(xh, gn_ref[...]).astype(BF16), gw_ref[...]))
        outs.append(xh + gate * e)
    o_ref[...] = jnp.concatenate(outs, axis=0).reshape(B, ts, D)


def _mix_ffn_ple(x, part_tmaj, part_bmaj, w_tmaj, w_bmaj, layer, ffn, ple):
    B, S, D = x.shape
    ts = TIME_CHUNK
    W = part_tmaj.shape[1]
    xs = pl.BlockSpec((B, ts, D), lambda s: (0, s, 0))
    in_specs = [xs, pl.BlockSpec((B * ts, W), lambda s: (s, 0)),
                pl.BlockSpec((B, ts, W), lambda s: (0, s, 0)),
                _const_spec((W, D)), _const_spec((W, D)),
                _layer_spec((1, D), layer), _layer_spec((D, D_FF), layer),
                _layer_spec((D, D_FF), layer), _layer_spec((D_FF, D), layer),
                pl.BlockSpec((None, B, ts, PLE_DIM), lambda s: (layer, 0, s, 0)),
                _layer_spec((PLE_DIM, D), layer), _layer_spec((1, D), layer),
                _layer_spec((1, D), layer), _layer_spec((D, D), layer)]
    return pl.pallas_call(
        _mix_ffn_ple_kernel, grid=(S // ts,), in_specs=in_specs, out_specs=xs,
        out_shape=jax.ShapeDtypeStruct((B, S, D), F32),
        scratch_shapes=[pltpu.VMEM((W // LANES, B * ts, LANES), F32)],
        compiler_params=_params("parallel"), name="mix_ffn_ple",
    )(x, part_tmaj, part_bmaj, w_tmaj, w_bmaj, *ffn, *ple)


def _bias_lane_maps():
    eq = np.zeros((LANES, LANES), np.float32)
    ek = np.zeros((LANES, LANES), np.float32)
    oq = np.zeros((1, LANES), np.float32)
    ok = np.zeros((1, LANES), np.float32)
    for h in range(FOX_HEADS):
        for i in range(3):
            eq[i * FOX_HEADS + h, BIAS_LANES * h + i] = 1.0
            ek[i * FOX_HEADS + h, BIAS_LANES * h + 3 + i] = -1.0
            oq[0, BIAS_LANES * h + 3 + i] = 1.0
            ok[0, BIAS_LANES * h + i] = 1.0
    return jnp.asarray(eq, BF16), jnp.asarray(ek, BF16), jnp.asarray(oq), jnp.asarray(ok)


def _rg_lru_chunk(xa, ya, cw_ref, cb_ref, wa_ref, ba_ref, wx_ref, bx_ref, lam_ref, xprev_ref, h_ref):
    rows = xa.shape[0]
    halo = (LRU_CONV - 1) * SUBLANES
    half = LRU_WIDTH // 2
    xfull = jnp.concatenate([xprev_ref[...], xa], axis=0)
    xprev_ref[...] = xa[rows - halo:rows, :]
    xc = cb_ref[...]
    for tap in range(LRU_CONV):
        xc = xc + xfull[tap * SUBLANES:tap * SUBLANES + rows, :] * cw_ref[tap:tap + 1, :]
    xb = xc.astype(BF16)

    def gate(w_ref, b_ref):
        z = jnp.concatenate([_dot(xb[:, :half], w_ref[0]), _dot(xb[:, half:], w_ref[1])], axis=-1)
        return jax.nn.sigmoid(z + b_ref[...])

    r = gate(wa_ref, ba_ref)
    i = gate(wx_ref, bx_ref)
    log_a = -LRU_C * r * _softplus(lam_ref[...])
    a = jnp.exp(log_a)
    th = jnp.tanh(log_a)
    b = jnp.sqrt(-2.0 * th / (1.0 - th)) * (i * xc)
    h = h_ref[...]
    hs = []
    for t in range(rows // SUBLANES):
        sl = slice(t * SUBLANES, (t + 1) * SUBLANES)
        h = a[sl, :] * h + b[sl, :]
        hs.append(h)
    h_ref[...] = h
    return jax.nn.gelu(ya) * jnp.concatenate(hs, axis=0)


def _even_in_kernel(x_ref, g_ref, wxyf_ref, wqkv_ref, bf_ref, qg_ref, kg_ref, ind_ref,
                    eqk_ref, oqk_ref, cw_ref, cb_ref, wa_ref, ba_ref, wx_ref, bx_ref, lam_ref,
                    a_out_ref, q_ref, k_ref, qe_ref, ke_ref, v_ref,
                    carry_ref, slab_ref, xprev_ref, h_ref):
    B, ts, D = x_ref.shape
    rows = B * ts

    @pl.when(pl.program_id(0) == 0)
    def _():
        carry_ref[...] = jnp.zeros_like(carry_ref)
        xprev_ref[...] = jnp.zeros_like(xprev_ref)
        h_ref[...] = jnp.zeros_like(h_ref)

    n = _rms(x_ref[...].reshape(rows, D), g_ref[...]).astype(BF16)
    f_cols = _dot(n, wxyf_ref[:, 2 * LRU_WIDTH:])
    xyf = jnp.concatenate([_dot(n, wxyf_ref[:, 0:2 * LRU_WIDTH]), f_cols], axis=-1)
    qkv = _dot(n, wqkv_ref[...])
    xyf = _to_time_major(xyf, slab_ref)
    a_out_ref[...] = _rg_lru_chunk(xyf[:, :LRU_WIDTH], xyf[:, LRU_WIDTH:2 * LRU_WIDTH], cw_ref, cb_ref,
                                   wa_ref, ba_ref, wx_ref, bx_ref, lam_ref, xprev_ref,
                                   h_ref).astype(BF16)

    logf = _log_sigmoid(xyf[:, 2 * LRU_WIDTH:] + bf_ref[...]) * LOG2E
    c = _cumsum_time(logf) + pltpu.repeat(carry_ref[...], ts, axis=0)
    carry_ref[...] = c[rows - SUBLANES:rows, :]
    hi, mid, lo = _split3_bf16(c)
    lane = lax.broadcasted_iota(jnp.int32, c.shape, 1)
    packed = jnp.where(lane < FOX_HEADS, hi, jnp.where(
        lane < 2 * FOX_HEADS, pltpu.roll(mid, FOX_HEADS, axis=1), jnp.where(
            lane < 3 * FOX_HEADS, pltpu.roll(lo, 2 * FOX_HEADS, axis=1), 0.0)))
    packed = _to_batch_major(packed, slab_ref).astype(BF16)
    qke = (_dot(packed, eqk_ref[...]) + oqk_ref[...]).astype(BF16)
    qe_ref[...] = qke[:, :LANES].reshape(B, ts, LANES)
    ke_ref[...] = qke[:, LANES:].reshape(B, ts, LANES)

    ind = ind_ref[...]
    q = _head_rms(qkv[:, :FOX_DIM], ind, qg_ref[...]) * (QK_SCALE * LOG2E)
    k = _head_rms(qkv[:, FOX_DIM:2 * FOX_DIM], ind, kg_ref[...])
    q_ref[...] = q.astype(BF16).reshape(B, ts, FOX_DIM)
    k_ref[...] = k.astype(BF16).reshape(B, ts, FOX_DIM)
    v_ref[...] = qkv[:, 2 * FOX_DIM:].astype(BF16).reshape(B, ts, FOX_DIM)


def _even_inproj(x, g, w_xy, w_qkv, w_f, b_f, qg, kg, lru):
    B, S, D = x.shape
    ts = TIME_CHUNK
    eq, ek, oq, ok = _bias_lane_maps()
    tmaj = pl.BlockSpec((B * ts, LRU_WIDTH), lambda s: (s, 0))
    bmaj = lambda w: pl.BlockSpec((B, ts, w), lambda s: (0, s, 0))
    consts = [g, jnp.concatenate([w_xy, w_f], axis=1), w_qkv, b_f, qg, kg,
              _head_mean_matrix(FOX_DIM), jnp.concatenate([eq, ek], axis=1),
              jnp.concatenate([oq, ok], axis=1), *lru]
    return pl.pallas_call(
        _even_in_kernel, grid=(S // ts,),
        in_specs=[bmaj(D)] + [_const_spec(a.shape) for a in consts],
        out_specs=[tmaj, bmaj(FOX_DIM), bmaj(FOX_DIM), bmaj(LANES), bmaj(LANES), bmaj(FOX_DIM)],
        out_shape=[jax.ShapeDtypeStruct((S * B, LRU_WIDTH), BF16),
                   jax.ShapeDtypeStruct((B, S, FOX_DIM), BF16),
                   jax.ShapeDtypeStruct((B, S, FOX_DIM), BF16),
                   jax.ShapeDtypeStruct((B, S, LANES), BF16),
                   jax.ShapeDtypeStruct((B, S, LANES), BF16),
                   jax.ShapeDtypeStruct((B, S, FOX_DIM), BF16)],
        scratch_shapes=[pltpu.VMEM((SUBLANES, LANES), F32),
                        pltpu.VMEM((2 * LRU_WIDTH // LANES + 1, SUBLANES * ROW_PITCH, LANES), F32),
                        pltpu.VMEM(((LRU_CONV - 1) * SUBLANES, LRU_WIDTH), F32),
                        pltpu.VMEM((SUBLANES, LRU_WIDTH), F32)],
        compiler_params=_params("arbitrary"), name="even_inproj",
    )(x, *consts)


def _vt_kernel(v_ref, o_ref):
    heads, nchunk, _, chunk = o_ref.shape
    vt = v_ref[...].astype(F32).T
    ones = jnp.ones((VT_ROWS - HEAD_DIM, chunk), BF16)
    for h in range(heads):
        for c in range(nchunk):
            o_ref[h, c, 0:HEAD_DIM, :] = vt[h * HEAD_DIM:(h + 1) * HEAD_DIM,
                                            c * chunk:(c + 1) * chunk].astype(BF16)
            o_ref[h, c, HEAD_DIM:VT_ROWS, :] = ones


def _value_transpose(v, chunk):
    B, S, W = v.shape
    heads = W // HEAD_DIM
    tk = min(ATTN_TILE, S)
    chunk = min(chunk, tk)
    return pl.pallas_call(
        _vt_kernel, grid=(B, S // tk),
        in_specs=[pl.BlockSpec((None, tk, W), lambda b, j: (b, j, 0))],
        out_specs=pl.BlockSpec((None, heads, tk // chunk, VT_ROWS, chunk),
                               lambda b, j: (b, 0, j, 0, 0)),
        out_shape=jax.ShapeDtypeStruct((B, heads, S // chunk, VT_ROWS, chunk), BF16),
        compiler_params=_params("parallel", "parallel"), name="value_transpose",
    )(v)


def _fox_kernel(q_ref, qe_ref, k_ref, ke_ref, vt_ref, o_ref, acc_ref, s_ref):
    tq = q_ref.shape[0]
    qi = pl.program_id(1)
    qe = qe_ref[...]
    qq = []
    for h in range(FOX_HEADS):
        half = h % 2
        blk = q_ref[:, (h // 2) * LANES:(h // 2 + 1) * LANES]
        qq.append(jnp.concatenate(
            [_keep_lanes(blk, half * HEAD_DIM, (half + 1) * HEAD_DIM),
             _keep_lanes(qe, BIAS_LANES * h, BIAS_LANES * (h + 1))], axis=-1))
    key = lax.broadcasted_iota(jnp.int32, (tq, tq), 0)
    qry = lax.broadcasted_iota(jnp.int32, (tq, tq), 1)
    acc_ref[...] = jnp.zeros_like(acc_ref)

    def update(j, m_run, masked):
        ks = pl.ds(pl.multiple_of(j * tq, tq), tq)
        ke = ke_ref[ks, :]
        for h in range(FOX_HEADS):
            kk = jnp.concatenate([k_ref[ks, (h // 2) * LANES:(h // 2 + 1) * LANES], ke], axis=-1)
            s = _dot_nt(kk, qq[h])
            if masked:
                s = jnp.where(key <= qry, s, -jnp.inf)
            s_ref[h] = s
        new = []
        for h in range(FOX_HEADS):
            m_new = jnp.maximum(m_run[h], jnp.max(s_ref[h], axis=0, keepdims=True))
            alpha = jnp.exp2(m_run[h] - m_new)
            p = jnp.exp2(s_ref[h] - m_new).astype(BF16)
            acc_ref[h] = alpha * acc_ref[h] + _dot(vt_ref[h, j], p)
            new.append(m_new)
        return tuple(new)

    init = (jnp.full((1, tq), -jnp.inf, F32),) * FOX_HEADS
    m_run = lax.fori_loop(0, qi, lambda j, m: update(j, m, False), init)
    update(qi, m_run, True)
    for pair in range(FOX_HEADS // 2):
        out = [acc_ref[h, 0:HEAD_DIM, :] / acc_ref[h, HEAD_DIM:HEAD_DIM + 1, :]
               for h in (2 * pair, 2 * pair + 1)]
        o_ref[:, pair * LANES:(pair + 1) * LANES] = jnp.concatenate(out, axis=0).T.astype(BF16)


def _fox_attention(q, qe, k, ke, vt):
    B, S, _ = q.shape
    nk, tq = vt.shape[2], vt.shape[4]
    qblk = lambda w: pl.BlockSpec((None, tq, w), lambda b, i: (b, i, 0))
    kblk = lambda w: pl.BlockSpec((None, S, w), lambda b, i: (b, 0, 0))
    return pl.pallas_call(
        _fox_kernel, grid=(B, S // tq),
        in_specs=[qblk(FOX_DIM), qblk(LANES), kblk(FOX_DIM), kblk(LANES),
                  pl.BlockSpec((None, FOX_HEADS, nk, VT_ROWS, tq), lambda b, i: (b, 0, 0, 0, 0))],
        out_specs=qblk(FOX_DIM),
        out_shape=jax.ShapeDtypeStruct((B, S, FOX_DIM), BF16),
        scratch_shapes=[pltpu.VMEM((FOX_HEADS, VT_ROWS, tq), F32),
                        pltpu.VMEM((FOX_HEADS, tq, tq), F32)],
        compiler_params=_params("parallel", "arbitrary"), name="fox_attention",
    )(q, qe, k, ke, vt)


def _rope_table_kernel(inv_ref, cos_ref, sin_ref):
    rows = cos_ref.shape[0]
    pos = pl.program_id(0) * rows + lax.broadcasted_iota(jnp.int32, cos_ref.shape, 0)
    lane = lax.broadcasted_iota(jnp.int32, cos_ref.shape, 1)
    ang = pos.astype(F32) * inv_ref[...]
    cos_ref[...] = jnp.cos(ang)
    sin_ref[...] = jnp.where((lane & (HEAD_DIM - 1)) < HEAD_DIM // 2, -1.0, 1.0) * jnp.sin(ang)


def _rope_tables(S):
    half = HEAD_DIM // 2
    inv = jnp.power(ROPE_THETA, -jnp.arange(half, dtype=F32) / half)
    inv4 = jnp.tile(inv, LANES // half)[None, :]
    rows = min(ROW_TILE, S)
    blk = pl.BlockSpec((rows, LANES), lambda i: (i, 0))
    return pl.pallas_call(
        _rope_table_kernel, grid=(S // rows,),
        in_specs=[pl.BlockSpec((1, LANES), lambda i: (0, 0))], out_specs=[blk, blk],
        out_shape=[jax.ShapeDtypeStruct((S, LANES), F32)] * 2,
        compiler_params=_params("parallel"), name="rope_tables",
    )(inv4)


def _rope(x, cos, sin):
    half = HEAD_DIM // 2
    lane = lax.broadcasted_iota(jnp.int32, x.shape, 1)
    first = (lane & (HEAD_DIM - 1)) < half
    swapped = jnp.where(first, pltpu.roll(x, LANES - half, axis=1), pltpu.roll(x, half, axis=1))
    return x * cos + swapped * sin


def _dup_halves(x):
    lane = lax.broadcasted_iota(jnp.int32, x.shape, 1)
    r = pltpu.roll(x, HEAD_DIM, axis=1)
    lo = lane < HEAD_DIM
    return jnp.concatenate([jnp.where(lo, x, r), jnp.where(lo, r, x)], axis=-1)


def _s5_chunk(u, bm_ref, cm_ref, ar_ref, ai_ref, d_ref, gw_ref, gb_ref, h_ref, state_ref):
    rows = u.shape[0]
    half = h_ref.shape[2] // 2
    ub = u.astype(BF16)
    for g in range(S5_LANE_GROUPS):
        h_ref[g] = _dot(ub[:, g * LANES:(g + 1) * LANES], bm_ref[g])
    ys = []
    for g in range(S5_LANE_GROUPS):
        ar = jnp.broadcast_to(ar_ref[g], (SUBLANES, half))
        ai = jnp.broadcast_to(ai_ref[g], (SUBLANES, half))
        hr = state_ref[g, :, 0:half]
        hi = state_ref[g, :, half:2 * half]
        for t in range(rows // SUBLANES):
            sl = slice(t * SUBLANES, (t + 1) * SUBLANES)
            nr = ar * hr - ai * hi + h_ref[g, sl, 0:half]
            ni = ar * hi + ai * hr + h_ref[g, sl, half:2 * half]
            h_ref[g, sl, 0:half] = nr
            h_ref[g, sl, half:2 * half] = ni
            hr, hi = nr, ni
        state_ref[g, :, 0:half] = hr
        state_ref[g, :, half:2 * half] = hi
        ys.append(_dot(h_ref[g].astype(BF16), cm_ref[g]))
    y = jnp.concatenate(ys, axis=-1) + d_ref[...] * u
    z = jax.nn.gelu(y)
    return z * jax.nn.sigmoid(_dot(z.astype(BF16), gw_ref[...]) + gb_ref[...])


def _odd_in_kernel(x_ref, g_ref, wq_ref, wkv_ref, wu_ref, qg_ref, kg_ref, ind_ref,
                   bm_ref, cm_ref, ar_ref, ai_ref, d_ref, gw_ref, gb_ref,
                   cos_ref, sin_ref, q_ref, kd_ref, v_ref, d_out_ref, slab_ref, h_ref, state_ref):
    B, ts, D = x_ref.shape
    rows = B * ts

    @pl.when(pl.program_id(0) == 0)
    def _():
        state_ref[...] = jnp.zeros_like(state_ref)

    n = _rms(x_ref[...].reshape(rows, D), g_ref[...]).astype(BF16)
    u = _to_time_major(_dot(n, wu_ref[...]), slab_ref)
    d_out_ref[...] = _s5_chunk(u, bm_ref, cm_ref, ar_ref, ai_ref, d_ref, gw_ref, gb_ref,
                               h_ref, state_ref).astype(BF16)
    cos = jnp.concatenate([cos_ref[...]] * B, axis=0)
    sin = jnp.concatenate([sin_ref[...]] * B, axis=0)
    q = _head_rms(_dot(n, wq_ref[...]), ind_ref[...], qg_ref[...])
    q = jnp.concatenate([_rope(q[:, blk * LANES:(blk + 1) * LANES], cos, sin)
                         for blk in range(SWA_DIM // LANES)], axis=-1) * (QK_SCALE * LOG2E)
    q_ref[...] = q.astype(BF16).reshape(B, ts, SWA_DIM)
    kv = _dot(n, wkv_ref[...])
    k = _head_rms(kv[:, :LANES], ind_ref[0:LANES, 0:LANES], kg_ref[...])
    kd_ref[...] = _dup_halves(_rope(k, cos, sin)).astype(BF16).reshape(B, ts, 2 * LANES)
    v_ref[...] = kv[:, LANES:].astype(BF16).reshape(B, ts, LANES)


def _odd_inproj(x, g, w_q, w_kv, w_u, qg, kg, s5, cos, sin):
    B, S, D = x.shape
    ts = TIME_CHUNK
    bmaj = lambda w: pl.BlockSpec((B, ts, w), lambda s: (0, s, 0))
    tab = pl.BlockSpec((ts, LANES), lambda s: (s, 0))
    consts = [g, w_q, w_kv, w_u, qg, kg, _head_mean_matrix(SWA_DIM), *s5]
    nstate = s5[0].shape[2]
    return pl.pallas_call(
        _odd_in_kernel, grid=(S // ts,),
        in_specs=[bmaj(D)] + [_const_spec(a.shape) for a in consts] + [tab, tab],
        out_specs=[bmaj(SWA_DIM), bmaj(2 * LANES), bmaj(LANES),
                   pl.BlockSpec((B * ts, S5_WIDTH), lambda s: (s, 0))],
        out_shape=[jax.ShapeDtypeStruct((B, S, SWA_DIM), BF16),
                   jax.ShapeDtypeStruct((B, S, 2 * LANES), BF16),
                   jax.ShapeDtypeStruct((B, S, LANES), BF16),
                   jax.ShapeDtypeStruct((S * B, S5_WIDTH), BF16)],
        scratch_shapes=[pltpu.VMEM((S5_WIDTH // LANES, SUBLANES * ROW_PITCH, LANES), F32),
                        pltpu.VMEM((S5_LANE_GROUPS, B * ts, nstate), F32),
                        pltpu.VMEM((S5_LANE_GROUPS, SUBLANES, nstate), F32)],
        compiler_params=_params("arbitrary"), name="odd_inproj",
    )(x, *consts, cos, sin)


def _swa_kernel(q_ref, kd_ref, vt_ref, sink_ref, o_ref):
    tq = q_ref.shape[0]
    W = SWA_WINDOW
    G = SWA_GROUP
    base = pl.program_id(1) * tq
    key = lax.broadcasted_iota(jnp.int32, (2 * W, G * W), 0)
    qoff = lax.broadcasted_iota(jnp.int32, (2 * W, G * W), 1) & (W - 1)
    cgrp = lax.broadcasted_iota(jnp.int32, (1, G * W), 1) // W
    sinks = sink_ref[...] * LOG2E
    blocks = [(n, kvh) for n in range(tq // W) for kvh in range(SWA_KV_HEADS)]
    kstarts = [pl.multiple_of(jnp.maximum(base + n * W - W, 0), W) for n in range(tq // W)]
    scores = []
    for n, kvh in blocks:
        parts = []
        for g in range(G):
            head = kvh * G + g
            blk = q_ref[n * W:(n + 1) * W, (head // 2) * LANES:(head // 2 + 1) * LANES]
            half = head % 2
            parts.append(_keep_lanes(blk, half * HEAD_DIM, (half + 1) * HEAD_DIM))
        scores.append(_dot_nt(kd_ref[pl.ds(kstarts[n], 2 * W), kvh * LANES:(kvh + 1) * LANES],
                              jnp.concatenate(parts, axis=0)))
    for (n, kvh), s in zip(blocks, scores):
        rows = slice(n * W, (n + 1) * W)
        chunk = kstarts[n] // W
        diff = (base + n * W + qoff) - (kstarts[n] + key)
        s = jnp.where((diff >= 0) & (diff < W), s, -jnp.inf)
        sink = jnp.zeros((1, G * W), F32)
        for g in range(G):
            head = kvh * G + g
            sink = jnp.where(cgrp == g, sinks[:, head:head + 1], sink)
        m = jnp.maximum(jnp.max(s, axis=0, keepdims=True), sink)
        p = jnp.exp2(s - m).astype(BF16)
        acc = _dot(vt_ref[kvh, chunk], p[0:W, :]) + _dot(vt_ref[kvh, chunk + 1], p[W:2 * W, :])
        den = acc[HEAD_DIM:HEAD_DIM + 1, :] + jnp.exp2(sink - m)
        o = acc[0:HEAD_DIM, :] / den
        for pair in range(G // 2):
            both = jnp.concatenate([o[:, (2 * pair) * W:(2 * pair + 1) * W],
                                    o[:, (2 * pair + 1) * W:(2 * pair + 2) * W]], axis=0)
            lb = kvh * (G // 2) + pair
            o_ref[rows, lb * LANES:(lb + 1) * LANES] = both.T.astype(BF16)


def _swa_attention(q, kd, vt, sinks):
    B, S, _ = q.shape
    tq = min(ATTN_TILE, S)
    return pl.pallas_call(
        _swa_kernel, grid=(B, S // tq),
        in_specs=[pl.BlockSpec((None, tq, SWA_DIM), lambda b, i: (b, i, 0)),
                  pl.BlockSpec((None, S, 2 * LANES), lambda b, i: (b, 0, 0)),
                  pl.BlockSpec((None,) + vt.shape[1:], lambda b, i: (b, 0, 0, 0, 0)),
                  pl.BlockSpec((1, SWA_HEADS), lambda b, i: (0, 0))],
        out_specs=pl.BlockSpec((None, tq, SWA_DIM), lambda b, i: (b, i, 0)),
        out_shape=jax.ShapeDtypeStruct((B, S, SWA_DIM), BF16),
        compiler_params=_params("parallel", "arbitrary"), name="swa_attention",
    )(q, kd, vt, sinks)


def _s5_prep_kernel(lr_ref, li_ref, ldt_ref, br_ref, bi_ref, ar_ref, ai_ref, bbr_ref, bbi_ref):
    lr, li = lr_ref[...], li_ref[...]
    dt = jnp.exp(ldt_ref[...])
    mag = jnp.exp(lr * dt)
    ar = mag * jnp.cos(li * dt)
    ai = mag * jnp.sin(li * dt)
    den = lr * lr + li * li
    cr = ((ar - 1.0) * lr + ai * li) / den
    ci = (ai * lr - (ar - 1.0) * li) / den
    br, bi = br_ref[...], bi_ref[...]
    ar_ref[...] = ar
    ai_ref[...] = ai
    bbr_ref[...] = cr * br - ci * bi
    bbi_ref[...] = cr * bi + ci * br


def _s5_prep(lam_re, lam_im, log_dt, b_re, b_im):
    G, P, C = b_re.shape
    rep = lambda a: jnp.repeat(a, C, axis=0)
    bt = lambda a: a.transpose(0, 2, 1).reshape(G * C, P)
    ldt = jnp.broadcast_to(log_dt[:, None], (G, P))
    full = pl.BlockSpec((G * C, P), lambda: (0, 0))
    outs = pl.pallas_call(
        _s5_prep_kernel, in_specs=[full] * 5, out_specs=[full] * 4,
        out_shape=[jax.ShapeDtypeStruct((G * C, P), F32)] * 4, name="s5_prep",
    )(rep(lam_re), rep(lam_im), rep(ldt), bt(b_re), bt(b_im))
    ar, ai, bbr, bbi = [o.reshape(G, C, P) for o in outs]
    return ar[:, 0], ai[:, 0], bbr, bbi


def _s5_matrices(ar, ai, bbr, bbi, c_re, c_im):
    L, GL = S5_LANE_GROUPS, S5_GROUPS // S5_LANE_GROUPS
    C, P = S5_GROUP, S5_STATE
    eye = jnp.eye(GL, dtype=F32)

    def inmap(b):
        return jnp.einsum("lgcp,gh->lgchp", b.reshape(L, GL, C, P), eye).reshape(L, GL * C, GL * P)

    def outmap(c):
        return jnp.einsum("lgcp,gh->lgphc", c.reshape(L, GL, C, P), eye).reshape(L, GL * P, GL * C)

    bmat = jnp.concatenate([inmap(bbr), inmap(bbi)], axis=2).astype(BF16)
    cmat = jnp.concatenate([outmap(c_re), outmap(-c_im)], axis=1).astype(BF16)
    a_r = ar.reshape(L, 1, GL * P)
    a_i = ai.reshape(L, 1, GL * P)
    return bmat, cmat, a_r, a_i


def _block_diag_pairs(w):
    nb, bs, _ = w.shape
    half = nb // 2
    eye = jnp.eye(half, dtype=w.dtype)
    out = jnp.einsum("thij,hk->thikj", w.reshape(2, half, bs, bs), eye)
    return out.reshape(2, half * bs, half * bs).astype(BF16)


def kernel(x, p, ffn1_norm, ffn1_wg, ffn1_wu, ffn1_wd, mix_norm, ffn2_norm, ffn2_wg, ffn2_wu, ffn2_wd, ple_w, ple_norm, ple_gate_norm, ple_gate_w, ev_w_in, lru_conv_w, lru_conv_b, lru_wa, lru_ba, lru_wx, lru_bx, lru_lambda, fox_bf, fox_q_norm, fox_k_norm, ev_w_out, od_w_in, swa_q_norm, swa_k_norm, swa_sinks, s5_lambda_re, s5_lambda_im, s5_log_dt, s5_b_re, s5_b_im, s5_c_re, s5_c_im, s5_d, s5_glu_w, s5_glu_b, od_w_out):
    B, S, D = x.shape
    depth = p.shape[0]
    assert B == SUBLANES and D == D_MODEL and S % TIME_CHUNK == 0
    T = B * S
    bf = lambda a: a.astype(BF16)
    row = lambda a: a[:, None, :]
    per_head = lambda gain, heads: jnp.tile(gain, heads)[None, :]

    f1 = (row(ffn1_norm), bf(ffn1_wg), bf(ffn1_wu), bf(ffn1_wd))
    f2 = (row(ffn2_norm), bf(ffn2_wg), bf(ffn2_wu), bf(ffn2_wd))
    ple = (p, bf(ple_w), row(ple_norm), row(ple_gate_norm), bf(ple_gate_w))
    cos, sin = _rope_tables(S)

    for i in range(depth):
        j = i // 2
        x = _ffn(x.reshape(T, D), i, *f1).reshape(B, S, D)
        g = mix_norm[i][None, :]
        if i % 2 == 0:
            w_in = bf(ev_w_in[j])
            o1, o2 = 2 * LRU_WIDTH, 2 * LRU_WIDTH + 3 * FOX_DIM
            w_f = jnp.pad(w_in[:, o2:], ((0, 0), (0, LANES - FOX_HEADS)))
            b_f = jnp.pad(fox_bf[j], (0, LANES - FOX_HEADS))[None, :]
            lru = (lru_conv_w[j], lru_conv_b[j][None, :],
                   _block_diag_pairs(lru_wa[j]), lru_ba[j][None, :],
                   _block_diag_pairs(lru_wx[j]), lru_bx[j][None, :], lru_lambda[j][None, :])
            a_out, q, k, qe, ke, v = _even_inproj(
                x, g, w_in[:, :o1], w_in[:, o1:o2], w_f, b_f,
                per_head(fox_q_norm[j], FOX_HEADS), per_head(fox_k_norm[j], FOX_HEADS), lru)
            b_out = _fox_attention(q, qe, k, ke, _value_transpose(v, ATTN_TILE))
            w_out = bf(ev_w_out[j])
            mixed = (a_out, b_out, w_out[:LRU_WIDTH], w_out[LRU_WIDTH:])
        else:
            w_in = bf(od_w_in[j])
            kvd = SWA_KV_HEADS * HEAD_DIM
            o1, o2 = SWA_DIM, SWA_DIM + 2 * kvd
            ar, ai, bbr, bbi = _s5_prep(s5_lambda_re[j], s5_lambda_im[j], s5_log_dt[j],
                                        s5_b_re[j], s5_b_im[j])
            bmat, cmat, a_r, a_i = _s5_matrices(ar, ai, bbr, bbi, s5_c_re[j], s5_c_im[j])
            s5 = (bmat, cmat, a_r, a_i, s5_d[j][None, :], bf(s5_glu_w[j]), s5_glu_b[j][None, :])
            q, kd, v, d_out = _odd_inproj(
                x, g, w_in[:, :o1], w_in[:, o1:o2], w_in[:, o2:],
                per_head(swa_q_norm[j], SWA_HEADS), per_head(swa_k_norm[j], SWA_KV_HEADS), s5,
                cos, sin)
            c_out = _swa_attention(q, kd, _value_transpose(v, SWA_WINDOW), swa_sinks[j][None, :])
            w_out = bf(od_w_out[j])
            mixed = (d_out, c_out, w_out[SWA_DIM:], w_out[:SWA_DIM])
        x = _mix_ffn_ple(x, *mixed, i, f2, ple)
    return x
```

```python
import numpy as np

import jax
import jax.numpy as jnp
from jax import lax
from jax.experimental import pallas as pl
from jax.experimental.pallas import tpu as pltpu

F32 = jnp.float32
BF16 = jnp.bfloat16

D_MODEL = 1024
HEAD_DIM = 64
LRU_WIDTH = 512
LRU_CONV = 4
LRU_C = 8.0
FOX_HEADS = 8
FOX_DIM = 512
SWA_HEADS = 8
SWA_KV_HEADS = 2
SWA_GROUP = SWA_HEADS // SWA_KV_HEADS
SWA_DIM = 512
SWA_WINDOW = 128
S5_WIDTH = 512
S5_GROUP = 16
S5_GROUPS = 32
S5_STATE = 64
D_FF = 2816
PLE_DIM = 256
ROPE_THETA = 10000.0
EPS = 1e-6
MACARON = 0.5
QK_SCALE = HEAD_DIM ** -0.5
LOG2E = 1.4426950408889634

SUBLANES = 8
LANES = 128
TIME_CHUNK = 64
ROW_PITCH = TIME_CHUNK + 8
ROW_TILE = SUBLANES * TIME_CHUNK
ATTN_TILE = 512
S5_LANE_GROUPS = 4
BIAS_LANES = 6
VT_ROWS = 80
VMEM_LIMIT = 56 * 1024 * 1024


def _dot(a, b):
    return jnp.dot(a, b, preferred_element_type=F32)


def _dot_nt(a, b):
    return lax.dot_general(a, b, (((1,), (1,)), ((), ())), preferred_element_type=F32)


def _rms(x, g):
    ms = jnp.mean(x * x, axis=-1, keepdims=True)
    return x * lax.rsqrt(ms + EPS) * g


def _head_rms(x, ind, gain):
    ms = _dot((x * x).astype(BF16), ind)
    return x * lax.rsqrt(ms + EPS) * gain


def _softplus(x):
    return jnp.maximum(x, 0.0) + jnp.log1p(jnp.exp(-jnp.abs(x)))


def _log_sigmoid(x):
    return -_softplus(-x)


def _cumsum_time(x):
    n = x.shape[0]
    row = lax.broadcasted_iota(jnp.int32, x.shape, 0)
    d = SUBLANES
    while d < n:
        x = x + jnp.where(row >= d, pltpu.roll(x, d, axis=0), 0.0)
        d *= 2
    return x


def _to_time_major(val, slab_ref):
    rows, width = val.shape
    steps = rows // SUBLANES
    for s in range(width // LANES):
        for b in range(SUBLANES):
            slab_ref[s, b * ROW_PITCH:b * ROW_PITCH + steps, :] = (
                val[b * steps:(b + 1) * steps, s * LANES:(s + 1) * LANES])
    return jnp.concatenate(
        [jnp.concatenate([slab_ref[s, pl.ds(t, SUBLANES, stride=ROW_PITCH), :]
                          for s in range(width // LANES)], axis=-1) for t in range(steps)], axis=0)


def _to_batch_major(val, slab_ref):
    rows, width = val.shape
    steps = rows // SUBLANES
    for s in range(width // LANES):
        slab_ref[s, 0:rows, :] = val[:, s * LANES:(s + 1) * LANES]
    return jnp.concatenate(
        [jnp.concatenate([slab_ref[s, pl.ds(b, steps, stride=SUBLANES), :]
                          for s in range(width // LANES)], axis=-1) for b in range(SUBLANES)], axis=0)


def _split3_bf16(c):
    hi = c.astype(BF16).astype(F32)
    r = c - hi
    mid = r.astype(BF16).astype(F32)
    return hi, mid, r - mid


def _keep_lanes(x, lo, hi):
    lane = lax.broadcasted_iota(jnp.int32, x.shape, 1)
    return jnp.where((lane >= lo) & (lane < hi), x.astype(F32), 0.0).astype(BF16)


def _params(*sem):
    return pltpu.CompilerParams(dimension_semantics=sem, vmem_limit_bytes=VMEM_LIMIT)


def _const_spec(shape):
    nd = len(shape)
    return pl.BlockSpec(shape, lambda *_: (0,) * nd, pipeline_mode=pl.Buffered(1))


def _layer_spec(shape, layer):
    nd = len(shape)
    return pl.BlockSpec((None,) + tuple(shape), lambda *_: (layer,) + (0,) * nd,
                        pipeline_mode=pl.Buffered(1))


def _head_mean_matrix(width):
    h = np.arange(width) // HEAD_DIM
    return jnp.asarray((h[:, None] == h[None, :]) / HEAD_DIM, dtype=BF16)


def _swiglu_update(x, g_ref, wg_ref, wu_ref, wd_ref):
    n = _rms(x, g_ref[...]).astype(BF16)
    hg = _dot(n, wg_ref[...])
    hu = _dot(n, wu_ref[...])
    act = (hg * jax.nn.sigmoid(hg) * hu).astype(BF16)
    return x + MACARON * _dot(act, wd_ref[...])


def _ffn_kernel(x_ref, g_ref, wg_ref, wu_ref, wd_ref, o_ref):
    half = x_ref.shape[0] // 2
    for i in range(2):
        rs = slice(i * half, (i + 1) * half)
        o_ref[rs, :] = _swiglu_update(x_ref[rs, :], g_ref, wg_ref, wu_ref, wd_ref)


def _ffn(x2d, layer, norm, wg, wu, wd):
    T, D = x2d.shape
    tm = min(ROW_TILE, T)
    row = pl.BlockSpec((tm, D), lambda i: (i, 0))
    return pl.pallas_call(
        _ffn_kernel, grid=(T // tm,),
        in_specs=[row, _layer_spec((1, D), layer), _layer_spec((D, D_FF), layer),
                  _layer_spec((D, D_FF), layer), _layer_spec((D_FF, D), layer)],
        out_specs=row, out_shape=jax.ShapeDtypeStruct((T, D), F32),
        compiler_params=_params("parallel"), name="ffn",
    )(x2d, norm, wg, wu, wd)


def _mix_ffn_ple_kernel(x_ref, tmaj_ref, bmaj_ref, wt_ref, wb_ref,
                        g_ref, wg_ref, wu_ref, wd_ref, p_ref, pw_ref, pn_ref, gn_ref, gw_ref, o_ref,
                        slab_ref):
    B, ts, D = x_ref.shape
    rows = B * ts
    part_t = _to_batch_major(tmaj_ref[...].astype(F32), slab_ref).astype(BF16)
    part_b = bmaj_ref[...].reshape(rows, bmaj_ref.shape[2])
    x = x_ref[...].reshape(rows, D) + _dot(part_t, wt_ref[...]) + _dot(part_b, wb_ref[...])
    half = rows // 2
    pb = p_ref[...].reshape(rows, PLE_DIM).astype(BF16)
    halves = [_swiglu_update(x[i * half:(i + 1) * half, :], g_ref, wg_ref, wu_ref, wd_ref)
              for i in range(2)]
    outs = []
    for i, xh in enumerate(halves):
        e = _rms(_dot(pb[i * half:(i + 1) * half, :], pw_ref[...]), pn_ref[...])
        gate = jax.nn.sigmoid(_dot(_rms(xh, gn_ref[...]).astype(BF16), gw_ref[...]))
        outs.append(xh + gate * e)
    o_ref[...] = jnp.concatenate(outs, axis=0).reshape(B, ts, D)


def _mix_ffn_ple(x, part_tmaj, part_bmaj, w_tmaj, w_bmaj, layer, ffn, ple):
    B, S, D = x.shape
    ts = TIME_CHUNK
    W = part_tmaj.shape[1]
    xs = pl.BlockSpec((B, ts, D), lambda s: (0, s, 0))
    in_specs = [xs, pl.BlockSpec((B * ts, W), lambda s: (s, 0)),
                pl.BlockSpec((B, ts, W), lambda s: (0, s, 0)),
                _const_spec((W, D)), _const_spec((W, D)),
                _layer_spec((1, D), layer), _layer_spec((D, D_FF), layer),
                _layer_spec((D, D_FF), layer), _layer_spec((D_FF, D), layer),
                pl.BlockSpec((None, B, ts, PLE_DIM), lambda s: (layer, 0, s, 0)),
                _layer_spec((PLE_DIM, D), layer), _layer_spec((1, D), layer),
                _layer_spec((1, D), layer), _layer_spec((D, D), layer)]
    return pl.pallas_call(
        _mix_ffn_ple_kernel, grid=(S // ts,), in_specs=in_specs, out_specs=xs,
        out_shape=jax.ShapeDtypeStruct((B, S, D), F32),
        scratch_shapes=[pltpu.VMEM((W // LANES, B * ts, LANES), F32)],
        compiler_params=_params("parallel"), name="mix_ffn_ple",
    )(x, part_tmaj, part_bmaj, w_tmaj, w_bmaj, *ffn, *ple)


def _bias_lane_maps():
    eq = np.zeros((LANES, LANES), np.float32)
    ek = np.zeros((LANES, LANES), np.float32)
    oq = np.zeros((1, LANES), np.float32)
    ok = np.zeros((1, LANES), np.float32)
    for h in range(FOX_HEADS):
        for i in range(3):
            eq[i * FOX_HEADS + h, BIAS_LANES * h + i] = 1.0
            ek[i * FOX_HEADS + h, BIAS_LANES * h + 3 + i] = -1.0
            oq[0, BIAS_LANES * h + 3 + i] = 1.0
            ok[0, BIAS_LANES * h + i] = 1.0
    return jnp.asarray(eq, BF16), jnp.asarray(ek, BF16), jnp.asarray(oq), jnp.asarray(ok)


def _rg_lru_chunk(xa, ya, cw_ref, cb_ref, wa_ref, ba_ref, wx_ref, bx_ref, lam_ref, xprev_ref, h_ref):
    rows = xa.shape[0]
    halo = (LRU_CONV - 1) * SUBLANES
    half = LRU_WIDTH // 2
    xfull = jnp.concatenate([xprev_ref[...], xa], axis=0)
    xprev_ref[...] = xa[rows - halo:rows, :]
    xc = cb_ref[...]
    for tap in range(LRU_CONV):
        xc = xc + xfull[tap * SUBLANES:tap * SUBLANES + rows, :] * cw_ref[tap:tap + 1, :]
    xb = xc.astype(BF16)

    def gate(w_ref, b_ref):
        z = jnp.concatenate([_dot(xb[:, :half], w_ref[0]), _dot(xb[:, half:], w_ref[1])], axis=-1)
        return jax.nn.sigmoid(z + b_ref[...])

    r = gate(wa_ref, ba_ref)
    i = gate(wx_ref, bx_ref)
    log_a = -LRU_C * r * _softplus(lam_ref[...])
    a = jnp.exp(log_a)
    th = jnp.tanh(log_a)
    b = jnp.sqrt(-2.0 * th / (1.0 - th)) * (i * xc)
    h = h_ref[...]
    hs = []
    for t in range(rows // SUBLANES):
        sl = slice(t * SUBLANES, (t + 1) * SUBLANES)
        h = a[sl, :] * h + b[sl, :]
        hs.append(h)
    h_ref[...] = h
    return jax.nn.gelu(ya) * jnp.concatenate(hs, axis=0)


def _even_in_kernel(x_ref, g_ref, wxyf_ref, wqkv_ref, bf_ref, qg_ref, kg_ref, ind_ref,
                    eqk_ref, oqk_ref, cw_ref, cb_ref, wa_ref, ba_ref, wx_ref, bx_ref, lam_ref,
                    a_out_ref, q_ref, k_ref, qe_ref, ke_ref, v_ref,
                    carry_ref, slab_ref, xprev_ref, h_ref):
    B, ts, D = x_ref.shape
    rows = B * ts

    @pl.when(pl.program_id(0) == 0)
    def _():
        carry_ref[...] = jnp.zeros_like(carry_ref)
        xprev_ref[...] = jnp.zeros_like(xprev_ref)
        h_ref[...] = jnp.zeros_like(h_ref)

    n = _rms(x_ref[...].reshape(rows, D), g_ref[...]).astype(BF16)
    f_cols = _dot(n, wxyf_ref[:, 2 * LRU_WIDTH:])
    xyf = jnp.concatenate([_dot(n, wxyf_ref[:, 0:2 * LRU_WIDTH]), f_cols], axis=-1)
    qkv = _dot(n, wqkv_ref[...])
    xyf = _to_time_major(xyf, slab_ref)
    a_out_ref[...] = _rg_lru_chunk(xyf[:, :LRU_WIDTH], xyf[:, LRU_WIDTH:2 * LRU_WIDTH], cw_ref, cb_ref,
                                   wa_ref, ba_ref, wx_ref, bx_ref, lam_ref, xprev_ref,
                                   h_ref).astype(BF16)

    logf = _log_sigmoid(xyf[:, 2 * LRU_WIDTH:] + bf_ref[...]) * LOG2E
    c = _cumsum_time(logf) + jnp.concatenate([carry_ref[...]] * ts, axis=0)
    carry_ref[...] = c[rows - SUBLANES:rows, :]
    hi, mid, lo = _split3_bf16(c)
    lane = lax.broadcasted_iota(jnp.int32, c.shape, 1)
    packed = jnp.where(lane < FOX_HEADS, hi, jnp.where(
        lane < 2 * FOX_HEADS, pltpu.roll(mid, FOX_HEADS, axis=1), jnp.where(
            lane < 3 * FOX_HEADS, pltpu.roll(lo, 2 * FOX_HEADS, axis=1), 0.0)))
    packed = _to_batch_major(packed, slab_ref).astype(BF16)
    qke = (_dot(packed, eqk_ref[...]) + oqk_ref[...]).astype(BF16)
    qe_ref[...] = qke[:, :LANES].reshape(B, ts, LANES)
    ke_ref[...] = qke[:, LANES:].reshape(B, ts, LANES)

    ind = ind_ref[...]
    q = _head_rms(qkv[:, :FOX_DIM], ind, qg_ref[...]) * (QK_SCALE * LOG2E)
    k = _head_rms(qkv[:, FOX_DIM:2 * FOX_DIM], ind, kg_ref[...])
    q_ref[...] = q.astype(BF16).reshape(B, ts, FOX_DIM)
    k_ref[...] = k.astype(BF16).reshape(B, ts, FOX_DIM)
    v_ref[...] = qkv[:, 2 * FOX_DIM:].astype(BF16).reshape(B, ts, FOX_DIM)


def _even_inproj(x, g, w_xy, w_qkv, w_f, b_f, qg, kg, lru):
    B, S, D = x.shape
    ts = TIME_CHUNK
    eq, ek, oq, ok = _bias_lane_maps()
    tmaj = pl.BlockSpec((B * ts, LRU_WIDTH), lambda s: (s, 0))
    bmaj = lambda w: pl.BlockSpec((B, ts, w), lambda s: (0, s, 0))
    consts = [g, jnp.concatenate([w_xy, w_f], axis=1), w_qkv, b_f, qg, kg,
              _head_mean_matrix(FOX_DIM), jnp.concatenate([eq, ek], axis=1),
              jnp.concatenate([oq, ok], axis=1), *lru]
    return pl.pallas_call(
        _even_in_kernel, grid=(S // ts,),
        in_specs=[bmaj(D)] + [_const_spec(a.shape) for a in consts],
        out_specs=[tmaj, bmaj(FOX_DIM), bmaj(FOX_DIM), bmaj(LANES), bmaj(LANES), bmaj(FOX_DIM)],
        out_shape=[jax.ShapeDtypeStruct((S * B, LRU_WIDTH), BF16),
                   jax.ShapeDtypeStruct((B, S, FOX_DIM), BF16),
                   jax.ShapeDtypeStruct((B, S, FOX_DIM), BF16),
                   jax.ShapeDtypeStruct((B, S, LANES), BF16),
                   jax.ShapeDtypeStruct((B, S, LANES), BF16),
                   jax.ShapeDtypeStruct((B, S, FOX_DIM), BF16)],
        scratch_shapes=[pltpu.VMEM((SUBLANES, LANES), F32),
                        pltpu.VMEM((2 * LRU_WIDTH // LANES + 1, SUBLANES * ROW_PITCH, LANES), F32),
                        pltpu.VMEM(((LRU_CONV - 1) * SUBLANES, LRU_WIDTH), F32),
                        pltpu.VMEM((SUBLANES, LRU_WIDTH), F32)],
        compiler_params=_params("arbitrary"), name="even_inproj",
    )(x, *consts)


def _store_value_transpose(v, vt_ref, tile, nchunk):
    heads, _, _, chunk = vt_ref.shape
    vt = v.astype(F32).T
    ones = jnp.ones((VT_ROWS - HEAD_DIM, chunk), BF16)
    for h in range(heads):
        for c in range(nchunk):
            vt_ref[h, tile * nchunk + c, 0:HEAD_DIM, :] = vt[h * HEAD_DIM:(h + 1) * HEAD_DIM,
                                                             c * chunk:(c + 1) * chunk].astype(BF16)
            vt_ref[h, tile * nchunk + c, HEAD_DIM:VT_ROWS, :] = ones


def _fox_kernel(q_ref, qe_ref, k_ref, ke_ref, v_ref, o_ref, acc_ref, s_ref, vt_ref):
    tq = q_ref.shape[0]
    qi = pl.program_id(1)
    qe = qe_ref[...]
    qq = []
    for h in range(FOX_HEADS):
        half = h % 2
        blk = q_ref[:, (h // 2) * LANES:(h // 2 + 1) * LANES]
        qq.append(jnp.concatenate(
            [_keep_lanes(blk, half * HEAD_DIM, (half + 1) * HEAD_DIM),
             _keep_lanes(qe, BIAS_LANES * h, BIAS_LANES * (h + 1))], axis=-1))
    key = lax.broadcasted_iota(jnp.int32, (tq, tq), 0)
    qry = lax.broadcasted_iota(jnp.int32, (tq, tq), 1)
    acc_ref[...] = jnp.zeros_like(acc_ref)
    _store_value_transpose(v_ref[...], vt_ref, qi, 1)

    def update(j, m_run, masked):
        ks = pl.ds(pl.multiple_of(j * tq, tq), tq)
        ke = ke_ref[ks, :]
        for h in range(FOX_HEADS):
            kk = jnp.concatenate([k_ref[ks, (h // 2) * LANES:(h // 2 + 1) * LANES], ke], axis=-1)
            s = _dot_nt(kk, qq[h])
            if masked:
                s = jnp.where(key <= qry, s, -jnp.inf)
            s_ref[h] = s
        new = []
        for h in range(FOX_HEADS):
            m_new = jnp.maximum(m_run[h], jnp.max(s_ref[h], axis=0, keepdims=True))
            alpha = jnp.exp2(m_run[h] - m_new)
            p = jnp.exp2(s_ref[h] - m_new).astype(BF16)
            acc_ref[h] = alpha * acc_ref[h] + _dot(vt_ref[h, j], p)
            new.append(m_new)
        return tuple(new)

    init = (jnp.full((1, tq), -jnp.inf, F32),) * FOX_HEADS
    m_run = lax.fori_loop(0, qi, lambda j, m: update(j, m, False), init)
    update(qi, m_run, True)
    for pair in range(FOX_HEADS // 2):
        out = [acc_ref[h, 0:HEAD_DIM, :] / acc_ref[h, HEAD_DIM:HEAD_DIM + 1, :]
               for h in (2 * pair, 2 * pair + 1)]
        o_ref[:, pair * LANES:(pair + 1) * LANES] = jnp.concatenate(out, axis=0).T.astype(BF16)


def _fox_attention(q, qe, k, ke, v):
    B, S, _ = q.shape
    tq = min(ATTN_TILE, S)
    qblk = lambda w: pl.BlockSpec((None, tq, w), lambda b, i: (b, i, 0))
    kblk = lambda w: pl.BlockSpec((None, S, w), lambda b, i: (b, 0, 0))
    return pl.pallas_call(
        _fox_kernel, grid=(B, S // tq),
        in_specs=[qblk(FOX_DIM), qblk(LANES), kblk(FOX_DIM), kblk(LANES), qblk(FOX_DIM)],
        out_specs=qblk(FOX_DIM),
        out_shape=jax.ShapeDtypeStruct((B, S, FOX_DIM), BF16),
        scratch_shapes=[pltpu.VMEM((FOX_HEADS, VT_ROWS, tq), F32),
                        pltpu.VMEM((FOX_HEADS, tq, tq), F32),
                        pltpu.VMEM((FOX_HEADS, S // tq, VT_ROWS, tq), BF16)],
        compiler_params=_params("parallel", "arbitrary"), name="fox_attention",
    )(q, qe, k, ke, v)


def _rope_table_kernel(inv_ref, cos_ref, sin_ref):
    rows = cos_ref.shape[0]
    pos = pl.program_id(0) * rows + lax.broadcasted_iota(jnp.int32, cos_ref.shape, 0)
    lane = lax.broadcasted_iota(jnp.int32, cos_ref.shape, 1)
    ang = pos.astype(F32) * inv_ref[...]
    cos_ref[...] = jnp.cos(ang)
    sin_ref[...] = jnp.where((lane & (HEAD_DIM - 1)) < HEAD_DIM // 2, -1.0, 1.0) * jnp.sin(ang)


def _rope_tables(S):
    half = HEAD_DIM // 2
    inv = jnp.power(ROPE_THETA, -jnp.arange(half, dtype=F32) / half)
    inv4 = jnp.tile(inv, LANES // half)[None, :]
    rows = min(ROW_TILE, S)
    blk = pl.BlockSpec((rows, LANES), lambda i: (i, 0))
    return pl.pallas_call(
        _rope_table_kernel, grid=(S // rows,),
        in_specs=[pl.BlockSpec((1, LANES), lambda i: (0, 0))], out_specs=[blk, blk],
        out_shape=[jax.ShapeDtypeStruct((S, LANES), F32)] * 2,
        compiler_params=_params("parallel"), name="rope_tables",
    )(inv4)


def _rope(x, cos, sin):
    half = HEAD_DIM // 2
    lane = lax.broadcasted_iota(jnp.int32, x.shape, 1)
    first = (lane & (HEAD_DIM - 1)) < half
    swapped = jnp.where(first, pltpu.roll(x, LANES - half, axis=1), pltpu.roll(x, half, axis=1))
    return x * cos + swapped * sin


def _dup_halves(x):
    lane = lax.broadcasted_iota(jnp.int32, x.shape, 1)
    r = pltpu.roll(x, HEAD_DIM, axis=1)
    lo = lane < HEAD_DIM
    return jnp.concatenate([jnp.where(lo, x, r), jnp.where(lo, r, x)], axis=-1)


def _s5_chunk(u, bm_ref, cm_ref, ar_ref, ai_ref, d_ref, gw_ref, gb_ref, h_ref, state_ref):
    rows = u.shape[0]
    half = h_ref.shape[2] // 2
    ub = u.astype(BF16)
    for g in range(S5_LANE_GROUPS):
        h_ref[g] = _dot(ub[:, g * LANES:(g + 1) * LANES], bm_ref[g])
    ys = []
    for g in range(S5_LANE_GROUPS):
        ar = jnp.broadcast_to(ar_ref[g], (SUBLANES, half))
        ai = jnp.broadcast_to(ai_ref[g], (SUBLANES, half))
        hr = state_ref[g, :, 0:half]
        hi = state_ref[g, :, half:2 * half]
        for t in range(rows // SUBLANES):
            sl = slice(t * SUBLANES, (t + 1) * SUBLANES)
            nr = ar * hr - ai * hi + h_ref[g, sl, 0:half]
            ni = ar * hi + ai * hr + h_ref[g, sl, half:2 * half]
            h_ref[g, sl, 0:half] = nr
            h_ref[g, sl, half:2 * half] = ni
            hr, hi = nr, ni
        state_ref[g, :, 0:half] = hr
        state_ref[g, :, half:2 * half] = hi
        ys.append(_dot(h_ref[g].astype(BF16), cm_ref[g]))
    y = jnp.concatenate(ys, axis=-1) + d_ref[...] * u
    z = jax.nn.gelu(y)
    return z * jax.nn.sigmoid(_dot(z.astype(BF16), gw_ref[...]) + gb_ref[...])


def _odd_in_kernel(x_ref, g_ref, wq_ref, wkv_ref, wu_ref, qg_ref, kg_ref, ind_ref,
                   bm_ref, cm_ref, ar_ref, ai_ref, d_ref, gw_ref, gb_ref,
                   cos_ref, sin_ref, q_ref, kd_ref, v_ref, d_out_ref, slab_ref, h_ref, state_ref):
    B, ts, D = x_ref.shape
    rows = B * ts

    @pl.when(pl.program_id(0) == 0)
    def _():
        state_ref[...] = jnp.zeros_like(state_ref)

    n = _rms(x_ref[...].reshape(rows, D), g_ref[...]).astype(BF16)
    u = _to_time_major(_dot(n, wu_ref[...]), slab_ref)
    d_out_ref[...] = _s5_chunk(u, bm_ref, cm_ref, ar_ref, ai_ref, d_ref, gw_ref, gb_ref,
                               h_ref, state_ref).astype(BF16)
    cos = jnp.concatenate([cos_ref[...]] * B, axis=0)
    sin = jnp.concatenate([sin_ref[...]] * B, axis=0)
    q = _head_rms(_dot(n, wq_ref[...]), ind_ref[...], qg_ref[...])
    q = jnp.concatenate([_rope(q[:, blk * LANES:(blk + 1) * LANES], cos, sin)
                         for blk in range(SWA_DIM // LANES)], axis=-1) * (QK_SCALE * LOG2E)
    q_ref[...] = q.astype(BF16).reshape(B, ts, SWA_DIM)
    kv = _dot(n, wkv_ref[...])
    k = _head_rms(kv[:, :LANES], ind_ref[0:LANES, 0:LANES], kg_ref[...])
    kd_ref[...] = _dup_halves(_rope(k, cos, sin)).astype(BF16).reshape(B, ts, 2 * LANES)
    v_ref[...] = kv[:, LANES:].astype(BF16).reshape(B, ts, LANES)


def _odd_inproj(x, g, w_q, w_kv, w_u, qg, kg, s5, cos, sin):
    B, S, D = x.shape
    ts = TIME_CHUNK
    bmaj = lambda w: pl.BlockSpec((B, ts, w), lambda s: (0, s, 0))
    tab = pl.BlockSpec((ts, LANES), lambda s: (s, 0))
    consts = [g, w_q, w_kv, w_u, qg, kg, _head_mean_matrix(SWA_DIM), *s5]
    nstate = s5[0].shape[2]
    return pl.pallas_call(
        _odd_in_kernel, grid=(S // ts,),
        in_specs=[bmaj(D)] + [_const_spec(a.shape) for a in consts] + [tab, tab],
        out_specs=[bmaj(SWA_DIM), bmaj(2 * LANES), bmaj(LANES),
                   pl.BlockSpec((B * ts, S5_WIDTH), lambda s: (s, 0))],
        out_shape=[jax.ShapeDtypeStruct((B, S, SWA_DIM), BF16),
                   jax.ShapeDtypeStruct((B, S, 2 * LANES), BF16),
                   jax.ShapeDtypeStruct((B, S, LANES), BF16),
                   jax.ShapeDtypeStruct((S * B, S5_WIDTH), BF16)],
        scratch_shapes=[pltpu.VMEM((S5_WIDTH // LANES, SUBLANES * ROW_PITCH, LANES), F32),
                        pltpu.VMEM((S5_LANE_GROUPS, B * ts, nstate), F32),
                        pltpu.VMEM((S5_LANE_GROUPS, SUBLANES, nstate), F32)],
        compiler_params=_params("arbitrary"), name="odd_inproj",
    )(x, *consts, cos, sin)


def _swa_kernel(q_ref, kd_ref, v_ref, sink_ref, o_ref, vt_ref):
    tq = q_ref.shape[0]
    W = SWA_WINDOW
    G = SWA_GROUP
    base = pl.program_id(1) * tq
    key = lax.broadcasted_iota(jnp.int32, (2 * W, G * W), 0)
    qoff = lax.broadcasted_iota(jnp.int32, (2 * W, G * W), 1) & (W - 1)
    cgrp = lax.broadcasted_iota(jnp.int32, (1, G * W), 1) // W
    sinks = sink_ref[...] * LOG2E
    _store_value_transpose(v_ref[...], vt_ref, pl.program_id(1), tq // W)
    blocks = [(n, kvh) for n in range(tq // W) for kvh in range(SWA_KV_HEADS)]
    kstarts = [pl.multiple_of(jnp.maximum(base + n * W - W, 0), W) for n in range(tq // W)]
    scores = []
    for n, kvh in blocks:
        parts = []
        for g in range(G):
            head = kvh * G + g
            blk = q_ref[n * W:(n + 1) * W, (head // 2) * LANES:(head // 2 + 1) * LANES]
            half = head % 2
            parts.append(_keep_lanes(blk, half * HEAD_DIM, (half + 1) * HEAD_DIM))
        scores.append(_dot_nt(kd_ref[pl.ds(kstarts[n], 2 * W), kvh * LANES:(kvh + 1) * LANES],
                              jnp.concatenate(parts, axis=0)))
    for (n, kvh), s in zip(blocks, scores):
        rows = slice(n * W, (n + 1) * W)
        chunk = kstarts[n] // W
        diff = (base + n * W + qoff) - (kstarts[n] + key)
        s = jnp.where((diff >= 0) & (diff < W), s, -jnp.inf)
        sink = jnp.zeros((1, G * W), F32)
        for g in range(G):
            head = kvh * G + g
            sink = jnp.where(cgrp == g, sinks[:, head:head + 1], sink)
        m = jnp.maximum(jnp.max(s, axis=0, keepdims=True), sink)
        p = jnp.exp2(s - m).astype(BF16)
        acc = _dot(vt_ref[kvh, chunk], p[0:W, :]) + _dot(vt_ref[kvh, chunk + 1], p[W:2 * W, :])
        den = acc[HEAD_DIM:HEAD_DIM + 1, :] + jnp.exp2(sink - m)
        o = acc[0:HEAD_DIM, :] / den
        for pair in range(G // 2):
            both = jnp.concatenate([o[:, (2 * pair) * W:(2 * pair + 1) * W],
                                    o[:, (2 * pair + 1) * W:(2 * pair + 2) * W]], axis=0)
            lb = kvh * (G // 2) + pair
            o_ref[rows, lb * LANES:(lb + 1) * LANES] = both.T.astype(BF16)


def _swa_attention(q, kd, v, sinks):
    B, S, _ = q.shape
    tq = min(ATTN_TILE, S)
    return pl.pallas_call(
        _swa_kernel, grid=(B, S // tq),
        in_specs=[pl.BlockSpec((None, tq, SWA_DIM), lambda b, i: (b, i, 0)),
                  pl.BlockSpec((None, S, 2 * LANES), lambda b, i: (b, 0, 0)),
                  pl.BlockSpec((None, tq, LANES), lambda b, i: (b, i, 0)),
                  pl.BlockSpec((1, SWA_HEADS), lambda b, i: (0, 0))],
        out_specs=pl.BlockSpec((None, tq, SWA_DIM), lambda b, i: (b, i, 0)),
        out_shape=jax.ShapeDtypeStruct((B, S, SWA_DIM), BF16),
        scratch_shapes=[pltpu.VMEM((SWA_KV_HEADS, S // SWA_WINDOW, VT_ROWS, SWA_WINDOW), BF16)],
        compiler_params=_params("parallel", "arbitrary"), name="swa_attention",
    )(q, kd, v, sinks)


def _s5_prep_kernel(lr_ref, li_ref, ldt_ref, br_ref, bi_ref, ar_ref, ai_ref, bbr_ref, bbi_ref):
    lr, li = lr_ref[...], li_ref[...]
    dt = jnp.exp(ldt_ref[...])
    mag = jnp.exp(lr * dt)
    ar = mag * jnp.cos(li * dt)
    ai = mag * jnp.sin(li * dt)
    den = lr * lr + li * li
    cr = ((ar - 1.0) * lr + ai * li) / den
    ci = (ai * lr - (ar - 1.0) * li) / den
    br, bi = br_ref[...], bi_ref[...]
    ar_ref[...] = ar
    ai_ref[...] = ai
    bbr_ref[...] = cr * br - ci * bi
    bbi_ref[...] = cr * bi + ci * br


def _s5_prep(lam_re, lam_im, log_dt, b_re, b_im):
    G, P, C = b_re.shape
    rep = lambda a: jnp.repeat(a, C, axis=0)
    bt = lambda a: a.transpose(0, 2, 1).reshape(G * C, P)
    ldt = jnp.broadcast_to(log_dt[:, None], (G, P))
    full = pl.BlockSpec((G * C, P), lambda: (0, 0))
    outs = pl.pallas_call(
        _s5_prep_kernel, in_specs=[full] * 5, out_specs=[full] * 4,
        out_shape=[jax.ShapeDtypeStruct((G * C, P), F32)] * 4, name="s5_prep",
    )(rep(lam_re), rep(lam_im), rep(ldt), bt(b_re), bt(b_im))
    ar, ai, bbr, bbi = [o.reshape(G, C, P) for o in outs]
    return ar[:, 0], ai[:, 0], bbr, bbi


def _s5_matrices(ar, ai, bbr, bbi, c_re, c_im):
    L, GL = S5_LANE_GROUPS, S5_GROUPS // S5_LANE_GROUPS
    C, P = S5_GROUP, S5_STATE
    eye = jnp.eye(GL, dtype=F32)

    def inmap(b):
        return jnp.einsum("lgcp,gh->lgchp", b.reshape(L, GL, C, P), eye).reshape(L, GL * C, GL * P)

    def outmap(c):
        return jnp.einsum("lgcp,gh->lgphc", c.reshape(L, GL, C, P), eye).reshape(L, GL * P, GL * C)

    bmat = jnp.concatenate([inmap(bbr), inmap(bbi)], axis=2).astype(BF16)
    cmat = jnp.concatenate([outmap(c_re), outmap(-c_im)], axis=1).astype(BF16)
    a_r = ar.reshape(L, 1, GL * P)
    a_i = ai.reshape(L, 1, GL * P)
    return bmat, cmat, a_r, a_i


def _block_diag_pairs(w):
    nb, bs, _ = w.shape
    half = nb // 2
    eye = jnp.eye(half, dtype=w.dtype)
    out = jnp.einsum("thij,hk->thikj", w.reshape(2, half, bs, bs), eye)
    return out.reshape(2, half * bs, half * bs).astype(BF16)


def kernel(x, p, ffn1_norm, ffn1_wg, ffn1_wu, ffn1_wd, mix_norm, ffn2_norm, ffn2_wg, ffn2_wu, ffn2_wd, ple_w, ple_norm, ple_gate_norm, ple_gate_w, ev_w_in, lru_conv_w, lru_conv_b, lru_wa, lru_ba, lru_wx, lru_bx, lru_lambda, fox_bf, fox_q_norm, fox_k_norm, ev_w_out, od_w_in, swa_q_norm, swa_k_norm, swa_sinks, s5_lambda_re, s5_lambda_im, s5_log_dt, s5_b_re, s5_b_im, s5_c_re, s5_c_im, s5_d, s5_glu_w, s5_glu_b, od_w_out):
    B, S, D = x.shape
    depth = p.shape[0]
    assert B == SUBLANES and D == D_MODEL and S % TIME_CHUNK == 0
    T = B * S
    bf = lambda a: a.astype(BF16)
    row = lambda a: a[:, None, :]
    per_head = lambda gain, heads: jnp.tile(gain, heads)[None, :]

    f1 = (row(ffn1_norm), bf(ffn1_wg), bf(ffn1_wu), bf(ffn1_wd))
    f2 = (row(ffn2_norm), bf(ffn2_wg), bf(ffn2_wu), bf(ffn2_wd))
    ple = (p, bf(ple_w), row(ple_norm), row(ple_gate_norm), bf(ple_gate_w))
    cos, sin = _rope_tables(S)

    for i in range(depth):
        j = i // 2
        x = _ffn(x.reshape(T, D), i, *f1).reshape(B, S, D)
        g = mix_norm[i][None, :]
        if i % 2 == 0:
            w_in = bf(ev_w_in[j])
            o1, o2 = 2 * LRU_WIDTH, 2 * LRU_WIDTH + 3 * FOX_DIM
            w_f = jnp.pad(w_in[:, o2:], ((0, 0), (0, LANES - FOX_HEADS)))
            b_f = jnp.pad(fox_bf[j], (0, LANES - FOX_HEADS))[None, :]
            lru = (lru_conv_w[j], lru_conv_b[j][None, :],
                   _block_diag_pairs(lru_wa[j]), lru_ba[j][None, :],
                   _block_diag_pairs(lru_wx[j]), lru_bx[j][None, :], lru_lambda[j][None, :])
            a_out, q, k, qe, ke, v = _even_inproj(
                x, g, w_in[:, :o1], w_in[:, o1:o2], w_f, b_f,
                per_head(fox_q_norm[j], FOX_HEADS), per_head(fox_k_norm[j], FOX_HEADS), lru)
            b_out = _fox_attention(q, qe, k, ke, v)
            w_out = bf(ev_w_out[j])
            mixed = (a_out, b_out, w_out[:LRU_WIDTH], w_out[LRU_WIDTH:])
        else:
            w_in = bf(od_w_in[j])
            kvd = SWA_KV_HEADS * HEAD_DIM
            o1, o2 = SWA_DIM, SWA_DIM + 2 * kvd
            ar, ai, bbr, bbi = _s5_prep(s5_lambda_re[j], s5_lambda_im[j], s5_log_dt[j],
                                        s5_b_re[j], s5_b_im[j])
            bmat, cmat, a_r, a_i = _s5_matrices(ar, ai, bbr, bbi, s5_c_re[j], s5_c_im[j])
            s5 = (bmat, cmat, a_r, a_i, s5_d[j][None, :], bf(s5_glu_w[j]), s5_glu_b[j][None, :])
            q, kd, v, d_out = _odd_inproj(
                x, g, w_in[:, :o1], w_in[:, o1:o2], w_in[:, o2:],
                per_head(swa_q_norm[j], SWA_HEADS), per_head(swa_k_norm[j], SWA_KV_HEADS), s5,
                cos, sin)
            c_out = _swa_attention(q, kd, v, swa_sinks[j][None, :])
            w_out = bf(od_w_out[j])
            mixed = (d_out, c_out, w_out[SWA_DIM:], w_out[:SWA_DIM])
        x = _mix_ffn_ple(x, *mixed, i, f2, ple)
    return x
```

```python
import numpy as np

import jax
import jax.numpy as jnp
from jax import lax
from jax.experimental import pallas as pl
from jax.experimental.pallas import tpu as pltpu

F32 = jnp.float32
BF16 = jnp.bfloat16

D_MODEL = 1024
HEAD_DIM = 64
LRU_WIDTH = 512
LRU_CONV = 4
LRU_C = 8.0
FOX_HEADS = 8
FOX_DIM = 512
SWA_HEADS = 8
SWA_KV_HEADS = 2
SWA_GROUP = SWA_HEADS // SWA_KV_HEADS
SWA_DIM = 512
SWA_WINDOW = 128
S5_WIDTH = 512
S5_GROUP = 16
S5_GROUPS = 32
S5_STATE = 64
D_FF = 2816
PLE_DIM = 256
ROPE_THETA = 10000.0
EPS = 1e-6
MACARON = 0.5
QK_SCALE = HEAD_DIM ** -0.5
LOG2E = 1.4426950408889634

SUBLANES = 8
LANES = 128
TIME_CHUNK = 64
ROW_PITCH = TIME_CHUNK + 8
ROW_TILE = SUBLANES * TIME_CHUNK
ATTN_TILE = 512
S5_LANE_GROUPS = 4
BIAS_LANES = 6
VT_ROWS = 80
VMEM_LIMIT = 56 * 1024 * 1024


def _dot(a, b):
    return jnp.dot(a, b, preferred_element_type=F32)


def _dot_nt(a, b):
    return lax.dot_general(a, b, (((1,), (1,)), ((), ())), preferred_element_type=F32)


def _rms(x, g):
    ms = jnp.mean(x * x, axis=-1, keepdims=True)
    return x * lax.rsqrt(ms + EPS) * g


def _head_rms(x, ind, gain):
    ms = _dot((x * x).astype(BF16), ind)
    return x * lax.rsqrt(ms + EPS) * gain


def _softplus(x):
    return jnp.maximum(x, 0.0) + jnp.log1p(jnp.exp(-jnp.abs(x)))


def _log_sigmoid(x):
    return -_softplus(-x)


def _cumsum_time(x):
    n = x.shape[0]
    row = lax.broadcasted_iota(jnp.int32, x.shape, 0)
    d = SUBLANES
    while d < n:
        x = x + jnp.where(row >= d, pltpu.roll(x, d, axis=0), 0.0)
        d *= 2
    return x


def _to_time_major(val, slab_ref):
    rows, width = val.shape
    steps = rows // SUBLANES
    for s in range(width // LANES):
        for b in range(SUBLANES):
            slab_ref[s, b * ROW_PITCH:b * ROW_PITCH + steps, :] = (
                val[b * steps:(b + 1) * steps, s * LANES:(s + 1) * LANES])
    return jnp.concatenate(
        [jnp.concatenate([slab_ref[s, pl.ds(t, SUBLANES, stride=ROW_PITCH), :]
                          for s in range(width // LANES)], axis=-1) for t in range(steps)], axis=0)


def _to_batch_major(val, slab_ref):
    rows, width = val.shape
    steps = rows // SUBLANES
    for s in range(width // LANES):
        slab_ref[s, 0:rows, :] = val[:, s * LANES:(s + 1) * LANES]
    return jnp.concatenate(
        [jnp.concatenate([slab_ref[s, pl.ds(b, steps, stride=SUBLANES), :]
                          for s in range(width // LANES)], axis=-1) for b in range(SUBLANES)], axis=0)


def _split3_bf16(c):
    hi = c.astype(BF16).astype(F32)
    r = c - hi
    mid = r.astype(BF16).astype(F32)
    return hi, mid, r - mid


def _keep_lanes(x, lo, hi):
    lane = lax.broadcasted_iota(jnp.int32, x.shape, 1)
    return jnp.where((lane >= lo) & (lane < hi), x.astype(F32), 0.0).astype(BF16)


def _params(*sem):
    return pltpu.CompilerParams(dimension_semantics=sem, vmem_limit_bytes=VMEM_LIMIT)


def _const_spec(shape):
    nd = len(shape)
    return pl.BlockSpec(shape, lambda *_: (0,) * nd, pipeline_mode=pl.Buffered(1))


def _layer_spec(shape, layer):
    nd = len(shape)
    return pl.BlockSpec((None,) + tuple(shape), lambda *_: (layer,) + (0,) * nd,
                        pipeline_mode=pl.Buffered(1))


def _head_mean_matrix(width):
    h = np.arange(width) // HEAD_DIM
    return jnp.asarray((h[:, None] == h[None, :]) / HEAD_DIM, dtype=BF16)


def _swiglu_update(x, g_ref, wg_ref, wu_ref, wd_ref):
    n = _rms(x, g_ref[...]).astype(BF16)
    hg = _dot(n, wg_ref[...])
    hu = _dot(n, wu_ref[...])
    act = (hg * jax.nn.sigmoid(hg) * hu).astype(BF16)
    return x + MACARON * _dot(act, wd_ref[...])


def _ffn_kernel(x_ref, g_ref, wg_ref, wu_ref, wd_ref, o_ref):
    half = x_ref.shape[0] // 2
    for i in range(2):
        rs = slice(i * half, (i + 1) * half)
        o_ref[rs, :] = _swiglu_update(x_ref[rs, :], g_ref, wg_ref, wu_ref, wd_ref)


def _ffn(x2d, layer, norm, wg, wu, wd):
    T, D = x2d.shape
    tm = min(ROW_TILE, T)
    row = pl.BlockSpec((tm, D), lambda i: (i, 0))
    return pl.pallas_call(
        _ffn_kernel, grid=(T // tm,),
        in_specs=[row, _layer_spec((1, D), layer), _layer_spec((D, D_FF), layer),
                  _layer_spec((D, D_FF), layer), _layer_spec((D_FF, D), layer)],
        out_specs=row, out_shape=jax.ShapeDtypeStruct((T, D), F32),
        compiler_params=_params("parallel"), name="ffn",
    )(x2d, norm, wg, wu, wd)


def _mix_ffn_ple_kernel(x_ref, tmaj_ref, bmaj_ref, wt_ref, wb_ref,
                        g_ref, wg_ref, wu_ref, wd_ref, p_ref, pw_ref, pn_ref, gn_ref, gw_ref, o_ref,
                        slab_ref):
    B, ts, D = x_ref.shape
    rows = B * ts
    part_t = _to_batch_major(tmaj_ref[...].astype(F32), slab_ref).astype(BF16)
    part_b = bmaj_ref[...].reshape(rows, bmaj_ref.shape[2])
    x = x_ref[...].reshape(rows, D) + _dot(part_t, wt_ref[...]) + _dot(part_b, wb_ref[...])
    half = rows // 2
    pb = p_ref[...].reshape(rows, PLE_DIM).astype(BF16)
    halves = [_swiglu_update(x[i * half:(i + 1) * half, :], g_ref, wg_ref, wu_ref, wd_ref)
              for i in range(2)]
    outs = []
    for i, xh in enumerate(halves):
        e = _rms(_dot(pb[i * half:(i + 1) * half, :], pw_ref[...]), pn_ref[...])
        gate = jax.nn.sigmoid(_dot(_rms(xh, gn_ref[...]).astype(BF16), gw_ref[...]))
        outs.append(xh + gate * e)
    o_ref[...] = jnp.concatenate(outs, axis=0).reshape(B, ts, D)


def _mix_ffn_ple(x, part_tmaj, part_bmaj, w_tmaj, w_bmaj, layer, ffn, ple):
    B, S, D = x.shape
    ts = TIME_CHUNK
    W = part_tmaj.shape[1]
    xs = pl.BlockSpec((B, ts, D), lambda s: (0, s, 0))
    in_specs = [xs, pl.BlockSpec((B * ts, W), lambda s: (s, 0)),
                pl.BlockSpec((B, ts, W), lambda s: (0, s, 0)),
                _const_spec((W, D)), _const_spec((W, D)),
                _layer_spec((1, D), layer), _layer_spec((D, D_FF), layer),
                _layer_spec((D, D_FF), layer), _layer_spec((D_FF, D), layer),
                pl.BlockSpec((None, B, ts, PLE_DIM), lambda s: (layer, 0, s, 0)),
                _layer_spec((PLE_DIM, D), layer), _layer_spec((1, D), layer),
                _layer_spec((1, D), layer), _layer_spec((D, D), layer)]
    return pl.pallas_call(
        _mix_ffn_ple_kernel, grid=(S // ts,), in_specs=in_specs, out_specs=xs,
        out_shape=jax.ShapeDtypeStruct((B, S, D), F32),
        scratch_shapes=[pltpu.VMEM((W // LANES, B * ts, LANES), F32)],
        compiler_params=_params("parallel"), name="mix_ffn_ple",
    )(x, part_tmaj, part_bmaj, w_tmaj, w_bmaj, *ffn, *ple)


def _bias_lane_maps():
    eq = np.zeros((LANES, LANES), np.float32)
    ek = np.zeros((LANES, LANES), np.float32)
    oq = np.zeros((1, LANES), np.float32)
    ok = np.zeros((1, LANES), np.float32)
    for h in range(FOX_HEADS):
        for i in range(3):
            eq[i * FOX_HEADS + h, BIAS_LANES * h + i] = 1.0
            ek[i * FOX_HEADS + h, BIAS_LANES * h + 3 + i] = -1.0
            oq[0, BIAS_LANES * h + 3 + i] = 1.0
            ok[0, BIAS_LANES * h + i] = 1.0
    return jnp.asarray(eq, BF16), jnp.asarray(ek, BF16), jnp.asarray(oq), jnp.asarray(ok)


def _rg_lru_chunk(xa, ya, cw_ref, cb_ref, wa_ref, ba_ref, wx_ref, bx_ref, lam_ref, xprev_ref, h_ref):
    rows = xa.shape[0]
    halo = (LRU_CONV - 1) * SUBLANES
    half = LRU_WIDTH // 2
    xfull = jnp.concatenate([xprev_ref[...], xa], axis=0)
    xprev_ref[...] = xa[rows - halo:rows, :]
    xc = cb_ref[...]
    for tap in range(LRU_CONV):
        xc = xc + xfull[tap * SUBLANES:tap * SUBLANES + rows, :] * cw_ref[tap:tap + 1, :]
    xb = xc.astype(BF16)

    def gate(w_ref, b_ref):
        z = jnp.concatenate([_dot(xb[:, :half], w_ref[0]), _dot(xb[:, half:], w_ref[1])], axis=-1)
        return jax.nn.sigmoid(z + b_ref[...])

    r = gate(wa_ref, ba_ref)
    i = gate(wx_ref, bx_ref)
    log_a = -LRU_C * r * _softplus(lam_ref[...])
    a = jnp.exp(log_a)
    th = jnp.tanh(log_a)
    b = jnp.sqrt(-2.0 * th / (1.0 - th)) * (i * xc)
    h = h_ref[...]
    hs = []
    for t in range(rows // SUBLANES):
        sl = slice(t * SUBLANES, (t + 1) * SUBLANES)
        h = a[sl, :] * h + b[sl, :]
        hs.append(h)
    h_ref[...] = h
    return jax.nn.gelu(ya) * jnp.concatenate(hs, axis=0)


def _even_in_kernel(x_ref, g_ref, wxyf_ref, wqkv_ref, bf_ref, qg_ref, kg_ref, ind_ref,
                    eqk_ref, oqk_ref, cw_ref, cb_ref, wa_ref, ba_ref, wx_ref, bx_ref, lam_ref,
                    a_out_ref, q_ref, k_ref, qe_ref, ke_ref, v_ref,
                    carry_ref, slab_ref, xprev_ref, h_ref):
    B, ts, D = x_ref.shape
    rows = B * ts

    @pl.when(pl.program_id(0) == 0)
    def _():
        carry_ref[...] = jnp.zeros_like(carry_ref)
        xprev_ref[...] = jnp.zeros_like(xprev_ref)
        h_ref[...] = jnp.zeros_like(h_ref)

    n = _rms(x_ref[...].reshape(rows, D), g_ref[...]).astype(BF16)
    f_cols = _dot(n, wxyf_ref[:, 2 * LRU_WIDTH:])
    xyf = jnp.concatenate([_dot(n, wxyf_ref[:, 0:2 * LRU_WIDTH]), f_cols], axis=-1)
    qkv = _dot(n, wqkv_ref[...])
    xyf = _to_time_major(xyf, slab_ref)
    a_out_ref[...] = _rg_lru_chunk(xyf[:, :LRU_WIDTH], xyf[:, LRU_WIDTH:2 * LRU_WIDTH], cw_ref, cb_ref,
                                   wa_ref, ba_ref, wx_ref, bx_ref, lam_ref, xprev_ref,
                                   h_ref).astype(BF16)

    logf = _log_sigmoid(xyf[:, 2 * LRU_WIDTH:] + bf_ref[...]) * LOG2E
    c = _cumsum_time(logf) + jnp.concatenate([carry_ref[...]] * ts, axis=0)
    carry_ref[...] = c[rows - SUBLANES:rows, :]
    hi, mid, lo = _split3_bf16(c)
    lane = lax.broadcasted_iota(jnp.int32, c.shape, 1)
    packed = jnp.where(lane < FOX_HEADS, hi, jnp.where(
        lane < 2 * FOX_HEADS, pltpu.roll(mid, FOX_HEADS, axis=1), jnp.where(
            lane < 3 * FOX_HEADS, pltpu.roll(lo, 2 * FOX_HEADS, axis=1), 0.0)))
    packed = _to_batch_major(packed, slab_ref).astype(BF16)
    qke = (_dot(packed, eqk_ref[...]) + oqk_ref[...]).astype(BF16)
    qe_ref[...] = qke[:, :LANES].reshape(B, ts, LANES)
    ke_ref[...] = qke[:, LANES:].reshape(B, ts, LANES)

    ind = ind_ref[...]
    q = _head_rms(qkv[:, :FOX_DIM], ind, qg_ref[...]) * (QK_SCALE * LOG2E)
    k = _head_rms(qkv[:, FOX_DIM:2 * FOX_DIM], ind, kg_ref[...])
    q_ref[...] = q.astype(BF16).reshape(B, ts, FOX_DIM)
    k_ref[...] = k.astype(BF16).reshape(B, ts, FOX_DIM)
    v_ref[...] = qkv[:, 2 * FOX_DIM:].astype(BF16).reshape(B, ts, FOX_DIM)


def _even_inproj(x, g, w_xy, w_qkv, w_f, b_f, qg, kg, lru):
    B, S, D = x.shape
    ts = TIME_CHUNK
    eq, ek, oq, ok = _bias_lane_maps()
    tmaj = pl.BlockSpec((B * ts, LRU_WIDTH), lambda s: (s, 0))
    bmaj = lambda w: pl.BlockSpec((B, ts, w), lambda s: (0, s, 0))
    consts = [g, jnp.concatenate([w_xy, w_f], axis=1), w_qkv, b_f, qg, kg,
              _head_mean_matrix(FOX_DIM), jnp.concatenate([eq, ek], axis=1),
              jnp.concatenate([oq, ok], axis=1), *lru]
    return pl.pallas_call(
        _even_in_kernel, grid=(S // ts,),
        in_specs=[bmaj(D)] + [_const_spec(a.shape) for a in consts],
        out_specs=[tmaj, bmaj(FOX_DIM), bmaj(FOX_DIM), bmaj(LANES), bmaj(LANES), bmaj(FOX_DIM)],
        out_shape=[jax.ShapeDtypeStruct((S * B, LRU_WIDTH), BF16),
                   jax.ShapeDtypeStruct((B, S, FOX_DIM), BF16),
                   jax.ShapeDtypeStruct((B, S, FOX_DIM), BF16),
                   jax.ShapeDtypeStruct((B, S, LANES), BF16),
                   jax.ShapeDtypeStruct((B, S, LANES), BF16),
                   jax.ShapeDtypeStruct((B, S, FOX_DIM), BF16)],
        scratch_shapes=[pltpu.VMEM((SUBLANES, LANES), F32),
                        pltpu.VMEM((2 * LRU_WIDTH // LANES + 1, SUBLANES * ROW_PITCH, LANES), F32),
                        pltpu.VMEM(((LRU_CONV - 1) * SUBLANES, LRU_WIDTH), F32),
                        pltpu.VMEM((SUBLANES, LRU_WIDTH), F32)],
        compiler_params=_params("arbitrary"), name="even_inproj",
    )(x, *consts)


def _store_value_transpose(v, vt_ref, tile, nchunk):
    heads, _, _, chunk = vt_ref.shape
    vt = v.astype(F32).T
    ones = jnp.ones((VT_ROWS - HEAD_DIM, chunk), BF16)
    for h in range(heads):
        for c in range(nchunk):
            vt_ref[h, tile * nchunk + c, 0:HEAD_DIM, :] = vt[h * HEAD_DIM:(h + 1) * HEAD_DIM,
                                                             c * chunk:(c + 1) * chunk].astype(BF16)
            vt_ref[h, tile * nchunk + c, HEAD_DIM:VT_ROWS, :] = ones


def _fox_kernel(q_ref, qe_ref, k_ref, ke_ref, v_ref, o_ref, acc_ref, s_ref, vt_ref):
    tq = q_ref.shape[0]
    qi = pl.program_id(1)
    qe = qe_ref[...]
    qq = []
    for h in range(FOX_HEADS):
        half = h % 2
        blk = q_ref[:, (h // 2) * LANES:(h // 2 + 1) * LANES]
        qq.append(jnp.concatenate(
            [_keep_lanes(blk, half * HEAD_DIM, (half + 1) * HEAD_DIM),
             _keep_lanes(qe, BIAS_LANES * h, BIAS_LANES * (h + 1))], axis=-1))
    key = lax.broadcasted_iota(jnp.int32, (tq, tq), 0)
    qry = lax.broadcasted_iota(jnp.int32, (tq, tq), 1)
    acc_ref[...] = jnp.zeros_like(acc_ref)
    _store_value_transpose(v_ref[...], vt_ref, qi, 1)

    def update(j, m_run, masked):
        ks = pl.ds(pl.multiple_of(j * tq, tq), tq)
        ke = ke_ref[ks, :]
        for h in range(FOX_HEADS):
            kk = jnp.concatenate([k_ref[ks, (h // 2) * LANES:(h // 2 + 1) * LANES], ke], axis=-1)
            s = _dot_nt(kk, qq[h])
            if masked:
                s = jnp.where(key <= qry, s, -jnp.inf)
            s_ref[h] = s
        new = []
        for h in range(FOX_HEADS):
            m_new = jnp.maximum(m_run[h], jnp.max(s_ref[h], axis=0, keepdims=True))
            alpha = jnp.exp2(m_run[h] - m_new)
            p = jnp.exp2(s_ref[h] - m_new).astype(BF16)
            acc_ref[h] = alpha * acc_ref[h] + _dot(vt_ref[h, j], p)
            new.append(m_new)
        return tuple(new)

    init = (jnp.full((1, tq), -jnp.inf, F32),) * FOX_HEADS
    m_run = update(qi, init, True)
    lax.fori_loop(0, qi, lambda j, m: update(j, m, False), m_run)
    for pair in range(FOX_HEADS // 2):
        out = [acc_ref[h, 0:HEAD_DIM, :] / acc_ref[h, HEAD_DIM:HEAD_DIM + 1, :]
               for h in (2 * pair, 2 * pair + 1)]
        o_ref[:, pair * LANES:(pair + 1) * LANES] = jnp.concatenate(out, axis=0).T.astype(BF16)


def _fox_attention(q, qe, k, ke, v):
    B, S, _ = q.shape
    tq = min(ATTN_TILE, S)
    qblk = lambda w: pl.BlockSpec((None, tq, w), lambda b, i: (b, i, 0))
    kblk = lambda w: pl.BlockSpec((None, S, w), lambda b, i: (b, 0, 0))
    return pl.pallas_call(
        _fox_kernel, grid=(B, S // tq),
        in_specs=[qblk(FOX_DIM), qblk(LANES), kblk(FOX_DIM), kblk(LANES), qblk(FOX_DIM)],
        out_specs=qblk(FOX_DIM),
        out_shape=jax.ShapeDtypeStruct((B, S, FOX_DIM), BF16),
        scratch_shapes=[pltpu.VMEM((FOX_HEADS, VT_ROWS, tq), F32),
                        pltpu.VMEM((FOX_HEADS, tq, tq), F32),
                        pltpu.VMEM((FOX_HEADS, S // tq, VT_ROWS, tq), BF16)],
        compiler_params=_params("parallel", "arbitrary"), name="fox_attention",
    )(q, qe, k, ke, v)


def _rope_table_kernel(inv_ref, cos_ref, sin_ref):
    rows = cos_ref.shape[0]
    pos = pl.program_id(0) * rows + lax.broadcasted_iota(jnp.int32, cos_ref.shape, 0)
    lane = lax.broadcasted_iota(jnp.int32, cos_ref.shape, 1)
    ang = pos.astype(F32) * inv_ref[...]
    cos_ref[...] = jnp.cos(ang)
    sin_ref[...] = jnp.where((lane & (HEAD_DIM - 1)) < HEAD_DIM // 2, -1.0, 1.0) * jnp.sin(ang)


def _rope_tables(S):
    half = HEAD_DIM // 2
    inv = jnp.power(ROPE_THETA, -jnp.arange(half, dtype=F32) / half)
    inv4 = jnp.tile(inv, LANES // half)[None, :]
    rows = min(ROW_TILE, S)
    blk = pl.BlockSpec((rows, LANES), lambda i: (i, 0))
    return pl.pallas_call(
        _rope_table_kernel, grid=(S // rows,),
        in_specs=[pl.BlockSpec((1, LANES), lambda i: (0, 0))], out_specs=[blk, blk],
        out_shape=[jax.ShapeDtypeStruct((S, LANES), F32)] * 2,
        compiler_params=_params("parallel"), name="rope_tables",
    )(inv4)


def _rope(x, cos, sin):
    half = HEAD_DIM // 2
    lane = lax.broadcasted_iota(jnp.int32, x.shape, 1)
    first = (lane & (HEAD_DIM - 1)) < half
    swapped = jnp.where(first, pltpu.roll(x, LANES - half, axis=1), pltpu.roll(x, half, axis=1))
    return x * cos + swapped * sin


def _dup_halves(x):
    lane = lax.broadcasted_iota(jnp.int32, x.shape, 1)
    r = pltpu.roll(x, HEAD_DIM, axis=1)
    lo = lane < HEAD_DIM
    return jnp.concatenate([jnp.where(lo, x, r), jnp.where(lo, r, x)], axis=-1)


def _s5_chunk(u, bm_ref, cm_ref, ar_ref, ai_ref, d_ref, gw_ref, gb_ref, h_ref, state_ref):
    rows = u.shape[0]
    half = h_ref.shape[2] // 2
    ub = u.astype(BF16)
    for g in range(S5_LANE_GROUPS):
        h_ref[g] = _dot(ub[:, g * LANES:(g + 1) * LANES], bm_ref[g])
    ys = []
    for g in range(S5_LANE_GROUPS):
        ar = jnp.broadcast_to(ar_ref[g], (SUBLANES, half))
        ai = jnp.broadcast_to(ai_ref[g], (SUBLANES, half))
        hr = state_ref[g, :, 0:half]
        hi = state_ref[g, :, half:2 * half]
        for t in range(rows // SUBLANES):
            sl = slice(t * SUBLANES, (t + 1) * SUBLANES)
            nr = ar * hr - ai * hi + h_ref[g, sl, 0:half]
            ni = ar * hi + ai * hr + h_ref[g, sl, half:2 * half]
            h_ref[g, sl, 0:half] = nr
            h_ref[g, sl, half:2 * half] = ni
            hr, hi = nr, ni
        state_ref[g, :, 0:half] = hr
        state_ref[g, :, half:2 * half] = hi
        ys.append(_dot(h_ref[g].astype(BF16), cm_ref[g]))
    y = jnp.concatenate(ys, axis=-1) + d_ref[...] * u
    z = jax.nn.gelu(y)
    return z * jax.nn.sigmoid(_dot(z.astype(BF16), gw_ref[...]) + gb_ref[...])


def _odd_in_kernel(x_ref, g_ref, wq_ref, wkv_ref, wu_ref, qg_ref, kg_ref, ind_ref,
                   bm_ref, cm_ref, ar_ref, ai_ref, d_ref, gw_ref, gb_ref,
                   cos_ref, sin_ref, q_ref, kd_ref, v_ref, d_out_ref, slab_ref, h_ref, state_ref):
    B, ts, D = x_ref.shape
    rows = B * ts

    @pl.when(pl.program_id(0) == 0)
    def _():
        state_ref[...] = jnp.zeros_like(state_ref)

    n = _rms(x_ref[...].reshape(rows, D), g_ref[...]).astype(BF16)
    u = _to_time_major(_dot(n, wu_ref[...]), slab_ref)
    d_out_ref[...] = _s5_chunk(u, bm_ref, cm_ref, ar_ref, ai_ref, d_ref, gw_ref, gb_ref,
                               h_ref, state_ref).astype(BF16)
    cos = jnp.concatenate([cos_ref[...]] * B, axis=0)
    sin = jnp.concatenate([sin_ref[...]] * B, axis=0)
    q = _head_rms(_dot(n, wq_ref[...]), ind_ref[...], qg_ref[...])
    q = jnp.concatenate([_rope(q[:, blk * LANES:(blk + 1) * LANES], cos, sin)
                         for blk in range(SWA_DIM // LANES)], axis=-1) * (QK_SCALE * LOG2E)
    q_ref[...] = q.astype(BF16).reshape(B, ts, SWA_DIM)
    kv = _dot(n, wkv_ref[...])
    k = _head_rms(kv[:, :LANES], ind_ref[0:LANES, 0:LANES], kg_ref[...])
    kd_ref[...] = _dup_halves(_rope(k, cos, sin)).astype(BF16).reshape(B, ts, 2 * LANES)
    v_ref[...] = kv[:, LANES:].astype(BF16).reshape(B, ts, LANES)


def _odd_inproj(x, g, w_q, w_kv, w_u, qg, kg, s5, cos, sin):
    B, S, D = x.shape
    ts = TIME_CHUNK
    bmaj = lambda w: pl.BlockSpec((B, ts, w), lambda s: (0, s, 0))
    tab = pl.BlockSpec((ts, LANES), lambda s: (s, 0))
    consts = [g, w_q, w_kv, w_u, qg, kg, _head_mean_matrix(SWA_DIM), *s5]
    nstate = s5[0].shape[2]
    return pl.pallas_call(
        _odd_in_kernel, grid=(S // ts,),
        in_specs=[bmaj(D)] + [_const_spec(a.shape) for a in consts] + [tab, tab],
        out_specs=[bmaj(SWA_DIM), bmaj(2 * LANES), bmaj(LANES),
                   pl.BlockSpec((B * ts, S5_WIDTH), lambda s: (s, 0))],
        out_shape=[jax.ShapeDtypeStruct((B, S, SWA_DIM), BF16),
                   jax.ShapeDtypeStruct((B, S, 2 * LANES), BF16),
                   jax.ShapeDtypeStruct((B, S, LANES), BF16),
                   jax.ShapeDtypeStruct((S * B, S5_WIDTH), BF16)],
        scratch_shapes=[pltpu.VMEM((S5_WIDTH // LANES, SUBLANES * ROW_PITCH, LANES), F32),
                        pltpu.VMEM((S5_LANE_GROUPS, B * ts, nstate), F32),
                        pltpu.VMEM((S5_LANE_GROUPS, SUBLANES, nstate), F32)],
        compiler_params=_params("arbitrary"), name="odd_inproj",
    )(x, *consts, cos, sin)


def _swa_kernel(q_ref, kd_ref, v_ref, sink_ref, o_ref, vt_ref):
    tq = q_ref.shape[0]
    W = SWA_WINDOW
    G = SWA_GROUP
    base = pl.program_id(1) * tq
    key = lax.broadcasted_iota(jnp.int32, (2 * W, G * W), 0)
    qoff = lax.broadcasted_iota(jnp.int32, (2 * W, G * W), 1) & (W - 1)
    cgrp = lax.broadcasted_iota(jnp.int32, (1, G * W), 1) // W
    sinks = sink_ref[...] * LOG2E
    _store_value_transpose(v_ref[...], vt_ref, pl.program_id(1), tq // W)
    blocks = [(n, kvh) for n in range(tq // W) for kvh in range(SWA_KV_HEADS)]
    kstarts = [pl.multiple_of(jnp.maximum(base + n * W - W, 0), W) for n in range(tq // W)]
    scores = []
    for n, kvh in blocks:
        parts = []
        for g in range(G):
            head = kvh * G + g
            blk = q_ref[n * W:(n + 1) * W, (head // 2) * LANES:(head // 2 + 1) * LANES]
            half = head % 2
            parts.append(_keep_lanes(blk, half * HEAD_DIM, (half + 1) * HEAD_DIM))
        scores.append(_dot_nt(kd_ref[pl.ds(kstarts[n], 2 * W), kvh * LANES:(kvh + 1) * LANES],
                              jnp.concatenate(parts, axis=0)))
    for (n, kvh), s in zip(blocks, scores):
        rows = slice(n * W, (n + 1) * W)
        chunk = kstarts[n] // W
        diff = (base + n * W + qoff) - (kstarts[n] + key)
        s = jnp.where((diff >= 0) & (diff < W), s, -jnp.inf)
        sink = jnp.zeros((1, G * W), F32)
        for g in range(G):
            head = kvh * G + g
            sink = jnp.where(cgrp == g, sinks[:, head:head + 1], sink)
        m = jnp.maximum(jnp.max(s, axis=0, keepdims=True), sink)
        p = jnp.exp2(s - m).astype(BF16)
        acc = _dot(vt_ref[kvh, chunk], p[0:W, :]) + _dot(vt_ref[kvh, chunk + 1], p[W:2 * W, :])
        den = acc[HEAD_DIM:HEAD_DIM + 1, :] + jnp.exp2(sink - m)
        o = acc[0:HEAD_DIM, :] / den
        for pair in range(G // 2):
            both = jnp.concatenate([o[:, (2 * pair) * W:(2 * pair + 1) * W],
                                    o[:, (2 * pair + 1) * W:(2 * pair + 2) * W]], axis=0)
            lb = kvh * (G // 2) + pair
            o_ref[rows, lb * LANES:(lb + 1) * LANES] = both.T.astype(BF16)


def _swa_attention(q, kd, v, sinks):
    B, S, _ = q.shape
    tq = min(ATTN_TILE, S)
    return pl.pallas_call(
        _swa_kernel, grid=(B, S // tq),
        in_specs=[pl.BlockSpec((None, tq, SWA_DIM), lambda b, i: (b, i, 0)),
                  pl.BlockSpec((None, S, 2 * LANES), lambda b, i: (b, 0, 0)),
                  pl.BlockSpec((None, tq, LANES), lambda b, i: (b, i, 0)),
                  pl.BlockSpec((1, SWA_HEADS), lambda b, i: (0, 0))],
        out_specs=pl.BlockSpec((None, tq, SWA_DIM), lambda b, i: (b, i, 0)),
        out_shape=jax.ShapeDtypeStruct((B, S, SWA_DIM), BF16),
        scratch_shapes=[pltpu.VMEM((SWA_KV_HEADS, S // SWA_WINDOW, VT_ROWS, SWA_WINDOW), BF16)],
        compiler_params=_params("parallel", "arbitrary"), name="swa_attention",
    )(q, kd, v, sinks)


def _s5_prep_kernel(lr_ref, li_ref, ldt_ref, br_ref, bi_ref, ar_ref, ai_ref, bbr_ref, bbi_ref):
    lr, li = lr_ref[...], li_ref[...]
    dt = jnp.exp(ldt_ref[...])
    mag = jnp.exp(lr * dt)
    ar = mag * jnp.cos(li * dt)
    ai = mag * jnp.sin(li * dt)
    den = lr * lr + li * li
    cr = ((ar - 1.0) * lr + ai * li) / den
    ci = (ai * lr - (ar - 1.0) * li) / den
    br, bi = br_ref[...], bi_ref[...]
    ar_ref[...] = ar
    ai_ref[...] = ai
    bbr_ref[...] = cr * br - ci * bi
    bbi_ref[...] = cr * bi + ci * br


def _s5_prep(lam_re, lam_im, log_dt, b_re, b_im):
    G, P, C = b_re.shape
    rep = lambda a: jnp.repeat(a, C, axis=0)
    bt = lambda a: a.transpose(0, 2, 1).reshape(G * C, P)
    ldt = jnp.broadcast_to(log_dt[:, None], (G, P))
    full = pl.BlockSpec((G * C, P), lambda: (0, 0))
    outs = pl.pallas_call(
        _s5_prep_kernel, in_specs=[full] * 5, out_specs=[full] * 4,
        out_shape=[jax.ShapeDtypeStruct((G * C, P), F32)] * 4, name="s5_prep",
    )(rep(lam_re), rep(lam_im), rep(ldt), bt(b_re), bt(b_im))
    ar, ai, bbr, bbi = [o.reshape(G, C, P) for o in outs]
    return ar[:, 0], ai[:, 0], bbr, bbi


def _s5_matrices(ar, ai, bbr, bbi, c_re, c_im):
    L, GL = S5_LANE_GROUPS, S5_GROUPS // S5_LANE_GROUPS
    C, P = S5_GROUP, S5_STATE
    eye = jnp.eye(GL, dtype=F32)

    def inmap(b):
        return jnp.einsum("lgcp,gh->lgchp", b.reshape(L, GL, C, P), eye).reshape(L, GL * C, GL * P)

    def outmap(c):
        return jnp.einsum("lgcp,gh->lgphc", c.reshape(L, GL, C, P), eye).reshape(L, GL * P, GL * C)

    bmat = jnp.concatenate([inmap(bbr), inmap(bbi)], axis=2).astype(BF16)
    cmat = jnp.concatenate([outmap(c_re), outmap(-c_im)], axis=1).astype(BF16)
    a_r = ar.reshape(L, 1, GL * P)
    a_i = ai.reshape(L, 1, GL * P)
    return bmat, cmat, a_r, a_i


def _block_diag_pairs(w):
    nb, bs, _ = w.shape
    half = nb // 2
    eye = jnp.eye(half, dtype=w.dtype)
    out = jnp.einsum("thij,hk->thikj", w.reshape(2, half, bs, bs), eye)
    return out.reshape(2, half * bs, half * bs).astype(BF16)


def kernel(x, p, ffn1_norm, ffn1_wg, ffn1_wu, ffn1_wd, mix_norm, ffn2_norm, ffn2_wg, ffn2_wu, ffn2_wd, ple_w, ple_norm, ple_gate_norm, ple_gate_w, ev_w_in, lru_conv_w, lru_conv_b, lru_wa, lru_ba, lru_wx, lru_bx, lru_lambda, fox_bf, fox_q_norm, fox_k_norm, ev_w_out, od_w_in, swa_q_norm, swa_k_norm, swa_sinks, s5_lambda_re, s5_lambda_im, s5_log_dt, s5_b_re, s5_b_im, s5_c_re, s5_c_im, s5_d, s5_glu_w, s5_glu_b, od_w_out):
    B, S, D = x.shape
    depth = p.shape[0]
    assert B == SUBLANES and D == D_MODEL and S % TIME_CHUNK == 0
    T = B * S
    bf = lambda a: a.astype(BF16)
    row = lambda a: a[:, None, :]
    per_head = lambda gain, heads: jnp.tile(gain, heads)[None, :]

    f1 = (row(ffn1_norm), bf(ffn1_wg), bf(ffn1_wu), bf(ffn1_wd))
    f2 = (row(ffn2_norm), bf(ffn2_wg), bf(ffn2_wu), bf(ffn2_wd))
    ple = (p, bf(ple_w), row(ple_norm), row(ple_gate_norm), bf(ple_gate_w))
    cos, sin = _rope_tables(S)

    for i in range(depth):
        j = i // 2
        x = _ffn(x.reshape(T, D), i, *f1).reshape(B, S, D)
        g = mix_norm[i][None, :]
        if i % 2 == 0:
            w_in = bf(ev_w_in[j])
            o1, o2 = 2 * LRU_WIDTH, 2 * LRU_WIDTH + 3 * FOX_DIM
            w_f = jnp.pad(w_in[:, o2:], ((0, 0), (0, LANES - FOX_HEADS)))
            b_f = jnp.pad(fox_bf[j], (0, LANES - FOX_HEADS))[None, :]
            lru = (lru_conv_w[j], lru_conv_b[j][None, :],
                   _block_diag_pairs(lru_wa[j]), lru_ba[j][None, :],
                   _block_diag_pairs(lru_wx[j]), lru_bx[j][None, :], lru_lambda[j][None, :])
            a_out, q, k, qe, ke, v = _even_inproj(
                x, g, w_in[:, :o1], w_in[:, o1:o2], w_f, b_f,
                per_head(fox_q_norm[j], FOX_HEADS), per_head(fox_k_norm[j], FOX_HEADS), lru)
            b_out = _fox_attention(q, qe, k, ke, v)
            w_out = bf(ev_w_out[j])
            mixed = (a_out, b_out, w_out[:LRU_WIDTH], w_out[LRU_WIDTH:])
        else:
            w_in = bf(od_w_in[j])
            kvd = SWA_KV_HEADS * HEAD_DIM
            o1, o2 = SWA_DIM, SWA_DIM + 2 * kvd
            ar, ai, bbr, bbi = _s5_prep(s5_lambda_re[j], s5_lambda_im[j], s5_log_dt[j],
                                        s5_b_re[j], s5_b_im[j])
            bmat, cmat, a_r, a_i = _s5_matrices(ar, ai, bbr, bbi, s5_c_re[j], s5_c_im[j])
            s5 = (bmat, cmat, a_r, a_i, s5_d[j][None, :], bf(s5_glu_w[j]), s5_glu_b[j][None, :])
            q, kd, v, d_out = _odd_inproj(
                x, g, w_in[:, :o1], w_in[:, o1:o2], w_in[:, o2:],
                per_head(swa_q_norm[j], SWA_HEADS), per_head(swa_k_norm[j], SWA_KV_HEADS), s5,
                cos, sin)
            c_out = _swa_attention(q, kd, v, swa_sinks[j][None, :])
            w_out = bf(od_w_out[j])
            mixed = (d_out, c_out, w_out[SWA_DIM:], w_out[:SWA_DIM])
        x = _mix_ffn_ple(x, *mixed, i, f2, ple)
    return x
```

```python
import numpy as np

import jax
import jax.numpy as jnp
from jax import lax
from jax.experimental import pallas as pl
from jax.experimental.pallas import tpu as pltpu

F32 = jnp.float32
BF16 = jnp.bfloat16

D_MODEL = 1024
HEAD_DIM = 64
LRU_WIDTH = 512
LRU_CONV = 4
LRU_C = 8.0
FOX_HEADS = 8
FOX_DIM = 512
SWA_HEADS = 8
SWA_KV_HEADS = 2
SWA_GROUP = SWA_HEADS // SWA_KV_HEADS
SWA_DIM = 512
SWA_WINDOW = 128
S5_WIDTH = 512
S5_GROUP = 16
S5_GROUPS = 32
S5_STATE = 64
D_FF = 2816
PLE_DIM = 256
ROPE_THETA = 10000.0
EPS = 1e-6
MACARON = 0.5
QK_SCALE = HEAD_DIM ** -0.5
LOG2E = 1.4426950408889634

SUBLANES = 8
LANES = 128
TIME_CHUNK = 64
ROW_PITCH = TIME_CHUNK + 8
ROW_TILE = SUBLANES * TIME_CHUNK
ATTN_TILE = 512
S5_LANE_GROUPS = 4
BIAS_LANES = 6
VT_ROWS = 80
VMEM_LIMIT = 56 * 1024 * 1024


def _dot(a, b):
    return jnp.dot(a, b, preferred_element_type=F32)


def _dot_nt(a, b):
    return lax.dot_general(a, b, (((1,), (1,)), ((), ())), preferred_element_type=F32)


def _rms(x, g):
    ms = jnp.mean(x * x, axis=-1, keepdims=True)
    return x * lax.rsqrt(ms + EPS) * g


def _head_rms(x, ind, gain):
    ms = _dot((x * x).astype(BF16), ind)
    return x * lax.rsqrt(ms + EPS) * gain


def _softplus(x):
    return jnp.maximum(x, 0.0) + jnp.log1p(jnp.exp(-jnp.abs(x)))


def _log_sigmoid(x):
    return -_softplus(-x)


def _cumsum_time(x):
    n = x.shape[0]
    row = lax.broadcasted_iota(jnp.int32, x.shape, 0)
    d = SUBLANES
    while d < n:
        x = x + jnp.where(row >= d, pltpu.roll(x, d, axis=0), 0.0)
        d *= 2
    return x


def _to_time_major(val, slab_ref):
    rows, width = val.shape
    steps = rows // SUBLANES
    for s in range(width // LANES):
        for b in range(SUBLANES):
            slab_ref[s, b * ROW_PITCH:b * ROW_PITCH + steps, :] = (
                val[b * steps:(b + 1) * steps, s * LANES:(s + 1) * LANES])
    return jnp.concatenate(
        [jnp.concatenate([slab_ref[s, pl.ds(t, SUBLANES, stride=ROW_PITCH), :]
                          for s in range(width // LANES)], axis=-1) for t in range(steps)], axis=0)


def _to_batch_major(val, slab_ref):
    rows, width = val.shape
    steps = rows // SUBLANES
    for s in range(width // LANES):
        slab_ref[s, 0:rows, :] = val[:, s * LANES:(s + 1) * LANES]
    return jnp.concatenate(
        [jnp.concatenate([slab_ref[s, pl.ds(b, steps, stride=SUBLANES), :]
                          for s in range(width // LANES)], axis=-1) for b in range(SUBLANES)], axis=0)


def _split3_bf16(c):
    hi = c.astype(BF16).astype(F32)
    r = c - hi
    mid = r.astype(BF16).astype(F32)
    return hi, mid, r - mid


def _keep_lanes(x, lo, hi):
    lane = lax.broadcasted_iota(jnp.int32, x.shape, 1)
    return jnp.where((lane >= lo) & (lane < hi), x.astype(F32), 0.0).astype(BF16)


def _params(*sem):
    return pltpu.CompilerParams(dimension_semantics=sem, vmem_limit_bytes=VMEM_LIMIT)


def _const_spec(shape):
    nd = len(shape)
    return pl.BlockSpec(shape, lambda *_: (0,) * nd, pipeline_mode=pl.Buffered(1))


def _layer_spec(shape, layer):
    nd = len(shape)
    return pl.BlockSpec((None,) + tuple(shape), lambda *_: (layer,) + (0,) * nd,
                        pipeline_mode=pl.Buffered(1))


def _head_mean_matrix(width):
    h = np.arange(width) // HEAD_DIM
    return jnp.asarray((h[:, None] == h[None, :]) / HEAD_DIM, dtype=BF16)


def _swiglu_update(x, g_ref, wg_ref, wu_ref, wd_ref):
    n = _rms(x, g_ref[...]).astype(BF16)
    hg = _dot(n, wg_ref[...])
    hu = _dot(n, wu_ref[...])
    act = (hg * jax.nn.sigmoid(hg) * hu).astype(BF16)
    return x + MACARON * _dot(act, wd_ref[...])


def _ffn_kernel(x_ref, g_ref, wg_ref, wu_ref, wd_ref, o_ref):
    half = x_ref.shape[0] // 2
    for i in range(2):
        rs = slice(i * half, (i + 1) * half)
        o_ref[rs, :] = _swiglu_update(x_ref[rs, :], g_ref, wg_ref, wu_ref, wd_ref)


def _ffn(x2d, layer, norm, wg, wu, wd):
    T, D = x2d.shape
    tm = min(ROW_TILE, T)
    row = pl.BlockSpec((tm, D), lambda i: (i, 0))
    return pl.pallas_call(
        _ffn_kernel, grid=(T // tm,),
        in_specs=[row, _layer_spec((1, D), layer), _layer_spec((D, D_FF), layer),
                  _layer_spec((D, D_FF), layer), _layer_spec((D_FF, D), layer)],
        out_specs=row, out_shape=jax.ShapeDtypeStruct((T, D), F32),
        compiler_params=_params("parallel"), name="ffn",
    )(x2d, norm, wg, wu, wd)


def _mix_ffn_ple_kernel(x_ref, tmaj_ref, bmaj_ref, wt_ref, wb_ref,
                        g_ref, wg_ref, wu_ref, wd_ref, p_ref, pw_ref, pn_ref, gn_ref, gw_ref, o_ref,
                        slab_ref):
    B, ts, D = x_ref.shape
    rows = B * ts
    part_t = _to_batch_major(tmaj_ref[...].astype(F32), slab_ref).astype(BF16)
    part_b = bmaj_ref[...].reshape(rows, bmaj_ref.shape[2])
    x = x_ref[...].reshape(rows, D) + _dot(part_t, wt_ref[...]) + _dot(part_b, wb_ref[...])
    half = rows // 2
    pb = p_ref[...].reshape(rows, PLE_DIM).astype(BF16)
    halves = [_swiglu_update(x[i * half:(i + 1) * half, :], g_ref, wg_ref, wu_ref, wd_ref)
              for i in range(2)]
    outs = []
    for i, xh in enumerate(halves):
        e = _rms(_dot(pb[i * half:(i + 1) * half, :], pw_ref[...]), pn_ref[...])
        gate = jax.nn.sigmoid(_dot(_rms(xh, gn_ref[...]).astype(BF16), gw_ref[...]))
        outs.append(xh + gate * e)
    o_ref[...] = jnp.concatenate(outs, axis=0).reshape(B, ts, D)


def _mix_ffn_ple(x, part_tmaj, part_bmaj, w_tmaj, w_bmaj, layer, ffn, ple):
    B, S, D = x.shape
    ts = TIME_CHUNK
    W = part_tmaj.shape[1]
    xs = pl.BlockSpec((B, ts, D), lambda s: (0, s, 0))
    in_specs = [xs, pl.BlockSpec((B * ts, W), lambda s: (s, 0)),
                pl.BlockSpec((B, ts, W), lambda s: (0, s, 0)),
                _const_spec((W, D)), _const_spec((W, D)),
                _layer_spec((1, D), layer), _layer_spec((D, D_FF), layer),
                _layer_spec((D, D_FF), layer), _layer_spec((D_FF, D), layer),
                pl.BlockSpec((None, B, ts, PLE_DIM), lambda s: (layer, 0, s, 0)),
                _layer_spec((PLE_DIM, D), layer), _layer_spec((1, D), layer),
                _layer_spec((1, D), layer), _layer_spec((D, D), layer)]
    return pl.pallas_call(
        _mix_ffn_ple_kernel, grid=(S // ts,), in_specs=in_specs, out_specs=xs,
        out_shape=jax.ShapeDtypeStruct((B, S, D), F32),
        scratch_shapes=[pltpu.VMEM((W // LANES, B * ts, LANES), F32)],
        compiler_params=_params("parallel"), name="mix_ffn_ple",
    )(x, part_tmaj, part_bmaj, w_tmaj, w_bmaj, *ffn, *ple)


def _bias_lane_maps():
    eq = np.zeros((LANES, LANES), np.float32)
    ek = np.zeros((LANES, LANES), np.float32)
    oq = np.zeros((1, LANES), np.float32)
    ok = np.zeros((1, LANES), np.float32)
    for h in range(FOX_HEADS):
        for i in range(3):
            eq[i * FOX_HEADS + h, BIAS_LANES * h + i] = 1.0
            ek[i * FOX_HEADS + h, BIAS_LANES * h + 3 + i] = -1.0
            oq[0, BIAS_LANES * h + 3 + i] = 1.0
            ok[0, BIAS_LANES * h + i] = 1.0
    return jnp.asarray(eq, BF16), jnp.asarray(ek, BF16), jnp.asarray(oq), jnp.asarray(ok)


def _rg_lru_chunk(xa, ya, cw_ref, cb_ref, wa_ref, ba_ref, wx_ref, bx_ref, lam_ref, xprev_ref, h_ref):
    rows = xa.shape[0]
    halo = (LRU_CONV - 1) * SUBLANES
    half = LRU_WIDTH // 2
    xfull = jnp.concatenate([xprev_ref[...], xa], axis=0)
    xprev_ref[...] = xa[rows - halo:rows, :]
    xc = cb_ref[...]
    for tap in range(LRU_CONV):
        xc = xc + xfull[tap * SUBLANES:tap * SUBLANES + rows, :] * cw_ref[tap:tap + 1, :]
    xb = xc.astype(BF16)

    def gate(w_ref, b_ref):
        z = jnp.concatenate([_dot(xb[:, :half], w_ref[0]), _dot(xb[:, half:], w_ref[1])], axis=-1)
        return jax.nn.sigmoid(z + b_ref[...])

    r = gate(wa_ref, ba_ref)
    i = gate(wx_ref, bx_ref)
    log_a = -LRU_C * r * _softplus(lam_ref[...])
    a = jnp.exp(log_a)
    th = jnp.tanh(log_a)
    b = jnp.sqrt(-2.0 * th / (1.0 - th)) * (i * xc)
    h = h_ref[...]
    hs = []
    for t in range(rows // SUBLANES):
        sl = slice(t * SUBLANES, (t + 1) * SUBLANES)
        h = a[sl, :] * h + b[sl, :]
        hs.append(h)
    h_ref[...] = h
    return jax.nn.gelu(ya) * jnp.concatenate(hs, axis=0)


def _even_in_kernel(x_ref, g_ref, wxyf_ref, wqkv_ref, bf_ref, qg_ref, kg_ref, ind_ref,
                    eqk_ref, oqk_ref, cw_ref, cb_ref, wa_ref, ba_ref, wx_ref, bx_ref, lam_ref,
                    a_out_ref, q_ref, k_ref, qe_ref, ke_ref, v_ref,
                    carry_ref, slab_ref, xprev_ref, h_ref):
    B, ts, D = x_ref.shape
    rows = B * ts

    @pl.when(pl.program_id(0) == 0)
    def _():
        carry_ref[...] = jnp.zeros_like(carry_ref)
        xprev_ref[...] = jnp.zeros_like(xprev_ref)
        h_ref[...] = jnp.zeros_like(h_ref)

    n = _rms(x_ref[...].reshape(rows, D), g_ref[...]).astype(BF16)
    f_cols = _dot(n, wxyf_ref[:, 2 * LRU_WIDTH:])
    xyf = jnp.concatenate([_dot(n, wxyf_ref[:, 0:2 * LRU_WIDTH]), f_cols], axis=-1)
    qkv = _dot(n, wqkv_ref[...])
    xyf = _to_time_major(xyf, slab_ref)
    a_out_ref[...] = _rg_lru_chunk(xyf[:, :LRU_WIDTH], xyf[:, LRU_WIDTH:2 * LRU_WIDTH], cw_ref, cb_ref,
                                   wa_ref, ba_ref, wx_ref, bx_ref, lam_ref, xprev_ref,
                                   h_ref).astype(BF16)

    logf = _log_sigmoid(xyf[:, 2 * LRU_WIDTH:] + bf_ref[...]) * LOG2E
    c = _cumsum_time(logf) + jnp.concatenate([carry_ref[...]] * ts, axis=0)
    carry_ref[...] = c[rows - SUBLANES:rows, :]
    hi, mid, lo = _split3_bf16(c)
    lane = lax.broadcasted_iota(jnp.int32, c.shape, 1)
    packed = jnp.where(lane < FOX_HEADS, hi, jnp.where(
        lane < 2 * FOX_HEADS, pltpu.roll(mid, FOX_HEADS, axis=1), jnp.where(
            lane < 3 * FOX_HEADS, pltpu.roll(lo, 2 * FOX_HEADS, axis=1), 0.0)))
    packed = _to_batch_major(packed, slab_ref).astype(BF16)
    qke = (_dot(packed, eqk_ref[...]) + oqk_ref[...]).astype(BF16)
    qe_ref[...] = qke[:, :LANES].reshape(B, ts, LANES)
    ke_ref[...] = qke[:, LANES:].reshape(B, ts, LANES)

    ind = ind_ref[...]
    q = _head_rms(qkv[:, :FOX_DIM], ind, qg_ref[...]) * (QK_SCALE * LOG2E)
    k = _head_rms(qkv[:, FOX_DIM:2 * FOX_DIM], ind, kg_ref[...])
    q_ref[...] = q.astype(BF16).reshape(B, ts, FOX_DIM)
    k_ref[...] = k.astype(BF16).reshape(B, ts, FOX_DIM)
    v_ref[...] = qkv[:, 2 * FOX_DIM:].astype(BF16).reshape(B, ts, FOX_DIM)


def _even_inproj(x, g, w_xy, w_qkv, w_f, b_f, qg, kg, lru):
    B, S, D = x.shape
    ts = TIME_CHUNK
    eq, ek, oq, ok = _bias_lane_maps()
    tmaj = pl.BlockSpec((B * ts, LRU_WIDTH), lambda s: (s, 0))
    bmaj = lambda w: pl.BlockSpec((B, ts, w), lambda s: (0, s, 0))
    consts = [g, jnp.concatenate([w_xy, w_f], axis=1), w_qkv, b_f, qg, kg,
              _head_mean_matrix(FOX_DIM), jnp.concatenate([eq, ek], axis=1),
              jnp.concatenate([oq, ok], axis=1), *lru]
    return pl.pallas_call(
        _even_in_kernel, grid=(S // ts,),
        in_specs=[bmaj(D)] + [_const_spec(a.shape) for a in consts],
        out_specs=[tmaj, bmaj(FOX_DIM), bmaj(FOX_DIM), bmaj(LANES), bmaj(LANES), bmaj(FOX_DIM)],
        out_shape=[jax.ShapeDtypeStruct((S * B, LRU_WIDTH), BF16),
                   jax.ShapeDtypeStruct((B, S, FOX_DIM), BF16),
                   jax.ShapeDtypeStruct((B, S, FOX_DIM), BF16),
                   jax.ShapeDtypeStruct((B, S, LANES), BF16),
                   jax.ShapeDtypeStruct((B, S, LANES), BF16),
                   jax.ShapeDtypeStruct((B, S, FOX_DIM), BF16)],
        scratch_shapes=[pltpu.VMEM((SUBLANES, LANES), F32),
                        pltpu.VMEM((2 * LRU_WIDTH // LANES + 1, SUBLANES * ROW_PITCH, LANES), F32),
                        pltpu.VMEM(((LRU_CONV - 1) * SUBLANES, LRU_WIDTH), F32),
                        pltpu.VMEM((SUBLANES, LRU_WIDTH), F32)],
        compiler_params=_params("arbitrary"), name="even_inproj",
    )(x, *consts)


def _store_value_transpose(v, vt_ref, tile, nchunk):
    heads, _, _, chunk = vt_ref.shape
    vt = v.astype(F32).T
    ones = jnp.ones((VT_ROWS - HEAD_DIM, chunk), BF16)
    for h in range(heads):
        for c in range(nchunk):
            vt_ref[h, tile * nchunk + c, 0:HEAD_DIM, :] = vt[h * HEAD_DIM:(h + 1) * HEAD_DIM,
                                                             c * chunk:(c + 1) * chunk].astype(BF16)
            vt_ref[h, tile * nchunk + c, HEAD_DIM:VT_ROWS, :] = ones


def _fox_kernel(q_ref, qe_ref, k_ref, ke_ref, v_ref, o_ref, acc_ref, s_ref, vt_ref):
    tq = q_ref.shape[0]
    qi = pl.program_id(1)
    qe = qe_ref[...]
    qq = []
    for h in range(FOX_HEADS):
        half = h % 2
        blk = q_ref[:, (h // 2) * LANES:(h // 2 + 1) * LANES]
        qq.append(jnp.concatenate(
            [_keep_lanes(blk, half * HEAD_DIM, (half + 1) * HEAD_DIM),
             _keep_lanes(qe, BIAS_LANES * h, BIAS_LANES * (h + 1))], axis=-1))
    key = lax.broadcasted_iota(jnp.int32, (tq, tq), 0)
    qry = lax.broadcasted_iota(jnp.int32, (tq, tq), 1)
    acc_ref[...] = jnp.zeros_like(acc_ref)
    _store_value_transpose(v_ref[...], vt_ref, qi, 1)

    def update(j, m_run, masked):
        ks = pl.ds(pl.multiple_of(j * tq, tq), tq)
        ke = ke_ref[ks, :]
        for h in range(FOX_HEADS):
            kk = jnp.concatenate([k_ref[ks, (h // 2) * LANES:(h // 2 + 1) * LANES], ke], axis=-1)
            s = _dot_nt(kk, qq[h])
            if masked:
                s = jnp.where(key <= qry, s, -jnp.inf)
            s_ref[h] = s
        new = []
        for h in range(FOX_HEADS):
            m_new = jnp.maximum(m_run[h], jnp.max(s_ref[h], axis=0, keepdims=True))
            alpha = jnp.exp2(m_run[h] - m_new)
            p = jnp.exp2(s_ref[h] - m_new).astype(BF16)
            acc_ref[h] = alpha * acc_ref[h] + _dot(vt_ref[h, j], p)
            new.append(m_new)
        return tuple(new)

    init = (jnp.full((1, tq), -jnp.inf, F32),) * FOX_HEADS
    m_run = update(qi, init, True)
    lax.fori_loop(0, qi, lambda j, m: update(j, m, False), m_run)
    for pair in range(FOX_HEADS // 2):
        out = [acc_ref[h, 0:HEAD_DIM, :] / acc_ref[h, HEAD_DIM:HEAD_DIM + 1, :]
               for h in (2 * pair, 2 * pair + 1)]
        o_ref[:, pair * LANES:(pair + 1) * LANES] = jnp.concatenate(out, axis=0).T.astype(BF16)


def _fox_attention(q, qe, k, ke, v):
    B, S, _ = q.shape
    tq = min(ATTN_TILE, S)
    qblk = lambda w: pl.BlockSpec((None, tq, w), lambda b, i: (b, i, 0))
    kblk = lambda w: pl.BlockSpec((None, S, w), lambda b, i: (b, 0, 0))
    return pl.pallas_call(
        _fox_kernel, grid=(B, S // tq),
        in_specs=[qblk(FOX_DIM), qblk(LANES), kblk(FOX_DIM), kblk(LANES), qblk(FOX_DIM)],
        out_specs=qblk(FOX_DIM),
        out_shape=jax.ShapeDtypeStruct((B, S, FOX_DIM), BF16),
        scratch_shapes=[pltpu.VMEM((FOX_HEADS, VT_ROWS, tq), F32),
                        pltpu.VMEM((FOX_HEADS, tq, tq), F32),
                        pltpu.VMEM((FOX_HEADS, S // tq, VT_ROWS, tq), BF16)],
        compiler_params=_params("parallel", "arbitrary"), name="fox_attention",
    )(q, qe, k, ke, v)


def _rope_table_kernel(inv_ref, cos_ref, sin_ref):
    rows = cos_ref.shape[0]
    pos = pl.program_id(0) * rows + lax.broadcasted_iota(jnp.int32, cos_ref.shape, 0)
    lane = lax.broadcasted_iota(jnp.int32, cos_ref.shape, 1)
    ang = pos.astype(F32) * inv_ref[...]
    cos_ref[...] = jnp.cos(ang)
    sin_ref[...] = jnp.where((lane & (HEAD_DIM - 1)) < HEAD_DIM // 2, -1.0, 1.0) * jnp.sin(ang)


def _rope_tables(S):
    half = HEAD_DIM // 2
    inv = jnp.power(ROPE_THETA, -jnp.arange(half, dtype=F32) / half)
    inv4 = jnp.tile(inv, LANES // half)[None, :]
    rows = min(ROW_TILE, S)
    blk = pl.BlockSpec((rows, LANES), lambda i: (i, 0))
    return pl.pallas_call(
        _rope_table_kernel, grid=(S // rows,),
        in_specs=[pl.BlockSpec((1, LANES), lambda i: (0, 0))], out_specs=[blk, blk],
        out_shape=[jax.ShapeDtypeStruct((S, LANES), F32)] * 2,
        compiler_params=_params("parallel"), name="rope_tables",
    )(inv4)


def _rope(x, cos, sin):
    half = HEAD_DIM // 2
    lane = lax.broadcasted_iota(jnp.int32, x.shape, 1)
    first = (lane & (HEAD_DIM - 1)) < half
    swapped = jnp.where(first, pltpu.roll(x, LANES - half, axis=1), pltpu.roll(x, half, axis=1))
    return x * cos + swapped * sin


def _dup_halves(x):
    lane = lax.broadcasted_iota(jnp.int32, x.shape, 1)
    r = pltpu.roll(x, HEAD_DIM, axis=1)
    lo = lane < HEAD_DIM
    return jnp.concatenate([jnp.where(lo, x, r), jnp.where(lo, r, x)], axis=-1)


def _s5_chunk(u, bm_ref, cm_ref, ar_ref, ai_ref, d_ref, gw_ref, gb_ref, h_ref, state_ref):
    rows = u.shape[0]
    half = h_ref.shape[2] // 2
    ub = u.astype(BF16)
    for g in range(S5_LANE_GROUPS):
        h_ref[g] = _dot(ub[:, g * LANES:(g + 1) * LANES], bm_ref[g])
    ys = []
    for g in range(S5_LANE_GROUPS):
        ar = jnp.broadcast_to(ar_ref[g], (SUBLANES, half))
        ai = jnp.broadcast_to(ai_ref[g], (SUBLANES, half))
        hr = state_ref[g, :, 0:half]
        hi = state_ref[g, :, half:2 * half]
        for t in range(rows // SUBLANES):
            sl = slice(t * SUBLANES, (t + 1) * SUBLANES)
            nr = ar * hr - ai * hi + h_ref[g, sl, 0:half]
            ni = ar * hi + ai * hr + h_ref[g, sl, half:2 * half]
            h_ref[g, sl, 0:half] = nr
            h_ref[g, sl, half:2 * half] = ni
            hr, hi = nr, ni
        state_ref[g, :, 0:half] = hr
        state_ref[g, :, half:2 * half] = hi
        ys.append(_dot(h_ref[g].astype(BF16), cm_ref[g]))
    y = jnp.concatenate(ys, axis=-1) + d_ref[...] * u
    z = jax.nn.gelu(y)
    return z * jax.nn.sigmoid(_dot(z.astype(BF16), gw_ref[...]) + gb_ref[...])


def _odd_in_kernel(x_ref, g_ref, wq_ref, wkv_ref, wu_ref, qg_ref, kg_ref, ind_ref,
                   bm_ref, cm_ref, ar_ref, ai_ref, d_ref, gw_ref, gb_ref,
                   cos_ref, sin_ref, q_ref, kd_ref, v_ref, d_out_ref, slab_ref, h_ref, state_ref):
    B, ts, D = x_ref.shape
    rows = B * ts

    @pl.when(pl.program_id(0) == 0)
    def _():
        state_ref[...] = jnp.zeros_like(state_ref)

    n = _rms(x_ref[...].reshape(rows, D), g_ref[...]).astype(BF16)
    u = _to_time_major(_dot(n, wu_ref[...]), slab_ref)
    d_out_ref[...] = _s5_chunk(u, bm_ref, cm_ref, ar_ref, ai_ref, d_ref, gw_ref, gb_ref,
                               h_ref, state_ref).astype(BF16)
    cos = jnp.concatenate([cos_ref[...]] * B, axis=0)
    sin = jnp.concatenate([sin_ref[...]] * B, axis=0)
    q = _head_rms(_dot(n, wq_ref[...]), ind_ref[...], qg_ref[...])
    q = jnp.concatenate([_rope(q[:, blk * LANES:(blk + 1) * LANES], cos, sin)
                         for blk in range(SWA_DIM // LANES)], axis=-1) * (QK_SCALE * LOG2E)
    q_ref[...] = q.astype(BF16).reshape(B, ts, SWA_DIM)
    kv = _dot(n, wkv_ref[...])
    k = _head_rms(kv[:, :LANES], ind_ref[0:LANES, 0:LANES], kg_ref[...])
    kd_ref[...] = _dup_halves(_rope(k, cos, sin)).astype(BF16).reshape(B, ts, 2 * LANES)
    v_ref[...] = kv[:, LANES:].astype(BF16).reshape(B, ts, LANES)


def _odd_inproj(x, g, w_q, w_kv, w_u, qg, kg, s5, cos, sin):
    B, S, D = x.shape
    ts = TIME_CHUNK
    bmaj = lambda w: pl.BlockSpec((B, ts, w), lambda s: (0, s, 0))
    tab = pl.BlockSpec((ts, LANES), lambda s: (s, 0))
    consts = [g, w_q, w_kv, w_u, qg, kg, _head_mean_matrix(SWA_DIM), *s5]
    nstate = s5[0].shape[2]
    return pl.pallas_call(
        _odd_in_kernel, grid=(S // ts,),
        in_specs=[bmaj(D)] + [_const_spec(a.shape) for a in consts] + [tab, tab],
        out_specs=[bmaj(SWA_DIM), bmaj(2 * LANES), bmaj(LANES),
                   pl.BlockSpec((B * ts, S5_WIDTH), lambda s: (s, 0))],
        out_shape=[jax.ShapeDtypeStruct((B, S, SWA_DIM), BF16),
                   jax.ShapeDtypeStruct((B, S, 2 * LANES), BF16),
                   jax.ShapeDtypeStruct((B, S, LANES), BF16),
                   jax.ShapeDtypeStruct((S * B, S5_WIDTH), BF16)],
        scratch_shapes=[pltpu.VMEM((S5_WIDTH // LANES, SUBLANES * ROW_PITCH, LANES), F32),
                        pltpu.VMEM((S5_LANE_GROUPS, B * ts, nstate), F32),
                        pltpu.VMEM((S5_LANE_GROUPS, SUBLANES, nstate), F32)],
        compiler_params=_params("arbitrary"), name="odd_inproj",
    )(x, *consts, cos, sin)


def _swa_kernel(q_ref, kd_ref, v_ref, sink_ref, o_ref, vt_ref):
    tq = q_ref.shape[0]
    W = SWA_WINDOW
    G = SWA_GROUP
    base = pl.program_id(1) * tq
    key = lax.broadcasted_iota(jnp.int32, (2 * W, G * W), 0)
    qoff = lax.broadcasted_iota(jnp.int32, (2 * W, G * W), 1) & (W - 1)
    cgrp = lax.broadcasted_iota(jnp.int32, (1, G * W), 1) // W
    sinks = sink_ref[...] * LOG2E
    _store_value_transpose(v_ref[...], vt_ref, pl.program_id(1), tq // W)
    blocks = [(n, kvh) for n in range(tq // W) for kvh in range(SWA_KV_HEADS)]
    kstarts = [pl.multiple_of(jnp.maximum(base + n * W - W, 0), W) for n in range(tq // W)]
    scores = []
    for n, kvh in blocks:
        parts = []
        for g in range(G):
            head = kvh * G + g
            blk = q_ref[n * W:(n + 1) * W, (head // 2) * LANES:(head // 2 + 1) * LANES]
            half = head % 2
            parts.append(_keep_lanes(blk, half * HEAD_DIM, (half + 1) * HEAD_DIM))
        scores.append(_dot_nt(kd_ref[pl.ds(kstarts[n], 2 * W), kvh * LANES:(kvh + 1) * LANES],
                              jnp.concatenate(parts, axis=0)))
    for (n, kvh), s in zip(blocks, scores):
        rows = slice(n * W, (n + 1) * W)
        chunk = kstarts[n] // W
        diff = (base + n * W + qoff) - (kstarts[n] + key)
        s = jnp.where((diff >= 0) & (diff < W), s, -jnp.inf)
        sink = jnp.zeros((1, G * W), F32)
        for g in range(G):
            head = kvh * G + g
            sink = jnp.where(cgrp == g, sinks[:, head:head + 1], sink)
        m = jnp.maximum(jnp.max(s, axis=0, keepdims=True), sink)
        p = jnp.exp2(s - m).astype(BF16)
        acc = _dot(vt_ref[kvh, chunk], p[0:W, :]) + _dot(vt_ref[kvh, chunk + 1], p[W:2 * W, :])
        den = acc[HEAD_DIM:HEAD_DIM + 1, :] + jnp.exp2(sink - m)
        o = acc[0:HEAD_DIM, :] / den
        for pair in range(G // 2):
            both = jnp.concatenate([o[:, (2 * pair) * W:(2 * pair + 1) * W],
                                    o[:, (2 * pair + 1) * W:(2 * pair + 2) * W]], axis=0)
            lb = kvh * (G // 2) + pair
            o_ref[rows, lb * LANES:(lb + 1) * LANES] = both.T.astype(BF16)


def _swa_attention(q, kd, v, sinks):
    B, S, _ = q.shape
    tq = min(2 * ATTN_TILE, S)
    return pl.pallas_call(
        _swa_kernel, grid=(B, S // tq),
        in_specs=[pl.BlockSpec((None, tq, SWA_DIM), lambda b, i: (b, i, 0)),
                  pl.BlockSpec((None, S, 2 * LANES), lambda b, i: (b, 0, 0)),
                  pl.BlockSpec((None, tq, LANES), lambda b, i: (b, i, 0)),
                  pl.BlockSpec((1, SWA_HEADS), lambda b, i: (0, 0))],
        out_specs=pl.BlockSpec((None, tq, SWA_DIM), lambda b, i: (b, i, 0)),
        out_shape=jax.ShapeDtypeStruct((B, S, SWA_DIM), BF16),
        scratch_shapes=[pltpu.VMEM((SWA_KV_HEADS, S // SWA_WINDOW, VT_ROWS, SWA_WINDOW), BF16)],
        compiler_params=_params("parallel", "arbitrary"), name="swa_attention",
    )(q, kd, v, sinks)


def _s5_prep_kernel(lr_ref, li_ref, ldt_ref, br_ref, bi_ref, ar_ref, ai_ref, bbr_ref, bbi_ref):
    lr, li = lr_ref[...], li_ref[...]
    dt = jnp.exp(ldt_ref[...])
    mag = jnp.exp(lr * dt)
    ar = mag * jnp.cos(li * dt)
    ai = mag * jnp.sin(li * dt)
    den = lr * lr + li * li
    cr = ((ar - 1.0) * lr + ai * li) / den
    ci = (ai * lr - (ar - 1.0) * li) / den
    br, bi = br_ref[...], bi_ref[...]
    ar_ref[...] = ar
    ai_ref[...] = ai
    bbr_ref[...] = cr * br - ci * bi
    bbi_ref[...] = cr * bi + ci * br


def _s5_prep(lam_re, lam_im, log_dt, b_re, b_im):
    G, P, C = b_re.shape
    rep = lambda a: jnp.repeat(a, C, axis=0)
    bt = lambda a: a.transpose(0, 2, 1).reshape(G * C, P)
    ldt = jnp.broadcast_to(log_dt[:, None], (G, P))
    full = pl.BlockSpec((G * C, P), lambda: (0, 0))
    outs = pl.pallas_call(
        _s5_prep_kernel, in_specs=[full] * 5, out_specs=[full] * 4,
        out_shape=[jax.ShapeDtypeStruct((G * C, P), F32)] * 4, name="s5_prep",
    )(rep(lam_re), rep(lam_im), rep(ldt), bt(b_re), bt(b_im))
    ar, ai, bbr, bbi = [o.reshape(G, C, P) for o in outs]
    return ar[:, 0], ai[:, 0], bbr, bbi


def _s5_matrices(ar, ai, bbr, bbi, c_re, c_im):
    L, GL = S5_LANE_GROUPS, S5_GROUPS // S5_LANE_GROUPS
    C, P = S5_GROUP, S5_STATE
    eye = jnp.eye(GL, dtype=F32)

    def inmap(b):
        return jnp.einsum("lgcp,gh->lgchp", b.reshape(L, GL, C, P), eye).reshape(L, GL * C, GL * P)

    def outmap(c):
        return jnp.einsum("lgcp,gh->lgphc", c.reshape(L, GL, C, P), eye).reshape(L, GL * P, GL * C)

    bmat = jnp.concatenate([inmap(bbr), inmap(bbi)], axis=2).astype(BF16)
    cmat = jnp.concatenate([outmap(c_re), outmap(-c_im)], axis=1).astype(BF16)
    a_r = ar.reshape(L, 1, GL * P)
    a_i = ai.reshape(L, 1, GL * P)
    return bmat, cmat, a_r, a_i


def _block_diag_pairs(w):
    nb, bs, _ = w.shape
    half = nb // 2
    eye = jnp.eye(half, dtype=w.dtype)
    out = jnp.einsum("thij,hk->thikj", w.reshape(2, half, bs, bs), eye)
    return out.reshape(2, half * bs, half * bs).astype(BF16)


def kernel(x, p, ffn1_norm, ffn1_wg, ffn1_wu, ffn1_wd, mix_norm, ffn2_norm, ffn2_wg, ffn2_wu, ffn2_wd, ple_w, ple_norm, ple_gate_norm, ple_gate_w, ev_w_in, lru_conv_w, lru_conv_b, lru_wa, lru_ba, lru_wx, lru_bx, lru_lambda, fox_bf, fox_q_norm, fox_k_norm, ev_w_out, od_w_in, swa_q_norm, swa_k_norm, swa_sinks, s5_lambda_re, s5_lambda_im, s5_log_dt, s5_b_re, s5_b_im, s5_c_re, s5_c_im, s5_d, s5_glu_w, s5_glu_b, od_w_out):
    B, S, D = x.shape
    depth = p.shape[0]
    assert B == SUBLANES and D == D_MODEL and S % TIME_CHUNK == 0
    T = B * S
    bf = lambda a: a.astype(BF16)
    row = lambda a: a[:, None, :]
    per_head = lambda gain, heads: jnp.tile(gain, heads)[None, :]

    f1 = (row(ffn1_norm), bf(ffn1_wg), bf(ffn1_wu), bf(ffn1_wd))
    f2 = (row(ffn2_norm), bf(ffn2_wg), bf(ffn2_wu), bf(ffn2_wd))
    ple = (p, bf(ple_w), row(ple_norm), row(ple_gate_norm), bf(ple_gate_w))
    cos, sin = _rope_tables(S)

    for i in range(depth):
        j = i // 2
        x = _ffn(x.reshape(T, D), i, *f1).reshape(B, S, D)
        g = mix_norm[i][None, :]
        if i % 2 == 0:
            w_in = bf(ev_w_in[j])
            o1, o2 = 2 * LRU_WIDTH, 2 * LRU_WIDTH + 3 * FOX_DIM
            w_f = jnp.pad(w_in[:, o2:], ((0, 0), (0, LANES - FOX_HEADS)))
            b_f = jnp.pad(fox_bf[j], (0, LANES - FOX_HEADS))[None, :]
            lru = (lru_conv_w[j], lru_conv_b[j][None, :],
                   _block_diag_pairs(lru_wa[j]), lru_ba[j][None, :],
                   _block_diag_pairs(lru_wx[j]), lru_bx[j][None, :], lru_lambda[j][None, :])
            a_out, q, k, qe, ke, v = _even_inproj(
                x, g, w_in[:, :o1], w_in[:, o1:o2], w_f, b_f,
                per_head(fox_q_norm[j], FOX_HEADS), per_head(fox_k_norm[j], FOX_HEADS), lru)
            b_out = _fox_attention(q, qe, k, ke, v)
            w_out = bf(ev_w_out[j])
            mixed = (a_out, b_out, w_out[:LRU_WIDTH], w_out[LRU_WIDTH:])
        else:
            w_in = bf(od_w_in[j])
            kvd = SWA_KV_HEADS * HEAD_DIM
            o1, o2 = SWA_DIM, SWA_DIM + 2 * kvd
            ar, ai, bbr, bbi = _s5_prep(s5_lambda_re[j], s5_lambda_im[j], s5_log_dt[j],
                                        s5_b_re[j], s5_b_im[j])
            bmat, cmat, a_r, a_i = _s5_matrices(ar, ai, bbr, bbi, s5_c_re[j], s5_c_im[j])
            s5 = (bmat, cmat, a_r, a_i, s5_d[j][None, :], bf(s5_glu_w[j]), s5_glu_b[j][None, :])
            q, kd, v, d_out = _odd_inproj(
                x, g, w_in[:, :o1], w_in[:, o1:o2], w_in[:, o2:],
                per_head(swa_q_norm[j], SWA_HEADS), per_head(swa_k_norm[j], SWA_KV_HEADS), s5,
                cos, sin)
            c_out = _swa_attention(q, kd, v, swa_sinks[j][None, :])
            w_out = bf(od_w_out[j])
            mixed = (d_out, c_out, w_out[SWA_DIM:], w_out[:SWA_DIM])
        x = _mix_ffn_ple(x, *mixed, i, f2, ple)
    return x
```
